```python
import jax
import jax.numpy as jnp
from jax import lax
import numpy as np

D_MODEL = 4096
BATCH = 2
SEQ = 8192
DEPTH = 1

N_META = 16
RET_WIDTH = D_MODEL // 2
RET_HEADS = 8
RET_HEAD_DIM = RET_WIDTH // RET_HEADS
POOL_WIDTH = D_MODEL - RET_WIDTH
POOL_WINDOWS = (2, 4, 8, 16)
POOL_GROUPS = len(POOL_WINDOWS)
POOL_GROUP_DIM = POOL_WIDTH // POOL_GROUPS
MIX_WIDTH = RET_WIDTH + POOL_WIDTH
IN_COLS = 4 * RET_WIDTH + POOL_WIDTH
CHUNK = 128
ROPE_BASE = 10000.0
N_EXPERTS = 32
TOP_K = 4
D_FF = D_MODEL // 2
SWIGLU_LIMIT = 7.0
SWIGLU_ALPHA = 1.702
MOE_BLOCK = 128
RMS_EPS = 1e-5

kernel_name = 'hybrid_retention_pool_moe_encoder'


def rms_norm(x, w):
    xf = x.astype(jnp.float32)
    y = xf * lax.rsqrt(jnp.mean(xf * xf, axis=-1, keepdims=True) + RMS_EPS)
    return (y * w.astype(jnp.float32)).astype(x.dtype)


def rotary(t, positions):
    half = t.shape[-1] // 2
    inv_freq = ROPE_BASE ** (-jnp.arange(half, dtype=jnp.float32) / half)
    ang = positions.astype(jnp.float32)[:, None] * inv_freq[None, :]
    cos = jnp.cos(ang)[None, :, None, :]
    sin = jnp.sin(ang)[None, :, None, :]
    t1, t2 = t[..., :half], t[..., half:]
    return jnp.concatenate([t1 * cos - t2 * sin, t2 * cos + t1 * sin], axis=-1)


def retention_scan(q, k, v, log_gamma, inclusive):
    B, H, N, C, d = q.shape
    idx = jnp.arange(C, dtype=jnp.float32)
    diff = idx[:, None] - idx[None, :]
    mask = (diff >= 0) if inclusive else (diff > 0)
    inner_decay = jnp.where(mask[None], jnp.exp(log_gamma[:, None, None] * jnp.maximum(diff, 0.0)[None]), 0.0)
    q_decay = jnp.exp(log_gamma[:, None] * (idx + 1.0)[None, :])[None, :, :, None]
    k_decay = jnp.exp(log_gamma[:, None] * (C - 1.0 - idx)[None, :])[None, :, :, None]
    chunk_decay = jnp.exp(log_gamma * C)[None, :, None, None]

    def step(state, blk):
        qc, kc, vc = blk
        scores = jnp.einsum('bhnd,bhmd->bhnm', qc, kc) * inner_decay[None]
        inner = jnp.einsum('bhnm,bhmd->bhnd', scores, vc)
        cross = jnp.einsum('bhnd,bhde->bhne', qc, state) * q_decay
        state = state * chunk_decay + jnp.einsum('bhmd,bhme->bhde', kc * k_decay, vc)
        return state, inner + cross

    init = jnp.zeros((B, H, d, d), jnp.float32)
    xs = (jnp.moveaxis(q, 2, 0), jnp.moveaxis(k, 2, 0), jnp.moveaxis(v, 2, 0))
    _, out = lax.scan(step, init, xs)
    return jnp.moveaxis(out, 0, 2)


def bidirectional_retention(q, k, v, log_gamma_fwd, log_gamma_bwd):
    B, Lp, H, d = q.shape
    N = Lp // CHUNK

    def to_chunks(t):
        return t.reshape(B, N, CHUNK, H, d).transpose(0, 3, 1, 2, 4)

    def from_chunks(t):
        return t.transpose(0, 2, 3, 1, 4).reshape(B, Lp, H, d)

    def flip(t):
        return jnp.flip(t, axis=1)

    fwd = retention_scan(to_chunks(q), to_chunks(k), to_chunks(v), log_gamma_fwd, True)
    bwd = retention_scan(to_chunks(flip(q)), to_chunks(flip(k)), to_chunks(flip(v)), log_gamma_bwd, False)
    return from_chunks(fwd) + flip(from_chunks(bwd))


def multiscale_pool(u):
    B, L, _ = u.shape
    ug = u.astype(jnp.float32).reshape(B, L, POOL_GROUPS, POOL_GROUP_DIM)
    csum = jnp.pad(jnp.cumsum(ug, axis=1), ((0, 0), (1, 0), (0, 0), (0, 0)))
    pos = jnp.arange(L)
    outs = []
    for gi, w in enumerate(POOL_WINDOWS):
        lo = jnp.clip(pos - w // 2, 0, L)
        hi = jnp.clip(pos + (w - w // 2), 0, L)
        cg = csum[:, :, gi]
        cnt = (hi - lo).astype(jnp.float32)[None, :, None]
        outs.append((cg[:, hi] - cg[:, lo]) / cnt - ug[:, :, gi])
    return jnp.stack(outs, axis=2)


def hybrid_mixer(h, w_in, ret_log_rate_fwd, ret_log_rate_bwd, pool_w, pool_scale, w_out):
    B, L, _ = h.shape
    proj = h @ w_in
    q, k, v, g, u = jnp.split(proj, [RET_WIDTH, 2 * RET_WIDTH, 3 * RET_WIDTH, 4 * RET_WIDTH], axis=-1)
    positions = jnp.arange(L, dtype=jnp.int32)

    def heads(t):
        return t.astype(jnp.float32).reshape(B, L, RET_HEADS, RET_HEAD_DIM)

    qh = rotary(heads(q), positions)
    kh = rotary(heads(k), positions) * (RET_HEAD_DIM ** -0.5)
    vh = heads(v)
    pad = CHUNK - N_META

    def lead_pad(t):
        return jnp.pad(t, ((0, 0), (pad, 0), (0, 0), (0, 0)))

    log_g_f = -jnp.exp(ret_log_rate_fwd.astype(jnp.float32))
    log_g_b = -jnp.exp(ret_log_rate_bwd.astype(jnp.float32))
    r = bidirectional_retention(lead_pad(qh), lead_pad(kh), lead_pad(vh), log_g_f, log_g_b)[:, pad:]
    r = r * lax.rsqrt(jnp.mean(r * r, axis=-1, keepdims=True) + RMS_EPS)
    ret_out = jax.nn.silu(g.astype(jnp.float32)) * r.reshape(B, L, RET_WIDTH)

    pooled = multiscale_pool(u)
    pool_out = jnp.einsum('blgc,gce->blge', pooled, pool_w.astype(jnp.float32)).reshape(B, L, POOL_WIDTH)
    pool_out = pool_out * pool_scale.astype(jnp.float32)

    mixed = jnp.concatenate([ret_out, pool_out], axis=-1).astype(h.dtype)
    return mixed @ w_out


def clamped_swiglu(gu):
    gate, up = gu[..., :D_FF], gu[..., D_FF:]
    gate = jnp.minimum(gate, SWIGLU_LIMIT)
    up = jnp.clip(up, -SWIGLU_LIMIT, SWIGLU_LIMIT)
    return (up + 1.0) * gate * jax.nn.sigmoid(SWIGLU_ALPHA * gate)


def moe(h, router_w, router_b, w_gate_up, b_gate_up, w_down, b_down):
    B, L, D = h.shape
    T = B * L
    M = T * TOP_K
    xt = h.reshape(T, D)
    logits = (xt @ router_w).astype(jnp.float32) + router_b.astype(jnp.float32)
    top_vals, top_idx = lax.top_k(logits, TOP_K)
    gates = jax.nn.softmax(top_vals, axis=-1)

    flat_e = top_idx.reshape(M).astype(jnp.int32)
    flat_tok = jnp.repeat(jnp.arange(T, dtype=jnp.int32), TOP_K)
    flat_g = gates.reshape(M)
    order = jnp.argsort(flat_e)
    sorted_e = flat_e[order]
    counts = jnp.zeros((N_EXPERTS,), jnp.int32).at[flat_e].add(1)
    padded = (counts + MOE_BLOCK - 1) // MOE_BLOCK * MOE_BLOCK
    start = jnp.cumsum(counts) - counts
    pend = jnp.cumsum(padded)
    pstart = pend - padded
    rank = jnp.arange(M, dtype=jnp.int32) - start[sorted_e]
    dest = pstart[sorted_e] + rank

    n_blocks = -(-(M + N_EXPERTS * (MOE_BLOCK - 1)) // MOE_BLOCK)
    rows = n_blocks * MOE_BLOCK
    tok_buf = jnp.zeros((rows,), jnp.int32).at[dest].set(flat_tok[order])
    gate_buf = jnp.zeros((rows,), jnp.float32).at[dest].set(flat_g[order])
    block_starts = jnp.arange(n_blocks, dtype=jnp.int32) * MOE_BLOCK
    block_e = jnp.minimum(jnp.searchsorted(pend, block_starts, side='right'), N_EXPERTS - 1)

    def step(acc, blk):
        e, toks, gw = blk
        xb = xt[toks]
        gu = xb @ w_gate_up[e] + b_gate_up[e]
        y = clamped_swiglu(gu) @ w_down[e] + b_down[e]
        acc = acc.at[toks].add(y.astype(jnp.float32) * gw[:, None])
        return acc, None

    acc, _ = lax.scan(step, jnp.zeros((T, D), jnp.float32),
                      (block_e, tok_buf.reshape(n_blocks, MOE_BLOCK), gate_buf.reshape(n_blocks, MOE_BLOCK)))
    return acc.astype(h.dtype).reshape(B, L, D)


def setup_inputs(seed: int = 0) -> dict:
    key = jax.random.key(seed)
    ks = jax.random.split(key, 17)
    f32 = jnp.float32

    def normal(k, shape, scale):
        return jax.random.normal(k, shape, f32) * scale

    base_rate = jnp.log(-jnp.log1p(-jnp.exp2(-5.0 - jnp.arange(RET_HEADS, dtype=f32))))
    return {
        'x': normal(ks[0], (BATCH, SEQ, D_MODEL), 1.0),
        'meta_tokens': normal(ks[1], (N_META, D_MODEL), 1.0),
        'norm_mix_w': 1.0 + normal(ks[2], (DEPTH, D_MODEL), 0.02),
        'w_in': normal(ks[3], (DEPTH, D_MODEL, IN_COLS), D_MODEL ** -0.5),
        'ret_log_rate_fwd': base_rate[None, :] + normal(ks[4], (DEPTH, RET_HEADS), 0.05),
        'ret_log_rate_bwd': base_rate[None, :] + normal(ks[5], (DEPTH, RET_HEADS), 0.05),
        'pool_w': normal(ks[6], (DEPTH, POOL_GROUPS, POOL_GROUP_DIM, POOL_GROUP_DIM), POOL_GROUP_DIM ** -0.5),
        'pool_scale': 1.0 + normal(ks[7], (DEPTH, POOL_WIDTH), 0.02),
        'w_out': normal(ks[8], (DEPTH, MIX_WIDTH, D_MODEL), MIX_WIDTH ** -0.5),
        'norm_ffn_w': 1.0 + normal(ks[9], (DEPTH, D_MODEL), 0.02),
        'router_w': normal(ks[10], (DEPTH, D_MODEL, N_EXPERTS), D_MODEL ** -0.5),
        'router_b': normal(ks[11], (DEPTH, N_EXPERTS), 0.01),
        'w_gate_up': normal(ks[12], (DEPTH, N_EXPERTS, D_MODEL, 2 * D_FF), D_MODEL ** -0.5),
        'b_gate_up': normal(ks[13], (DEPTH, N_EXPERTS, 2 * D_FF), 0.01),
        'w_down': normal(ks[14], (DEPTH, N_EXPERTS, D_FF, D_MODEL), D_FF ** -0.5),
        'b_down': normal(ks[15], (DEPTH, N_EXPERTS, D_MODEL), 0.01),
        'final_norm_w': 1.0 + normal(ks[16], (D_MODEL,), 0.02),
    }


def reference(x, meta_tokens, norm_mix_w, w_in, ret_log_rate_fwd, ret_log_rate_bwd, pool_w, pool_scale,
              w_out, norm_ffn_w, router_w, router_b, w_gate_up, b_gate_up, w_down, b_down, final_norm_w):
    B = x.shape[0]
    meta = jnp.broadcast_to(meta_tokens.astype(x.dtype)[None], (B, N_META, D_MODEL))
    hs = jnp.concatenate([meta, x], axis=1)
    for layer in range(DEPTH):
        a = rms_norm(hs, norm_mix_w[layer])
        hs = hs + hybrid_mixer(a, w_in[layer], ret_log_rate_fwd[layer], ret_log_rate_bwd[layer],
                               pool_w[layer], pool_scale[layer], w_out[layer])
        f = rms_norm(hs, norm_ffn_w[layer])
        hs = hs + moe(f, router_w[layer], router_b[layer], w_gate_up[layer], b_gate_up[layer],
                      w_down[layer], b_down[layer])
    return rms_norm(hs, final_norm_w)[:, N_META:]
```

```python
import functools

import jax
import jax.numpy as jnp
from jax import lax
from jax.experimental import pallas as pl
from jax.experimental.pallas import tpu as pltpu

N_META = 16
CHUNK = 128
PAD = CHUNK - N_META
POOL_WINDOWS = (2, 4, 8, 16)
POOL_HALO = 64
TOP_K = 4
SWIGLU_LIMIT = 7.0
SWIGLU_ALPHA = 1.702
RMS_EPS = 1e-5
ROPE_BASE = 10000.0

V7X_VMEM_BYTES = 64 * 1024 * 1024
VMEM_LIMIT = V7X_VMEM_BYTES - 8 * 1024 * 1024

F32 = jnp.float32
BF16 = jnp.bfloat16

DEFAULT_TILES = dict(
    in_tm=640, in_tn=1024, norm_rows=32,
    ret_group_chunks=13,
    pool_tm=640,
    out_tm=640, out_tn=1024,
    router_tm=256,
    moe_tm=256, gu_tn=512, down_tn=2048,
    token_tile=128,
)


def _params(semantics):
    return pltpu.CompilerParams(dimension_semantics=semantics, vmem_limit_bytes=VMEM_LIMIT)


def _in_proj_kernel(x_ref, nw_ref, w_ref, o_ref, a_scr, *, sub):
    @pl.when(pl.program_id(1) == 0)
    def _():
        def body(r, c):
            rows = pl.ds(pl.multiple_of(r * sub, sub), sub)
            x = x_ref[rows, :]
            ms = jnp.mean(x * x, axis=-1, keepdims=True)
            a_scr[rows, :] = (x * lax.rsqrt(ms + RMS_EPS) * nw_ref[...]).astype(a_scr.dtype)
            return c
        lax.fori_loop(0, x_ref.shape[0] // sub, body, 0)

    o_ref[...] = jnp.dot(a_scr[...], w_ref[...], preferred_element_type=F32).astype(o_ref.dtype)


def _in_proj(hs, norm_w, w_in_bf16, *, tm, tn, sub):
    T, D = hs.shape
    N = w_in_bf16.shape[1]
    return pl.pallas_call(
        functools.partial(_in_proj_kernel, sub=sub),
        out_shape=jax.ShapeDtypeStruct((T, N), BF16),
        grid=(T // tm, N // tn),
        in_specs=[
            pl.BlockSpec((tm, D), lambda i, j: (i, 0)),
            pl.BlockSpec((1, D), lambda i, j: (0, 0)),
            pl.BlockSpec((D, tn), lambda i, j: (0, j)),
        ],
        out_specs=pl.BlockSpec((tm, tn), lambda i, j: (i, j)),
        scratch_shapes=[pltpu.VMEM((tm, D), BF16)],
        compiler_params=_params(("arbitrary", "arbitrary")),
        name="in_proj",
    )(hs, norm_w.reshape(1, D), w_in_bf16)


def _retention_kernel(lg_ref, q_ref, k_ref, v_ref, g_ref, cos_ref, sin_ref, o_ref,
                      acc_scr, sf_scr, sb_scr, dmat_scr, dvec_scr, *, G, GC):
    h = pl.program_id(1)
    s = pl.program_id(2)
    phase = s // G
    w = s % G
    gi = w + phase * (G - 1 - 2 * w)
    C = CHUNK
    d = q_ref.shape[1]
    half = d // 2
    GR = GC * C
    lgf = lg_ref[0, h]
    lgb = lg_ref[1, h]
    k_scale = d ** -0.5
    nt = (((1,), (1,)), ((), ()))
    tn = (((0,), (0,)), ((), ()))

    @pl.when(s == 0)
    def _():
        t = lax.broadcasted_iota(jnp.int32, (C, C), 0).astype(F32)
        u = lax.broadcasted_iota(jnp.int32, (C, C), 1).astype(F32)
        diff = t - u
        dmat_scr[...] = jnp.where(diff >= 0.0, jnp.exp(lgf * jnp.maximum(diff, 0.0)),
                                  jnp.exp(lgb * jnp.maximum(-diff, 0.0)))
        idx = lax.broadcasted_iota(jnp.int32, (C, d), 0).astype(F32)
        dvec_scr[0] = jnp.exp(lgf * (idx + 1.0))
        dvec_scr[1] = jnp.exp(lgf * (C - 1.0 - idx))
        dvec_scr[2] = jnp.exp(lgb * (C - idx))
        dvec_scr[3] = jnp.exp(lgb * idx)
        dvec_scr[4] = jnp.exp(jnp.full((C, d), lgf * C, F32))
        dvec_scr[5] = jnp.exp(jnp.full((C, d), lgb * C, F32))
        sf_scr[...] = jnp.zeros_like(sf_scr)

    @pl.when(s == G)
    def _():
        sb_scr[...] = jnp.zeros_like(sb_scr)

    def rotary(ref, rows):
        x = ref[rows, :].astype(F32)
        x1 = x[:, :half]
        x2 = x[:, half:]
        cs = cos_ref[rows, :]
        sn = sin_ref[rows, :]
        return jnp.concatenate([x1 * cs - x2 * sn, x2 * cs + x1 * sn], axis=-1)

    @pl.when(phase == 0)
    def _():
        def body(c, carry):
            rows = pl.ds(pl.multiple_of(c * C, C), C)
            arows = pl.ds(pl.multiple_of(gi * GR + c * C, C), C)
            qc = rotary(q_ref, rows).astype(BF16)
            kf = rotary(k_ref, rows) * k_scale
            vc = v_ref[rows, :]
            sc = lax.dot_general(qc, kf.astype(BF16), nt, preferred_element_type=F32)
            p = (sc * dmat_scr[...]).astype(BF16)
            inner = jnp.dot(p, vc, preferred_element_type=F32)
            sf = sf_scr[...]
            cross = jnp.dot(qc, sf.astype(BF16), preferred_element_type=F32) * dvec_scr[0]
            acc_scr[arows, :] = inner + cross
            kd = (kf * dvec_scr[1]).astype(BF16)
            sf_scr[...] = sf * dvec_scr[4, 0:1, :] + lax.dot_general(kd, vc, tn, preferred_element_type=F32)
            return carry
        lax.fori_loop(0, GC, body, 0)

    @pl.when(phase == 1)
    def _():
        def body(cc, carry):
            c = GC - 1 - cc
            rows = pl.ds(pl.multiple_of(c * C, C), C)
            arows = pl.ds(pl.multiple_of(gi * GR + c * C, C), C)
            qc = rotary(q_ref, rows).astype(BF16)
            kf = rotary(k_ref, rows) * k_scale
            vc = v_ref[rows, :]
            sb = sb_scr[...]
            r = acc_scr[arows, :] + jnp.dot(qc, sb.astype(BF16), preferred_element_type=F32) * dvec_scr[2]
            kd = (kf * dvec_scr[3]).astype(BF16)
            sb_scr[...] = sb * dvec_scr[5, 0:1, :] + lax.dot_general(kd, vc, tn, preferred_element_type=F32)
            rn = r * lax.rsqrt(jnp.mean(r * r, axis=-1, keepdims=True) + RMS_EPS)
            gg = g_ref[rows, :].astype(F32)
            o_ref[rows, :] = (gg * jax.nn.sigmoid(gg) * rn).astype(o_ref.dtype)
            return carry
        lax.fori_loop(0, GC, body, 0)


def _retention(proj, log_gamma, cos_tab, sin_tab, *, B, LP, H, d, GC):
    T = proj.shape[0]
    GR = GC * CHUNK
    G = LP // GR

    def grp(s):
        phase = s // G
        w = s % G
        return phase, w + phase * (G - 1 - 2 * w)

    def qkv_map(col0):
        def m(b, h, s):
            _, gi = grp(s)
            return (b * G + gi, col0 + h)
        return m

    def late_map(col0):
        def m(b, h, s):
            phase, gi = grp(s)
            return (b * G + jnp.where(phase == 0, G - 1, gi), col0 + h)
        return m

    def tab_map(b, h, s):
        _, gi = grp(s)
        return (gi, 0)

    return pl.pallas_call(
        functools.partial(_retention_kernel, G=G, GC=GC),
        out_shape=jax.ShapeDtypeStruct((T, H * d), BF16),
        grid=(B, H, 2 * G),
        in_specs=[
            pl.BlockSpec(memory_space=pltpu.SMEM),
            pl.BlockSpec((GR, d), qkv_map(0)),
            pl.BlockSpec((GR, d), qkv_map(H)),
            pl.BlockSpec((GR, d), qkv_map(2 * H)),
            pl.BlockSpec((GR, d), late_map(3 * H)),
            pl.BlockSpec((GR, d // 2), tab_map),
            pl.BlockSpec((GR, d // 2), tab_map),
        ],
        out_specs=pl.BlockSpec((GR, d), late_map(0)),
        scratch_shapes=[
            pltpu.VMEM((LP, d), F32),
            pltpu.VMEM((d, d), F32),
            pltpu.VMEM((d, d), F32),
            pltpu.VMEM((CHUNK, CHUNK), F32),
            pltpu.VMEM((6, CHUNK, d), F32),
        ],
        compiler_params=_params(("arbitrary", "arbitrary", "arbitrary")),
        name="retention",
    )(log_gamma, proj, proj, proj, proj, cos_tab, sin_tab)


def _pool_kernel(prev_ref, cur_ref, next_ref, pw_ref, ps_ref, o_ref, *, tiles_per_seq, seq_len):
    i = pl.program_id(0)
    tm = cur_ref.shape[0]
    ext = tm + 2 * POOL_HALO
    n_groups = pw_ref.shape[0]
    cg = pw_ref.shape[1]
    row0 = (i % tiles_per_seq) * tm - PAD
    pos_t = row0 + lax.broadcasted_iota(jnp.int32, (tm, ext), 0)
    pos_s = row0 - POOL_HALO + lax.broadcasted_iota(jnp.int32, (tm, ext), 1)
    pos_c = row0 + lax.broadcasted_iota(jnp.int32, (tm, 1), 0)
    for gi in range(n_groups):
        win = POOL_WINDOWS[gi]
        cols = slice(gi * cg, (gi + 1) * cg)
        lo = jnp.clip(pos_t - win // 2, 0, seq_len)
        hi = jnp.clip(pos_t + (win - win // 2), 0, seq_len)
        band = jnp.where(pos_s >= lo, jnp.where(pos_s < hi, 1.0, 0.0), 0.0).astype(BF16)
        cnt = (jnp.clip(pos_c + (win - win // 2), 0, seq_len) - jnp.clip(pos_c - win // 2, 0, seq_len)).astype(F32)
        inv = jnp.where(pos_c >= 0, 1.0 / jnp.maximum(cnt, 1.0), 0.0)
        u_cur = cur_ref[:, cols]
        u_ext = jnp.concatenate([prev_ref[:, cols], u_cur, next_ref[:, cols]], axis=0)
        wsum = jnp.dot(band, u_ext, preferred_element_type=F32)
        pooled = wsum * inv - u_cur.astype(F32)
        y = jnp.dot(pooled.astype(BF16), pw_ref[gi], preferred_element_type=F32)
        o_ref[:, cols] = (y * ps_ref[:, cols]).astype(o_ref.dtype)


def _pool(proj, pool_w_bf16, pool_scale, *, LP, seq_len, u_col0, tm):
    T = proj.shape[0]
    n_groups, cg, _ = pool_w_bf16.shape
    PW = n_groups * cg
    col_blk = u_col0 // PW
    hb = tm // POOL_HALO
    n_halo = T // POOL_HALO
    return pl.pallas_call(
        functools.partial(_pool_kernel, tiles_per_seq=LP // tm, seq_len=seq_len),
        out_shape=jax.ShapeDtypeStruct((T, PW), BF16),
        grid=(T // tm,),
        in_specs=[
            pl.BlockSpec((POOL_HALO, PW), lambda i: (jnp.maximum(i * hb - 1, 0), col_blk)),
            pl.BlockSpec((tm, PW), lambda i: (i, col_blk)),
            pl.BlockSpec((POOL_HALO, PW), lambda i: (jnp.minimum((i + 1) * hb, n_halo - 1), col_blk)),
            pl.BlockSpec((n_groups, cg, cg), lambda i: (0, 0, 0)),
            pl.BlockSpec((1, PW), lambda i: (0, 0)),
        ],
        out_specs=pl.BlockSpec((tm, PW), lambda i: (i, 0)),
        compiler_params=_params(("arbitrary",)),
        name="pool",
    )(proj, proj, proj, pool_w_bf16, pool_scale.reshape(1, PW))


def _out_proj_kernel(ret_ref, pool_ref, wr_ref, wp_ref, hs_ref, o_ref):
    y = jnp.dot(ret_ref[...], wr_ref[...], preferred_element_type=F32)
    y = y + jnp.dot(pool_ref[...], wp_ref[...], preferred_element_type=F32)
    o_ref[...] = hs_ref[...] + y


def _out_proj(ret, pool, w_out_bf16, hs, *, tm, tn):
    T, RW = ret.shape
    PW = pool.shape[1]
    D = hs.shape[1]
    assert RW == PW, "the two mixer halves share one row-block split of w_out"
    return pl.pallas_call(
        _out_proj_kernel,
        out_shape=jax.ShapeDtypeStruct((T, D), F32),
        grid=(T // tm, D // tn),
        in_specs=[
            pl.BlockSpec((tm, RW), lambda i, j: (i, 0)),
            pl.BlockSpec((tm, PW), lambda i, j: (i, 0)),
            pl.BlockSpec((RW, tn), lambda i, j: (0, j)),
            pl.BlockSpec((PW, tn), lambda i, j: (1, j)),
            pl.BlockSpec((tm, tn), lambda i, j: (i, j)),
        ],
        out_specs=pl.BlockSpec((tm, tn), lambda i, j: (i, j)),
        compiler_params=_params(("arbitrary", "arbitrary")),
        name="out_proj",
    )(ret, pool, w_out_bf16, w_out_bf16, hs)


def _pack_bf16_pair(lo, hi):
    lo_bits = lax.bitcast_convert_type(lo.astype(BF16).astype(F32), jnp.uint32)
    hi_bits = lax.bitcast_convert_type(hi.astype(BF16).astype(F32), jnp.uint32)
    return (hi_bits & jnp.uint32(0xFFFF0000)) | (lo_bits >> jnp.uint32(16))


def _unpack_bf16_pair(word):
    lo = lax.bitcast_convert_type(word << jnp.uint32(16), F32).astype(BF16)
    hi = lax.bitcast_convert_type(word & jnp.uint32(0xFFFF0000), F32).astype(BF16)
    return lo, hi


def _router_kernel(hs_ref, nw_ref, rw_ref, rwhi_ref, rb_ref, valid_ref,
                   fp_ref, eid_ref, gate_ref, rank_ref, cnt_ref,
                   fhi_scr, flo_scr, carry_scr, *, sub):
    i = pl.program_id(0)
    tm, D = hs_ref.shape
    E = rb_ref.shape[0]
    half = D // 2

    @pl.when(i == 0)
    def _():
        carry_scr[...] = jnp.zeros_like(carry_scr)

    def body(r, c):
        rows = pl.ds(pl.multiple_of(r * sub, sub), sub)
        x = hs_ref[rows, :]
        ms = jnp.mean(x * x, axis=-1, keepdims=True)
        f = x * lax.rsqrt(ms + RMS_EPS) * nw_ref[...]
        f_hi = f.astype(BF16)
        fhi_scr[rows, :] = f_hi
        flo_scr[rows, :] = (f - f_hi.astype(F32)).astype(BF16)
        fp_ref[rows, :] = _pack_bf16_pair(f[:, :half], f[:, half:])
        return c
    lax.fori_loop(0, tm // sub, body, 0)

    nt = (((1,), (1,)), ((), ()))
    part = lax.dot_general(rw_ref[...], fhi_scr[...], nt, preferred_element_type=F32)
    logits = part[:E] + part[E:] + lax.dot_general(rwhi_ref[...], flo_scr[...], nt, preferred_element_type=F32)
    logits = logits + rb_ref[...]

    eio = lax.broadcasted_iota(jnp.int32, (E, tm), 0).astype(F32)
    valid = valid_ref[...]

    vals, onehots = [], []
    l = logits
    for k in range(TOP_K):
        m = jnp.max(l, axis=0, keepdims=True)
        idx = jnp.min(jnp.where(l == m, eio, float(E)), axis=0, keepdims=True)
        sel = eio == idx
        vals.append(m)
        onehots.append(jnp.where(sel, valid, 0.0))
        eid_ref[pl.ds(k, 1), :] = idx.astype(jnp.int32)
        l = jnp.where(sel, -jnp.inf, l)

    exps = [jnp.exp(v - vals[0]) for v in vals]
    denom = exps[0] + exps[1] + exps[2] + exps[3]
    for k in range(TOP_K):
        gate_ref[pl.ds(k, 1), :] = exps[k] / denom

    oh = onehots[0] + onehots[1] + onehots[2] + onehots[3]
    tt = lax.broadcasted_iota(jnp.int32, (tm, tm), 0)
    uu = lax.broadcasted_iota(jnp.int32, (tm, tm), 1)
    upper = jnp.where(tt < uu, 1.0, 0.0).astype(BF16)
    carry = carry_scr[...]
    before = jnp.dot(oh.astype(BF16), upper, preferred_element_type=F32) + carry[:, 0:1]
    for k in range(TOP_K):
        rank_ref[pl.ds(k, 1), :] = jnp.sum(onehots[k] * before, axis=0, keepdims=True).astype(jnp.int32)
    carry = carry + jnp.sum(oh, axis=1, keepdims=True)
    carry_scr[...] = carry
    cnt_ref[...] = carry


def _router(hs1, norm_w, router_w, router_b, is_token, *, tm, sub):
    T, D = hs1.shape
    E = router_w.shape[1]
    rw_t = router_w.T.astype(F32)
    rw_hi = rw_t.astype(BF16)
    rw_lo = (rw_t - rw_hi.astype(F32)).astype(BF16)
    rw_stack = jnp.concatenate([rw_hi, rw_lo], axis=0)
    n = T // tm
    return pl.pallas_call(
        functools.partial(_router_kernel, sub=sub),
        out_shape=(
            jax.ShapeDtypeStruct((T, D // 2), jnp.uint32),
            jax.ShapeDtypeStruct((TOP_K, T), jnp.int32),
            jax.ShapeDtypeStruct((TOP_K, T), F32),
            jax.ShapeDtypeStruct((TOP_K, T), jnp.int32),
            jax.ShapeDtypeStruct((E, CHUNK), F32),
        ),
        grid=(n,),
        in_specs=[
            pl.BlockSpec((tm, D), lambda i: (i, 0)),
            pl.BlockSpec((1, D), lambda i: (0, 0)),
            pl.BlockSpec((2 * E, D), lambda i: (0, 0)),
            pl.BlockSpec((E, D), lambda i: (0, 0)),
            pl.BlockSpec((E, 1), lambda i: (0, 0)),
            pl.BlockSpec((1, tm), lambda i: (0, i)),
        ],
        out_specs=(
            pl.BlockSpec((tm, D // 2), lambda i: (i, 0)),
            pl.BlockSpec((TOP_K, tm), lambda i: (0, i)),
            pl.BlockSpec((TOP_K, tm), lambda i: (0, i)),
            pl.BlockSpec((TOP_K, tm), lambda i: (0, i)),
            pl.BlockSpec((E, CHUNK), lambda i: (0, 0)),
        ),
        scratch_shapes=[pltpu.VMEM((tm, D), BF16), pltpu.VMEM((tm, D), BF16), pltpu.VMEM((E, CHUNK), F32)],
        compiler_params=_params(("arbitrary",)),
        name="router",
    )(hs1, norm_w.reshape(1, D), rw_stack, rw_hi, router_b.reshape(E, 1).astype(F32),
      is_token.astype(F32).reshape(1, T))


def _dispatch_kernel(pend_ref, padded_ref, nu_ref, dest_ref, fp_ref, zeros_ref, xs_hbm, zsem, sem, *, moe_tm):
    i = pl.program_id(0)
    tm = fp_ref.shape[0]
    E = pend_ref.shape[0]
    n_blocks = xs_hbm.shape[0] // moe_tm

    def zero_rows(start):
        return pltpu.make_async_copy(zeros_ref, xs_hbm.at[pl.ds(pl.multiple_of(start, moe_tm), moe_tm)], zsem)

    @pl.when(i == 0)
    def _():
        for e in range(E):
            @pl.when(padded_ref[e] > 0)
            def _():
                zero_rows(pend_ref[e] - moe_tm).start()

        def start_tail(b, c):
            zero_rows(b * moe_tm).start()
            return c
        lax.fori_loop(nu_ref[0], n_blocks, start_tail, 0)
        for e in range(E):
            @pl.when(padded_ref[e] > 0)
            def _():
                zero_rows(pend_ref[e] - moe_tm).wait()

        def wait_tail(b, c):
            zero_rows(b * moe_tm).wait()
            return c
        lax.fori_loop(nu_ref[0], n_blocks, wait_tail, 0)

    def body(r, c):
        for k in range(TOP_K):
            pltpu.make_async_copy(fp_ref.at[pl.ds(r, 1)], xs_hbm.at[pl.ds(dest_ref[k, r], 1)], sem).start()
        return c
    lax.fori_loop(0, tm, body, 0)
    for k in range(TOP_K):
        pltpu.make_async_copy(fp_ref, xs_hbm.at[pl.ds(0, tm)], sem).wait()


def _dispatch(fp, dest3, pend, padded, n_used, *, rows_total, moe_tm, tile):
    T, W = fp.shape
    assert rows_total % moe_tm == 0
    zeros = jnp.zeros((moe_tm, W), fp.dtype)
    grid_spec = pltpu.PrefetchScalarGridSpec(
        num_scalar_prefetch=3,
        grid=(T // tile,),
        in_specs=[
            pl.BlockSpec((None, TOP_K, tile), lambda i, *_: (i, 0, 0), memory_space=pltpu.SMEM),
            pl.BlockSpec((tile, W), lambda i, *_: (i, 0)),
            pl.BlockSpec((moe_tm, W), lambda i, *_: (0, 0)),
        ],
        out_specs=pl.BlockSpec(memory_space=pl.ANY),
        scratch_shapes=[pltpu.SemaphoreType.DMA, pltpu.SemaphoreType.DMA],
    )
    return pl.pallas_call(
        functools.partial(_dispatch_kernel, moe_tm=moe_tm),
        out_shape=jax.ShapeDtypeStruct((rows_total, W), fp.dtype),
        grid_spec=grid_spec,
        compiler_params=_params(("arbitrary",)),
        name="dispatch",
    )(pend, padded, n_used, dest3, fp, zeros)


def _cast_rows(src_ref, dst_ref, rows_per_step):
    def body(r, c):
        rows = pl.ds(pl.multiple_of(r * rows_per_step, rows_per_step), rows_per_step)
        dst_ref[rows, :] = src_ref[rows, :].astype(dst_ref.dtype)
        return c
    lax.fori_loop(0, src_ref.shape[0] // rows_per_step, body, 0)


def _expert_changed(te_ref, i):
    prev = te_ref[jnp.maximum(i - 1, 0)]
    return jnp.logical_or(i == 0, te_ref[i] != prev)


def _gate_up_kernel(te_ref, nu_ref, xs_ref, wg_ref, wu_ref, bg_ref, bu_ref, h_ref, wg_scr, wu_scr):
    i = pl.program_id(1)
    active = i < nu_ref[0]

    @pl.when(jnp.logical_and(active, _expert_changed(te_ref, i)))
    def _():
        _cast_rows(wg_ref, wg_scr, 256)
        _cast_rows(wu_ref, wu_scr, 256)

    @pl.when(active)
    def _():
        lo, hi = _unpack_bf16_pair(xs_ref[...])
        x = jnp.concatenate([lo, hi], axis=-1)
        gate = jnp.dot(x, wg_scr[...], preferred_element_type=F32) + bg_ref[...]
        up = jnp.dot(x, wu_scr[...], preferred_element_type=F32) + bu_ref[...]
        gate = jnp.minimum(gate, SWIGLU_LIMIT)
        up = jnp.clip(up, -SWIGLU_LIMIT, SWIGLU_LIMIT)
        h_ref[...] = ((up + 1.0) * gate * jax.nn.sigmoid(SWIGLU_ALPHA * gate)).astype(h_ref.dtype)

    @pl.when(jnp.logical_not(active))
    def _():
        h_ref[...] = jnp.zeros_like(h_ref)


def _gate_up(xs, tile_e, n_used, w_gate_up, b_gate_up, *, n_tiles, tm, tn):
    E, D, F2 = w_gate_up.shape
    FF = F2 // 2
    nj = FF // tn
    W = xs.shape[1]

    def row(i, te, nu):
        return jnp.minimum(i, nu[0] - 1)

    grid_spec = pltpu.PrefetchScalarGridSpec(
        num_scalar_prefetch=2,
        grid=(nj, n_tiles),
        in_specs=[
            pl.BlockSpec((tm, W), lambda j, i, te, nu: (row(i, te, nu), 0)),
            pl.BlockSpec((None, D, tn), lambda j, i, te, nu: (te[row(i, te, nu)], 0, j)),
            pl.BlockSpec((None, D, tn), lambda j, i, te, nu: (te[row(i, te, nu)], 0, nj + j)),
            pl.BlockSpec((None, 1, tn), lambda j, i, te, nu: (te[row(i, te, nu)], 0, j)),
            pl.BlockSpec((None, 1, tn), lambda j, i, te, nu: (te[row(i, te, nu)], 0, nj + j)),
        ],
        out_specs=pl.BlockSpec((tm, tn), lambda j, i, te, nu: (i, j)),
        scratch_shapes=[pltpu.VMEM((D, tn), BF16), pltpu.VMEM((D, tn), BF16)],
    )
    b3 = b_gate_up.reshape(E, 1, F2)
    return pl.pallas_call(
        _gate_up_kernel,
        out_shape=jax.ShapeDtypeStruct((n_tiles * tm, FF), BF16),
        grid_spec=grid_spec,
        compiler_params=_params(("arbitrary", "arbitrary")),
        name="gate_up",
    )(tile_e, n_used, xs, w_gate_up, w_gate_up, b3, b3)


def _down_kernel(te_ref, nu_ref, h_ref, w_ref, b_ref, y_ref, w_scr):
    i = pl.program_id(1)
    active = i < nu_ref[0]

    @pl.when(jnp.logical_and(active, _expert_changed(te_ref, i)))
    def _():
        _cast_rows(w_ref, w_scr, 256)

    @pl.when(active)
    def _():
        y_ref[...] = jnp.dot(h_ref[...], w_scr[...], preferred_element_type=F32) + b_ref[...]

    @pl.when(jnp.logical_not(active))
    def _():
        y_ref[...] = jnp.zeros_like(y_ref)


def _down(h, tile_e, n_used, w_down, b_down, *, n_tiles, tm, tn):
    E, FF, D = w_down.shape

    def row(i, te, nu):
        return jnp.minimum(i, nu[0] - 1)

    grid_spec = pltpu.PrefetchScalarGridSpec(
        num_scalar_prefetch=2,
        grid=(D // tn, n_tiles),
        in_specs=[
            pl.BlockSpec((tm, FF), lambda j, i, te, nu: (row(i, te, nu), 0)),
            pl.BlockSpec((None, FF, tn), lambda j, i, te, nu: (te[row(i, te, nu)], 0, j)),
            pl.BlockSpec((None, 1, tn), lambda j, i, te, nu: (te[row(i, te, nu)], 0, j)),
        ],
        out_specs=pl.BlockSpec((tm, tn), lambda j, i, te, nu: (i, j)),
        scratch_shapes=[pltpu.VMEM((FF, tn), BF16)],
    )
    return pl.pallas_call(
        _down_kernel,
        out_shape=jax.ShapeDtypeStruct((n_tiles * tm, D), F32),
        grid_spec=grid_spec,
        compiler_params=_params(("arbitrary", "arbitrary")),
        name="down",
    )(tile_e, n_used, h, w_down, b_down.reshape(E, 1, D))


def _combine_kernel(dest_ref, hs_ref, gate_ref, nw_ref, y_hbm, o_ref, ybuf, sem):
    tm = hs_ref.shape[0]

    def body(r, c):
        for k in range(TOP_K):
            pltpu.make_async_copy(y_hbm.at[pl.ds(dest_ref[k, r], 1)], ybuf.at[k, pl.ds(r, 1)], sem).start()
        return c
    lax.fori_loop(0, tm, body, 0)
    for k in range(TOP_K):
        pltpu.make_async_copy(y_hbm.at[pl.ds(0, tm)], ybuf.at[k], sem).wait()

    g = gate_ref[...]
    acc = hs_ref[...]
    for k in range(TOP_K):
        acc = acc + ybuf[k] * g[:, k:k + 1]
    ms = jnp.mean(acc * acc, axis=-1, keepdims=True)
    o_ref[...] = acc * lax.rsqrt(ms + RMS_EPS) * nw_ref[...]


def _combine(hs1, gates_t, dest3, y, final_norm_w, *, B, LP, seq_len, tile):
    T, D = hs1.shape
    per_seq = LP // tile
    n_seq_tiles = seq_len // tile
    lead = (LP - seq_len) // tile

    def tok(b, i):
        return b * per_seq + lead + i

    return pl.pallas_call(
        _combine_kernel,
        out_shape=jax.ShapeDtypeStruct((B, seq_len, D), F32),
        grid=(B, n_seq_tiles),
        in_specs=[
            pl.BlockSpec((None, TOP_K, tile), lambda b, i: (tok(b, i), 0, 0), memory_space=pltpu.SMEM),
            pl.BlockSpec((tile, D), lambda b, i: (tok(b, i), 0)),
            pl.BlockSpec((tile, TOP_K), lambda b, i: (tok(b, i), 0)),
            pl.BlockSpec((1, D), lambda b, i: (0, 0)),
            pl.BlockSpec(memory_space=pl.ANY),
        ],
        out_specs=pl.BlockSpec((None, tile, D), lambda b, i: (b, i, 0)),
        scratch_shapes=[pltpu.VMEM((TOP_K, tile, D), F32), pltpu.SemaphoreType.DMA],
        compiler_params=_params(("arbitrary", "arbitrary")),
        name="combine",
    )(dest3, hs1, gates_t, final_norm_w.reshape(1, D), y)


def _forward(x, meta_tokens, norm_mix_w, w_in, ret_log_rate_fwd, ret_log_rate_bwd, pool_w, pool_scale,
             w_out, norm_ffn_w, router_w, router_b, w_gate_up, b_gate_up, w_down, b_down, final_norm_w, tiles):
    B, S, D = x.shape
    depth = norm_mix_w.shape[0]
    assert depth == 1, "single-layer block"
    H = ret_log_rate_fwd.shape[-1]
    n_groups, cg = pool_w.shape[1], pool_w.shape[2]
    PW = n_groups * cg
    RW = w_out.shape[1] - PW
    d = RW // H
    E = router_w.shape[-1]
    assert w_in.shape[-1] == 4 * RW + PW and S % CHUNK == 0
    LP = PAD + N_META + S
    T = B * LP
    t = tiles

    hs0 = jnp.concatenate(
        [jnp.zeros((B, PAD, D), x.dtype), jnp.broadcast_to(meta_tokens.astype(x.dtype)[None], (B, N_META, D)), x],
        axis=1).reshape(T, D)

    proj = _in_proj(hs0, norm_mix_w[0], w_in[0].astype(BF16), tm=t["in_tm"], tn=t["in_tn"], sub=t["norm_rows"])
    pos = (jnp.arange(LP, dtype=jnp.int32) - PAD).astype(F32)
    inv_freq = ROPE_BASE ** (-jnp.arange(d // 2, dtype=F32) / (d // 2))
    ang = pos[:, None] * inv_freq[None, :]
    log_gamma = jnp.stack([-jnp.exp(ret_log_rate_fwd[0].astype(F32)), -jnp.exp(ret_log_rate_bwd[0].astype(F32))])
    ret = _retention(proj, log_gamma, jnp.cos(ang), jnp.sin(ang), B=B, LP=LP, H=H, d=d, GC=t["ret_group_chunks"])
    pooled = _pool(proj, pool_w[0].astype(BF16), pool_scale[0].astype(F32), LP=LP, seq_len=N_META + S,
                   u_col0=4 * RW, tm=t["pool_tm"])
    hs1 = _out_proj(ret, pooled, w_out[0].astype(BF16), hs0, tm=t["out_tm"], tn=t["out_tn"])

    row_id = jnp.arange(T, dtype=jnp.int32)
    is_token = (row_id % LP) >= PAD
    fp, eid, gates, rank, cnt = _router(hs1, norm_ffn_w[0], router_w[0], router_b[0], is_token,
                                        tm=t["router_tm"], sub=t["norm_rows"])
    moe_tm = t["moe_tm"]
    tile = t["token_tile"]
    counts = cnt[:, 0].astype(jnp.int32)
    padded = (counts + moe_tm - 1) // moe_tm * moe_tm
    pend = jnp.cumsum(padded)
    pstart = pend - padded
    n_assign = B * (N_META + S) * TOP_K
    n_tiles = -(-(n_assign + E * (moe_tm - 1)) // moe_tm)
    rows = n_tiles * moe_tm
    tile_e = jnp.minimum(jnp.searchsorted(pend, jnp.arange(n_tiles, dtype=jnp.int32) * moe_tm, side="right"),
                         E - 1).astype(jnp.int32)
    n_used = (pend[-1:] // moe_tm).astype(jnp.int32)
    spare = rows + jnp.arange(TOP_K, dtype=jnp.int32)[:, None] * tile + (row_id % tile)[None, :]
    dest = jnp.where(is_token[None, :], pstart[eid] + rank, spare)
    dest3 = dest.reshape(TOP_K, T // tile, tile).transpose(1, 0, 2)

    spare_rows = -(-(TOP_K * tile) // moe_tm) * moe_tm
    xs = _dispatch(fp, dest3, pend.astype(jnp.int32), padded, n_used, rows_total=rows + spare_rows,
                   moe_tm=moe_tm, tile=tile)
    hmid = _gate_up(xs, tile_e, n_used, w_gate_up[0], b_gate_up[0], n_tiles=n_tiles, tm=moe_tm, tn=t["gu_tn"])
    y = _down(hmid, tile_e, n_used, w_down[0], b_down[0], n_tiles=n_tiles, tm=moe_tm, tn=t["down_tn"])
    return _combine(hs1, gates.T, dest3, y, final_norm_w, B=B, LP=LP, seq_len=S, tile=tile)


def kernel(x, meta_tokens, norm_mix_w, w_in, ret_log_rate_fwd, ret_log_rate_bwd, pool_w, pool_scale, w_out,
           norm_ffn_w, router_w, router_b, w_gate_up, b_gate_up, w_down, b_down, final_norm_w):
    return _forward(x, meta_tokens, norm_mix_w, w_in, ret_log_rate_fwd, ret_log_rate_bwd, pool_w, pool_scale,
                    w_out, norm_ffn_w, router_w, router_b, w_gate_up, b_gate_up, w_down, b_down, final_norm_w,
                    DEFAULT_TILES)
```

```python
import functools

import jax
import jax.numpy as jnp
from jax import lax
from jax.experimental import pallas as pl
from jax.experimental.pallas import tpu as pltpu

N_META = 16
CHUNK = 128
PAD = CHUNK - N_META
POOL_WINDOWS = (2, 4, 8, 16)
POOL_HALO = 64
TOP_K = 4
SWIGLU_LIMIT = 7.0
SWIGLU_ALPHA = 1.702
RMS_EPS = 1e-5
ROPE_BASE = 10000.0

V7X_VMEM_BYTES = 64 * 1024 * 1024
VMEM_LIMIT = V7X_VMEM_BYTES - 8 * 1024 * 1024

F32 = jnp.float32
BF16 = jnp.bfloat16

DEFAULT_TILES = dict(
    in_tm=640, in_tn=1024, norm_rows=32,
    ret_group_chunks=13,
    pool_tm=640,
    out_tm=640, out_tn=1024,
    router_tm=256,
    moe_tm=256, gu_tn=512, down_tn=2048,
    token_tile=128,
)


def _params(semantics):
    return pltpu.CompilerParams(dimension_semantics=semantics, vmem_limit_bytes=VMEM_LIMIT)


def _in_proj_kernel(x_ref, nw_ref, w_ref, o_ref, a_scr, *, sub):
    @pl.when(pl.program_id(1) == 0)
    def _():
        def body(r, c):
            rows = pl.ds(pl.multiple_of(r * sub, sub), sub)
            x = x_ref[rows, :]
            ms = jnp.mean(x * x, axis=-1, keepdims=True)
            a_scr[rows, :] = (x * lax.rsqrt(ms + RMS_EPS) * nw_ref[...]).astype(a_scr.dtype)
            return c
        lax.fori_loop(0, x_ref.shape[0] // sub, body, 0)

    o_ref[...] = jnp.dot(a_scr[...], w_ref[...], preferred_element_type=F32).astype(o_ref.dtype)


def _in_proj(hs, norm_w, w_in_bf16, *, tm, tn, sub):
    T, D = hs.shape
    N = w_in_bf16.shape[1]
    return pl.pallas_call(
        functools.partial(_in_proj_kernel, sub=sub),
        out_shape=jax.ShapeDtypeStruct((T, N), BF16),
        grid=(T // tm, N // tn),
        in_specs=[
            pl.BlockSpec((tm, D), lambda i, j: (i, 0)),
            pl.BlockSpec((1, D), lambda i, j: (0, 0)),
            pl.BlockSpec((D, tn), lambda i, j: (0, j)),
        ],
        out_specs=pl.BlockSpec((tm, tn), lambda i, j: (i, j)),
        scratch_shapes=[pltpu.VMEM((tm, D), BF16)],
        compiler_params=_params(("arbitrary", "arbitrary")),
        name="in_proj",
    )(hs, norm_w.reshape(1, D), w_in_bf16)


def _retention_kernel(lg_ref, q_ref, k_ref, v_ref, g_ref, cos_ref, sin_ref, o_ref,
                      acc_scr, sf_scr, sb_scr, dmat_scr, dvec_scr, *, G, GC):
    h = pl.program_id(1)
    s = pl.program_id(2)
    phase = s // G
    w = s % G
    gi = w + phase * (G - 1 - 2 * w)
    C = CHUNK
    d = q_ref.shape[1]
    half = d // 2
    GR = GC * C
    lgf = lg_ref[0, h]
    lgb = lg_ref[1, h]
    k_scale = d ** -0.5
    nt = (((1,), (1,)), ((), ()))
    tn = (((0,), (0,)), ((), ()))

    @pl.when(s == 0)
    def _():
        t = lax.broadcasted_iota(jnp.int32, (C, C), 0).astype(F32)
        u = lax.broadcasted_iota(jnp.int32, (C, C), 1).astype(F32)
        diff = t - u
        dmat_scr[...] = jnp.where(diff >= 0.0, jnp.exp(lgf * jnp.maximum(diff, 0.0)),
                                  jnp.exp(lgb * jnp.maximum(-diff, 0.0)))
        idx = lax.broadcasted_iota(jnp.int32, (C, d), 0).astype(F32)
        dvec_scr[0] = jnp.exp(lgf * (idx + 1.0))
        dvec_scr[1] = jnp.exp(lgf * (C - 1.0 - idx))
        dvec_scr[2] = jnp.exp(lgb * (C - idx))
        dvec_scr[3] = jnp.exp(lgb * idx)
        dvec_scr[4] = jnp.exp(jnp.full((C, d), lgf * C, F32))
        dvec_scr[5] = jnp.exp(jnp.full((C, d), lgb * C, F32))
        sf_scr[...] = jnp.zeros_like(sf_scr)

    @pl.when(s == G)
    def _():
        sb_scr[...] = jnp.zeros_like(sb_scr)

    def rotary(ref, rows):
        x = ref[rows, :].astype(F32)
        x1 = x[:, :half]
        x2 = x[:, half:]
        cs = cos_ref[rows, :]
        sn = sin_ref[rows, :]
        return jnp.concatenate([x1 * cs - x2 * sn, x2 * cs + x1 * sn], axis=-1)

    @pl.when(phase == 0)
    def _():
        def body(c, carry):
            rows = pl.ds(pl.multiple_of(c * C, C), C)
            arows = pl.ds(pl.multiple_of(gi * GR + c * C, C), C)
            qc = rotary(q_ref, rows).astype(BF16)
            kf = rotary(k_ref, rows) * k_scale
            vc = v_ref[rows, :]
            sc = lax.dot_general(qc, kf.astype(BF16), nt, preferred_element_type=F32)
            p = (sc * dmat_scr[...]).astype(BF16)
            inner = jnp.dot(p, vc, preferred_element_type=F32)
            sf = sf_scr[...]
            cross = jnp.dot(qc, sf.astype(BF16), preferred_element_type=F32) * dvec_scr[0]
            acc_scr[arows, :] = inner + cross
            kd = (kf * dvec_scr[1]).astype(BF16)
            sf_scr[...] = sf * dvec_scr[4, 0:1, :] + lax.dot_general(kd, vc, tn, preferred_element_type=F32)
            return carry
        lax.fori_loop(0, GC, body, 0)

    @pl.when(phase == 1)
    def _():
        def body(cc, carry):
            c = GC - 1 - cc
            rows = pl.ds(pl.multiple_of(c * C, C), C)
            arows = pl.ds(pl.multiple_of(gi * GR + c * C, C), C)
            qc = rotary(q_ref, rows).astype(BF16)
            kf = rotary(k_ref, rows) * k_scale
            vc = v_ref[rows, :]
            sb = sb_scr[...]
            r = acc_scr[arows, :] + jnp.dot(qc, sb.astype(BF16), preferred_element_type=F32) * dvec_scr[2]
            kd = (kf * dvec_scr[3]).astype(BF16)
            sb_scr[...] = sb * dvec_scr[5, 0:1, :] + lax.dot_general(kd, vc, tn, preferred_element_type=F32)
            rn = r * lax.rsqrt(jnp.mean(r * r, axis=-1, keepdims=True) + RMS_EPS)
            gg = g_ref[rows, :].astype(F32)
            o_ref[rows, :] = (gg * jax.nn.sigmoid(gg) * rn).astype(o_ref.dtype)
            return carry
        lax.fori_loop(0, GC, body, 0)


def _retention(proj, log_gamma, cos_tab, sin_tab, *, B, LP, H, d, GC):
    T = proj.shape[0]
    GR = GC * CHUNK
    G = LP // GR

    def grp(s):
        phase = s // G
        w = s % G
        return phase, w + phase * (G - 1 - 2 * w)

    def qkv_map(col0):
        def m(b, h, s):
            _, gi = grp(s)
            return (b * G + gi, col0 + h)
        return m

    def late_map(col0):
        def m(b, h, s):
            phase, gi = grp(s)
            return (b * G + jnp.where(phase == 0, G - 1, gi), col0 + h)
        return m

    def tab_map(b, h, s):
        _, gi = grp(s)
        return (gi, 0)

    return pl.pallas_call(
        functools.partial(_retention_kernel, G=G, GC=GC),
        out_shape=jax.ShapeDtypeStruct((T, H * d), BF16),
        grid=(B, H, 2 * G),
        in_specs=[
            pl.BlockSpec(memory_space=pltpu.SMEM),
            pl.BlockSpec((GR, d), qkv_map(0)),
            pl.BlockSpec((GR, d), qkv_map(H)),
            pl.BlockSpec((GR, d), qkv_map(2 * H)),
            pl.BlockSpec((GR, d), late_map(3 * H)),
            pl.BlockSpec((GR, d // 2), tab_map),
            pl.BlockSpec((GR, d // 2), tab_map),
        ],
        out_specs=pl.BlockSpec((GR, d), late_map(0)),
        scratch_shapes=[
            pltpu.VMEM((LP, d), F32),
            pltpu.VMEM((d, d), F32),
            pltpu.VMEM((d, d), F32),
            pltpu.VMEM((CHUNK, CHUNK), F32),
            pltpu.VMEM((6, CHUNK, d), F32),
        ],
        compiler_params=_params(("arbitrary", "arbitrary", "arbitrary")),
        name="retention",
    )(log_gamma, proj, proj, proj, proj, cos_tab, sin_tab)


def _pool_kernel(prev_ref, cur_ref, next_ref, pw_ref, ps_ref, o_ref, *, tiles_per_seq, seq_len):
    i = pl.program_id(0)
    tm = cur_ref.shape[0]
    ext = tm + 2 * POOL_HALO
    n_groups = pw_ref.shape[0]
    cg = pw_ref.shape[1]
    row0 = (i % tiles_per_seq) * tm - PAD
    pos_t = row0 + lax.broadcasted_iota(jnp.int32, (tm, ext), 0)
    pos_s = row0 - POOL_HALO + lax.broadcasted_iota(jnp.int32, (tm, ext), 1)
    pos_c = row0 + lax.broadcasted_iota(jnp.int32, (tm, 1), 0)
    for gi in range(n_groups):
        win = POOL_WINDOWS[gi]
        cols = slice(gi * cg, (gi + 1) * cg)
        lo = jnp.clip(pos_t - win // 2, 0, seq_len)
        hi = jnp.clip(pos_t + (win - win // 2), 0, seq_len)
        band = jnp.where(pos_s >= lo, jnp.where(pos_s < hi, 1.0, 0.0), 0.0).astype(BF16)
        cnt = (jnp.clip(pos_c + (win - win // 2), 0, seq_len) - jnp.clip(pos_c - win // 2, 0, seq_len)).astype(F32)
        inv = jnp.where(pos_c >= 0, 1.0 / jnp.maximum(cnt, 1.0), 0.0)
        u_cur = cur_ref[:, cols]
        u_ext = jnp.concatenate([prev_ref[:, cols], u_cur, next_ref[:, cols]], axis=0)
        wsum = jnp.dot(band, u_ext, preferred_element_type=F32)
        pooled = wsum * inv - u_cur.astype(F32)
        y = jnp.dot(pooled.astype(BF16), pw_ref[gi], preferred_element_type=F32)
        o_ref[:, cols] = (y * ps_ref[:, cols]).astype(o_ref.dtype)


def _pool(proj, pool_w_bf16, pool_scale, *, LP, seq_len, u_col0, tm):
    T = proj.shape[0]
    n_groups, cg, _ = pool_w_bf16.shape
    PW = n_groups * cg
    col_blk = u_col0 // PW
    hb = tm // POOL_HALO
    n_halo = T // POOL_HALO
    return pl.pallas_call(
        functools.partial(_pool_kernel, tiles_per_seq=LP // tm, seq_len=seq_len),
        out_shape=jax.ShapeDtypeStruct((T, PW), BF16),
        grid=(T // tm,),
        in_specs=[
            pl.BlockSpec((POOL_HALO, PW), lambda i: (jnp.maximum(i * hb - 1, 0), col_blk)),
            pl.BlockSpec((tm, PW), lambda i: (i, col_blk)),
            pl.BlockSpec((POOL_HALO, PW), lambda i: (jnp.minimum((i + 1) * hb, n_halo - 1), col_blk)),
            pl.BlockSpec((n_groups, cg, cg), lambda i: (0, 0, 0)),
            pl.BlockSpec((1, PW), lambda i: (0, 0)),
        ],
        out_specs=pl.BlockSpec((tm, PW), lambda i: (i, 0)),
        compiler_params=_params(("arbitrary",)),
        name="pool",
    )(proj, proj, proj, pool_w_bf16, pool_scale.reshape(1, PW))


def _out_proj_kernel(ret_ref, pool_ref, wr_ref, wp_ref, hs_ref, o_ref):
    y = jnp.dot(ret_ref[...], wr_ref[...], preferred_element_type=F32)
    y = y + jnp.dot(pool_ref[...], wp_ref[...], preferred_element_type=F32)
    o_ref[...] = hs_ref[...] + y


def _out_proj(ret, pool, w_out_bf16, hs, *, tm, tn):
    T, RW = ret.shape
    PW = pool.shape[1]
    D = hs.shape[1]
    assert RW == PW, "the two mixer halves share one row-block split of w_out"
    return pl.pallas_call(
        _out_proj_kernel,
        out_shape=jax.ShapeDtypeStruct((T, D), F32),
        grid=(T // tm, D // tn),
        in_specs=[
            pl.BlockSpec((tm, RW), lambda i, j: (i, 0)),
            pl.BlockSpec((tm, PW), lambda i, j: (i, 0)),
            pl.BlockSpec((RW, tn), lambda i, j: (0, j)),
            pl.BlockSpec((PW, tn), lambda i, j: (1, j)),
            pl.BlockSpec((tm, tn), lambda i, j: (i, j)),
        ],
        out_specs=pl.BlockSpec((tm, tn), lambda i, j: (i, j)),
        compiler_params=_params(("arbitrary", "arbitrary")),
        name="out_proj",
    )(ret, pool, w_out_bf16, w_out_bf16, hs)


def _pack_bf16_pair(lo, hi):
    lo_bits = lax.bitcast_convert_type(lo.astype(BF16).astype(F32), jnp.uint32)
    hi_bits = lax.bitcast_convert_type(hi.astype(BF16).astype(F32), jnp.uint32)
    return (hi_bits & jnp.uint32(0xFFFF0000)) | (lo_bits >> jnp.uint32(16))


def _unpack_bf16_pair(word):
    lo = lax.bitcast_convert_type(word << jnp.uint32(16), F32).astype(BF16)
    hi = lax.bitcast_convert_type(word & jnp.uint32(0xFFFF0000), F32).astype(BF16)
    return lo, hi


def _router_kernel(hs_ref, nw_ref, rw_ref, rwhi_ref, rb_ref, valid_ref,
                   fp_ref, eid_ref, gate_ref, rank_ref, cnt_ref,
                   fhi_scr, flo_scr, carry_scr, *, sub):
    i = pl.program_id(0)
    tm, D = hs_ref.shape
    E = rb_ref.shape[0]
    half = D // 2

    @pl.when(i == 0)
    def _():
        carry_scr[...] = jnp.zeros_like(carry_scr)

    def body(r, c):
        rows = pl.ds(pl.multiple_of(r * sub, sub), sub)
        x = hs_ref[rows, :]
        ms = jnp.mean(x * x, axis=-1, keepdims=True)
        f = x * lax.rsqrt(ms + RMS_EPS) * nw_ref[...]
        f_hi = f.astype(BF16)
        fhi_scr[rows, :] = f_hi
        flo_scr[rows, :] = (f - f_hi.astype(F32)).astype(BF16)
        fp_ref[rows, :] = _pack_bf16_pair(f[:, :half], f[:, half:])
        return c
    lax.fori_loop(0, tm // sub, body, 0)

    nt = (((1,), (1,)), ((), ()))
    part = lax.dot_general(rw_ref[...], fhi_scr[...], nt, preferred_element_type=F32)
    logits = part[:E] + part[E:] + lax.dot_general(rwhi_ref[...], flo_scr[...], nt, preferred_element_type=F32)
    logits = logits + rb_ref[...]

    eio = lax.broadcasted_iota(jnp.int32, (E, tm), 0).astype(F32)
    valid = valid_ref[...]

    vals, onehots = [], []
    l = logits
    for k in range(TOP_K):
        m = jnp.max(l, axis=0, keepdims=True)
        idx = jnp.min(jnp.where(l == m, eio, float(E)), axis=0, keepdims=True)
        sel = eio == idx
        vals.append(m)
        onehots.append(jnp.where(sel, valid, 0.0))
        eid_ref[pl.ds(k, 1), :] = idx.astype(jnp.int32)
        l = jnp.where(sel, -jnp.inf, l)

    exps = [jnp.exp(v - vals[0]) for v in vals]
    denom = exps[0] + exps[1] + exps[2] + exps[3]
    for k in range(TOP_K):
        gate_ref[pl.ds(k, 1), :] = exps[k] / denom

    oh = onehots[0] + onehots[1] + onehots[2] + onehots[3]
    tt = lax.broadcasted_iota(jnp.int32, (tm, tm), 0)
    uu = lax.broadcasted_iota(jnp.int32, (tm, tm), 1)
    upper = jnp.where(tt < uu, 1.0, 0.0).astype(BF16)
    carry = carry_scr[...]
    before = jnp.dot(oh.astype(BF16), upper, preferred_element_type=F32) + carry[:, 0:1]
    for k in range(TOP_K):
        rank_ref[pl.ds(k, 1), :] = jnp.sum(onehots[k] * before, axis=0, keepdims=True).astype(jnp.int32)
    carry = carry + jnp.sum(oh, axis=1, keepdims=True)
    carry_scr[...] = carry
    cnt_ref[...] = carry


def _router(hs1, norm_w, router_w, router_b, is_token, *, tm, sub):
    T, D = hs1.shape
    E = router_w.shape[1]
    rw_t = router_w.T.astype(F32)
    rw_hi = rw_t.astype(BF16)
    rw_lo = (rw_t - rw_hi.astype(F32)).astype(BF16)
    rw_stack = jnp.concatenate([rw_hi, rw_lo], axis=0)
    n = T // tm
    return pl.pallas_call(
        functools.partial(_router_kernel, sub=sub),
        out_shape=(
            jax.ShapeDtypeStruct((T, D // 2), jnp.uint32),
            jax.ShapeDtypeStruct((TOP_K, T), jnp.int32),
            jax.ShapeDtypeStruct((TOP_K, T), F32),
            jax.ShapeDtypeStruct((TOP_K, T), jnp.int32),
            jax.ShapeDtypeStruct((E, CHUNK), F32),
        ),
        grid=(n,),
        in_specs=[
            pl.BlockSpec((tm, D), lambda i: (i, 0)),
            pl.BlockSpec((1, D), lambda i: (0, 0)),
            pl.BlockSpec((2 * E, D), lambda i: (0, 0)),
            pl.BlockSpec((E, D), lambda i: (0, 0)),
            pl.BlockSpec((E, 1), lambda i: (0, 0)),
            pl.BlockSpec((1, tm), lambda i: (0, i)),
        ],
        out_specs=(
            pl.BlockSpec((tm, D // 2), lambda i: (i, 0)),
            pl.BlockSpec((TOP_K, tm), lambda i: (0, i)),
            pl.BlockSpec((TOP_K, tm), lambda i: (0, i)),
            pl.BlockSpec((TOP_K, tm), lambda i: (0, i)),
            pl.BlockSpec((E, CHUNK), lambda i: (0, 0)),
        ),
        scratch_shapes=[pltpu.VMEM((tm, D), BF16), pltpu.VMEM((tm, D), BF16), pltpu.VMEM((E, CHUNK), F32)],
        compiler_params=_params(("arbitrary",)),
        name="router",
    )(hs1, norm_w.reshape(1, D), rw_stack, rw_hi, router_b.reshape(E, 1).astype(F32),
      is_token.astype(F32).reshape(1, T))


def _dispatch_kernel(pend_ref, padded_ref, nu_ref, dest_ref, fp_ref, zeros_ref, xs_hbm, zsem, sem, *, moe_tm):
    i = pl.program_id(0)
    tm = fp_ref.shape[0]
    E = pend_ref.shape[0]
    n_blocks = xs_hbm.shape[0] // moe_tm

    def zero_rows(start):
        return pltpu.make_async_copy(zeros_ref, xs_hbm.at[pl.ds(pl.multiple_of(start, moe_tm), moe_tm)], zsem)

    @pl.when(i == 0)
    def _():
        for e in range(E):
            @pl.when(padded_ref[e] > 0)
            def _():
                zero_rows(pend_ref[e] - moe_tm).start()

        def start_tail(b, c):
            zero_rows(b * moe_tm).start()
            return c
        lax.fori_loop(nu_ref[0], n_blocks, start_tail, 0)
        for e in range(E):
            @pl.when(padded_ref[e] > 0)
            def _():
                zero_rows(pend_ref[e] - moe_tm).wait()

        def wait_tail(b, c):
            zero_rows(b * moe_tm).wait()
            return c
        lax.fori_loop(nu_ref[0], n_blocks, wait_tail, 0)

    def body(r, c):
        for k in range(TOP_K):
            pltpu.make_async_copy(fp_ref.at[pl.ds(r, 1)], xs_hbm.at[pl.ds(dest_ref[k, r], 1)], sem).start()
        return c
    lax.fori_loop(0, tm, body, 0)
    for k in range(TOP_K):
        pltpu.make_async_copy(fp_ref, xs_hbm.at[pl.ds(0, tm)], sem).wait()


def _dispatch(fp, dest3, pend, padded, n_used, *, rows_total, moe_tm, tile):
    T, W = fp.shape
    assert rows_total % moe_tm == 0
    zeros = jnp.zeros((moe_tm, W), fp.dtype)
    grid_spec = pltpu.PrefetchScalarGridSpec(
        num_scalar_prefetch=3,
        grid=(T // tile,),
        in_specs=[
            pl.BlockSpec((None, TOP_K, tile), lambda i, *_: (i, 0, 0), memory_space=pltpu.SMEM),
            pl.BlockSpec((tile, W), lambda i, *_: (i, 0)),
            pl.BlockSpec((moe_tm, W), lambda i, *_: (0, 0)),
        ],
        out_specs=pl.BlockSpec(memory_space=pl.ANY),
        scratch_shapes=[pltpu.SemaphoreType.DMA, pltpu.SemaphoreType.DMA],
    )
    return pl.pallas_call(
        functools.partial(_dispatch_kernel, moe_tm=moe_tm),
        out_shape=jax.ShapeDtypeStruct((rows_total, W), fp.dtype),
        grid_spec=grid_spec,
        compiler_params=_params(("arbitrary",)),
        name="dispatch",
    )(pend, padded, n_used, dest3, fp, zeros)


def _gate_up_kernel(te_ref, nu_ref, xs_ref, wg_ref, wu_ref, bg_ref, bu_ref, h_ref):
    i = pl.program_id(1)
    active = i < nu_ref[0]

    @pl.when(active)
    def _():
        lo, hi = _unpack_bf16_pair(xs_ref[...])
        x = jnp.concatenate([lo, hi], axis=-1)
        gate = jnp.dot(x, wg_ref[...].astype(BF16), preferred_element_type=F32) + bg_ref[...]
        up = jnp.dot(x, wu_ref[...].astype(BF16), preferred_element_type=F32) + bu_ref[...]
        gate = jnp.minimum(gate, SWIGLU_LIMIT)
        up = jnp.clip(up, -SWIGLU_LIMIT, SWIGLU_LIMIT)
        h_ref[...] = ((up + 1.0) * gate * jax.nn.sigmoid(SWIGLU_ALPHA * gate)).astype(h_ref.dtype)

    @pl.when(jnp.logical_not(active))
    def _():
        h_ref[...] = jnp.zeros_like(h_ref)


def _gate_up(xs, tile_e, n_used, w_gate_up, b_gate_up, *, n_tiles, tm, tn):
    E, D, F2 = w_gate_up.shape
    FF = F2 // 2
    nj = FF // tn
    W = xs.shape[1]

    def row(i, te, nu):
        return jnp.minimum(i, nu[0] - 1)

    grid_spec = pltpu.PrefetchScalarGridSpec(
        num_scalar_prefetch=2,
        grid=(nj, n_tiles),
        in_specs=[
            pl.BlockSpec((tm, W), lambda j, i, te, nu: (row(i, te, nu), 0)),
            pl.BlockSpec((None, D, tn), lambda j, i, te, nu: (te[row(i, te, nu)], 0, j)),
            pl.BlockSpec((None, D, tn), lambda j, i, te, nu: (te[row(i, te, nu)], 0, nj + j)),
            pl.BlockSpec((None, 1, tn), lambda j, i, te, nu: (te[row(i, te, nu)], 0, j)),
            pl.BlockSpec((None, 1, tn), lambda j, i, te, nu: (te[row(i, te, nu)], 0, nj + j)),
        ],
        out_specs=pl.BlockSpec((tm, tn), lambda j, i, te, nu: (i, j)),
    )
    b3 = b_gate_up.reshape(E, 1, F2)
    return pl.pallas_call(
        _gate_up_kernel,
        out_shape=jax.ShapeDtypeStruct((n_tiles * tm, FF), BF16),
        grid_spec=grid_spec,
        compiler_params=_params(("arbitrary", "arbitrary")),
        name="gate_up",
    )(tile_e, n_used, xs, w_gate_up, w_gate_up, b3, b3)


def _down_kernel(te_ref, nu_ref, h_ref, w_ref, b_ref, y_ref):
    i = pl.program_id(1)
    active = i < nu_ref[0]

    @pl.when(active)
    def _():
        y_ref[...] = jnp.dot(h_ref[...], w_ref[...].astype(BF16), preferred_element_type=F32) + b_ref[...]

    @pl.when(jnp.logical_not(active))
    def _():
        y_ref[...] = jnp.zeros_like(y_ref)


def _down(h, tile_e, n_used, w_down, b_down, *, n_tiles, tm, tn):
    E, FF, D = w_down.shape

    def row(i, te, nu):
        return jnp.minimum(i, nu[0] - 1)

    grid_spec = pltpu.PrefetchScalarGridSpec(
        num_scalar_prefetch=2,
        grid=(D // tn, n_tiles),
        in_specs=[
            pl.BlockSpec((tm, FF), lambda j, i, te, nu: (row(i, te, nu), 0)),
            pl.BlockSpec((None, FF, tn), lambda j, i, te, nu: (te[row(i, te, nu)], 0, j)),
            pl.BlockSpec((None, 1, tn), lambda j, i, te, nu: (te[row(i, te, nu)], 0, j)),
        ],
        out_specs=pl.BlockSpec((tm, tn), lambda j, i, te, nu: (i, j)),
    )
    return pl.pallas_call(
        _down_kernel,
        out_shape=jax.ShapeDtypeStruct((n_tiles * tm, D), F32),
        grid_spec=grid_spec,
        compiler_params=_params(("arbitrary", "arbitrary")),
        name="down",
    )(tile_e, n_used, h, w_down, b_down.reshape(E, 1, D))


def _combine_kernel(dest_ref, hs_ref, gate_ref, nw_ref, y_hbm, o_ref, ybuf, sem):
    tm = hs_ref.shape[0]

    def body(r, c):
        for k in range(TOP_K):
            pltpu.make_async_copy(y_hbm.at[pl.ds(dest_ref[k, r], 1)], ybuf.at[k, pl.ds(r, 1)], sem).start()
        return c
    lax.fori_loop(0, tm, body, 0)
    for k in range(TOP_K):
        pltpu.make_async_copy(y_hbm.at[pl.ds(0, tm)], ybuf.at[k], sem).wait()

    g = gate_ref[...]
    acc = hs_ref[...]
    for k in range(TOP_K):
        acc = acc + ybuf[k] * g[:, k:k + 1]
    ms = jnp.mean(acc * acc, axis=-1, keepdims=True)
    o_ref[...] = acc * lax.rsqrt(ms + RMS_EPS) * nw_ref[...]


def _combine(hs1, gates_t, dest3, y, final_norm_w, *, B, LP, seq_len, tile):
    T, D = hs1.shape
    per_seq = LP // tile
    n_seq_tiles = seq_len // tile
    lead = (LP - seq_len) // tile

    def tok(b, i):
        return b * per_seq + lead + i

    return pl.pallas_call(
        _combine_kernel,
        out_shape=jax.ShapeDtypeStruct((B, seq_len, D), F32),
        grid=(B, n_seq_tiles),
        in_specs=[
            pl.BlockSpec((None, TOP_K, tile), lambda b, i: (tok(b, i), 0, 0), memory_space=pltpu.SMEM),
            pl.BlockSpec((tile, D), lambda b, i: (tok(b, i), 0)),
            pl.BlockSpec((tile, TOP_K), lambda b, i: (tok(b, i), 0)),
            pl.BlockSpec((1, D), lambda b, i: (0, 0)),
            pl.BlockSpec(memory_space=pl.ANY),
        ],
        out_specs=pl.BlockSpec((None, tile, D), lambda b, i: (b, i, 0)),
        scratch_shapes=[pltpu.VMEM((TOP_K, tile, D), F32), pltpu.SemaphoreType.DMA],
        compiler_params=_params(("arbitrary", "arbitrary")),
        name="combine",
    )(dest3, hs1, gates_t, final_norm_w.reshape(1, D), y)


def _forward(x, meta_tokens, norm_mix_w, w_in, ret_log_rate_fwd, ret_log_rate_bwd, pool_w, pool_scale,
             w_out, norm_ffn_w, router_w, router_b, w_gate_up, b_gate_up, w_down, b_down, final_norm_w, tiles):
    B, S, D = x.shape
    depth = norm_mix_w.shape[0]
    assert depth == 1, "single-layer block"
    H = ret_log_rate_fwd.shape[-1]
    n_groups, cg = pool_w.shape[1], pool_w.shape[2]
    PW = n_groups * cg
    RW = w_out.shape[1] - PW
    d = RW // H
    E = router_w.shape[-1]
    assert w_in.shape[-1] == 4 * RW + PW and S % CHUNK == 0
    LP = PAD + N_META + S
    T = B * LP
    t = tiles

    hs0 = jnp.concatenate(
        [jnp.zeros((B, PAD, D), x.dtype), jnp.broadcast_to(meta_tokens.astype(x.dtype)[None], (B, N_META, D)), x],
        axis=1).reshape(T, D)

    proj = _in_proj(hs0, norm_mix_w[0], w_in[0].astype(BF16), tm=t["in_tm"], tn=t["in_tn"], sub=t["norm_rows"])
    pos = (jnp.arange(LP, dtype=jnp.int32) - PAD).astype(F32)
    inv_freq = ROPE_BASE ** (-jnp.arange(d // 2, dtype=F32) / (d // 2))
    ang = pos[:, None] * inv_freq[None, :]
    log_gamma = jnp.stack([-jnp.exp(ret_log_rate_fwd[0].astype(F32)), -jnp.exp(ret_log_rate_bwd[0].astype(F32))])
    ret = _retention(proj, log_gamma, jnp.cos(ang), jnp.sin(ang), B=B, LP=LP, H=H, d=d, GC=t["ret_group_chunks"])
    pooled = _pool(proj, pool_w[0].astype(BF16), pool_scale[0].astype(F32), LP=LP, seq_len=N_META + S,
                   u_col0=4 * RW, tm=t["pool_tm"])
    hs1 = _out_proj(ret, pooled, w_out[0].astype(BF16), hs0, tm=t["out_tm"], tn=t["out_tn"])

    row_id = jnp.arange(T, dtype=jnp.int32)
    is_token = (row_id % LP) >= PAD
    fp, eid, gates, rank, cnt = _router(hs1, norm_ffn_w[0], router_w[0], router_b[0], is_token,
                                        tm=t["router_tm"], sub=t["norm_rows"])
    moe_tm = t["moe_tm"]
    tile = t["token_tile"]
    counts = cnt[:, 0].astype(jnp.int32)
    padded = (counts + moe_tm - 1) // moe_tm * moe_tm
    pend = jnp.cumsum(padded)
    pstart = pend - padded
    n_assign = B * (N_META + S) * TOP_K
    n_tiles = -(-(n_assign + E * (moe_tm - 1)) // moe_tm)
    rows = n_tiles * moe_tm
    tile_start = jnp.arange(n_tiles, dtype=jnp.int32) * moe_tm
    tile_e = jnp.minimum(jnp.sum((pend[None, :] <= tile_start[:, None]).astype(jnp.int32), axis=1), E - 1)
    n_used = (pend[-1:] // moe_tm).astype(jnp.int32)
    spare = rows + jnp.arange(TOP_K, dtype=jnp.int32)[:, None] * tile + (row_id % tile)[None, :]
    expert_ids = jnp.arange(E, dtype=jnp.int32)[:, None, None]
    first_row = jnp.sum(jnp.where(eid[None] == expert_ids, pstart[:, None, None], 0), axis=0)
    dest = jnp.where(is_token[None, :], first_row + rank, spare)
    dest3 = dest.reshape(TOP_K, T // tile, tile).transpose(1, 0, 2)

    spare_rows = -(-(TOP_K * tile) // moe_tm) * moe_tm
    xs = _dispatch(fp, dest3, pend.astype(jnp.int32), padded, n_used, rows_total=rows + spare_rows,
                   moe_tm=moe_tm, tile=tile)
    hmid = _gate_up(xs, tile_e, n_used, w_gate_up[0], b_gate_up[0], n_tiles=n_tiles, tm=moe_tm, tn=t["gu_tn"])
    y = _down(hmid, tile_e, n_used, w_down[0], b_down[0], n_tiles=n_tiles, tm=moe_tm, tn=t["down_tn"])
    return _combine(hs1, gates.T, dest3, y, final_norm_w, B=B, LP=LP, seq_len=S, tile=tile)


def kernel(x, meta_tokens, norm_mix_w, w_in, ret_log_rate_fwd, ret_log_rate_bwd, pool_w, pool_scale, w_out,
           norm_ffn_w, router_w, router_b, w_gate_up, b_gate_up, w_down, b_down, final_norm_w):
    return _forward(x, meta_tokens, norm_mix_w, w_in, ret_log_rate_fwd, ret_log_rate_bwd, pool_w, pool_scale,
                    w_out, norm_ffn_w, router_w, router_b, w_gate_up, b_gate_up, w_down, b_down, final_norm_w,
                    DEFAULT_TILES)
```

```python
import functools

import jax
import jax.numpy as jnp
from jax import lax
from jax.experimental import pallas as pl
from jax.experimental.pallas import tpu as pltpu

N_META = 16
CHUNK = 128
PAD = CHUNK - N_META
POOL_WINDOWS = (2, 4, 8, 16)
POOL_HALO = 64
TOP_K = 4
SWIGLU_LIMIT = 7.0
SWIGLU_ALPHA = 1.702
RMS_EPS = 1e-5
ROPE_BASE = 10000.0

V7X_VMEM_BYTES = 64 * 1024 * 1024
VMEM_LIMIT = V7X_VMEM_BYTES - 8 * 1024 * 1024

F32 = jnp.float32
BF16 = jnp.bfloat16

DEFAULT_TILES = dict(
    in_tm=640, in_tn=1024, norm_rows=32,
    ret_group_chunks=13, ret_heads_per_step=2,
    pool_tm=640,
    out_tm=640, out_tn=1024,
    router_tm=256,
    moe_tm=512, gu_tn=512, down_tn=2048,
    token_tile=128, combine_rows=32,
)


def _params(semantics):
    return pltpu.CompilerParams(dimension_semantics=semantics, vmem_limit_bytes=VMEM_LIMIT)


def _in_proj_kernel(x_ref, nw_ref, w_ref, o_ref, a_scr, *, sub):
    @pl.when(pl.program_id(1) == 0)
    def _():
        def body(r, c):
            rows = pl.ds(pl.multiple_of(r * sub, sub), sub)
            x = x_ref[rows, :]
            ms = jnp.mean(x * x, axis=-1, keepdims=True)
            a_scr[rows, :] = (x * lax.rsqrt(ms + RMS_EPS) * nw_ref[...]).astype(a_scr.dtype)
            return c
        lax.fori_loop(0, x_ref.shape[0] // sub, body, 0)

    o_ref[...] = jnp.dot(a_scr[...], w_ref[...], preferred_element_type=F32).astype(o_ref.dtype)


def _in_proj(hs, norm_w, w_in_bf16, *, tm, tn, sub):
    T, D = hs.shape
    N = w_in_bf16.shape[1]
    return pl.pallas_call(
        functools.partial(_in_proj_kernel, sub=sub),
        out_shape=jax.ShapeDtypeStruct((T, N), BF16),
        grid=(T // tm, N // tn),
        in_specs=[
            pl.BlockSpec((tm, D), lambda i, j: (i, 0)),
            pl.BlockSpec((1, D), lambda i, j: (0, 0)),
            pl.BlockSpec((D, tn), lambda i, j: (0, j)),
        ],
        out_specs=pl.BlockSpec((tm, tn), lambda i, j: (i, j)),
        scratch_shapes=[pltpu.VMEM((tm, D), BF16)],
        compiler_params=_params(("arbitrary", "arbitrary")),
        name="in_proj",
    )(hs, norm_w.reshape(1, D), w_in_bf16)


def _retention_kernel(lg_ref, q_ref, k_ref, v_ref, g_ref, cos_ref, sin_ref, o_ref,
                      acc_scr, sf_scr, sb_scr, dmat_scr, dvec_scr, *, G, GC):
    hb = pl.program_id(1)
    s = pl.program_id(2)
    phase = s // G
    w = s % G
    gi = w + phase * (G - 1 - 2 * w)
    C = CHUNK
    d = cos_ref.shape[1] * 2
    half = d // 2
    HB = q_ref.shape[1] // d
    GR = GC * C
    k_scale = d ** -0.5
    nt = (((1,), (1,)), ((), ()))
    tn = (((0,), (0,)), ((), ()))

    @pl.when(s == 0)
    def _():
        t = lax.broadcasted_iota(jnp.int32, (C, C), 0).astype(F32)
        u = lax.broadcasted_iota(jnp.int32, (C, C), 1).astype(F32)
        diff = t - u
        idx = lax.broadcasted_iota(jnp.int32, (C, d), 0).astype(F32)
        for hh in range(HB):
            lgf = lg_ref[0, hb * HB + hh]
            lgb = lg_ref[1, hb * HB + hh]
            dmat_scr[hh] = jnp.where(diff >= 0.0, jnp.exp(lgf * jnp.maximum(diff, 0.0)),
                                     jnp.exp(lgb * jnp.maximum(-diff, 0.0)))
            dvec_scr[hh, 0] = jnp.exp(lgf * (idx + 1.0))
            dvec_scr[hh, 1] = jnp.exp(lgf * (C - 1.0 - idx))
            dvec_scr[hh, 2] = jnp.exp(lgb * (C - idx))
            dvec_scr[hh, 3] = jnp.exp(lgb * idx)
            dvec_scr[hh, 4] = jnp.exp(jnp.full((C, d), lgf * C, F32))
            dvec_scr[hh, 5] = jnp.exp(jnp.full((C, d), lgb * C, F32))
        sf_scr[...] = jnp.zeros_like(sf_scr)

    @pl.when(s == G)
    def _():
        sb_scr[...] = jnp.zeros_like(sb_scr)

    def rotary(ref, rows, hh):
        x = ref[rows, hh * d:(hh + 1) * d].astype(F32)
        x1 = x[:, :half]
        x2 = x[:, half:]
        cs = cos_ref[rows, :]
        sn = sin_ref[rows, :]
        return jnp.concatenate([x1 * cs - x2 * sn, x2 * cs + x1 * sn], axis=-1)

    @pl.when(phase == 0)
    def _():
        def body(c, carry):
            rows = pl.ds(pl.multiple_of(c * C, C), C)
            arows = pl.ds(pl.multiple_of(gi * GR + c * C, C), C)
            for hh in range(HB):
                cols = slice(hh * d, (hh + 1) * d)
                qc = rotary(q_ref, rows, hh).astype(BF16)
                kf = rotary(k_ref, rows, hh) * k_scale
                vc = v_ref[rows, cols]
                sc = lax.dot_general(qc, kf.astype(BF16), nt, preferred_element_type=F32)
                p = (sc * dmat_scr[hh]).astype(BF16)
                inner = jnp.dot(p, vc, preferred_element_type=F32)
                sf = sf_scr[hh]
                cross = jnp.dot(qc, sf.astype(BF16), preferred_element_type=F32) * dvec_scr[hh, 0]
                acc_scr[arows, cols] = inner + cross
                kd = (kf * dvec_scr[hh, 1]).astype(BF16)
                sf_scr[hh] = sf * dvec_scr[hh, 4, 0:1, :] + lax.dot_general(kd, vc, tn, preferred_element_type=F32)
            return carry
        lax.fori_loop(0, GC, body, 0)

    @pl.when(phase == 1)
    def _():
        def body(cc, carry):
            c = GC - 1 - cc
            rows = pl.ds(pl.multiple_of(c * C, C), C)
            arows = pl.ds(pl.multiple_of(gi * GR + c * C, C), C)
            for hh in range(HB):
                cols = slice(hh * d, (hh + 1) * d)
                qc = rotary(q_ref, rows, hh).astype(BF16)
                kf = rotary(k_ref, rows, hh) * k_scale
                vc = v_ref[rows, cols]
                sb = sb_scr[hh]
                r = acc_scr[arows, cols] + jnp.dot(qc, sb.astype(BF16), preferred_element_type=F32) * dvec_scr[hh, 2]
                kd = (kf * dvec_scr[hh, 3]).astype(BF16)
                sb_scr[hh] = sb * dvec_scr[hh, 5, 0:1, :] + lax.dot_general(kd, vc, tn, preferred_element_type=F32)
                rn = r * lax.rsqrt(jnp.mean(r * r, axis=-1, keepdims=True) + RMS_EPS)
                gg = g_ref[rows, cols].astype(F32)
                o_ref[rows, cols] = (gg * jax.nn.sigmoid(gg) * rn).astype(o_ref.dtype)
            return carry
        lax.fori_loop(0, GC, body, 0)


def _retention(proj, log_gamma, cos_tab, sin_tab, *, B, LP, H, d, GC, HB):
    T = proj.shape[0]
    GR = GC * CHUNK
    G = LP // GR
    assert H % HB == 0
    H = H // HB
    bw = HB * d

    def grp(s):
        phase = s // G
        w = s % G
        return phase, w + phase * (G - 1 - 2 * w)

    def qkv_map(col0):
        def m(b, h, s):
            _, gi = grp(s)
            return (b * G + gi, col0 + h)
        return m

    def late_map(col0):
        def m(b, h, s):
            phase, gi = grp(s)
            return (b * G + jnp.where(phase == 0, G - 1, gi), col0 + h)
        return m

    def tab_map(b, h, s):
        _, gi = grp(s)
        return (gi, 0)

    return pl.pallas_call(
        functools.partial(_retention_kernel, G=G, GC=GC),
        out_shape=jax.ShapeDtypeStruct((T, H * bw), BF16),
        grid=(B, H, 2 * G),
        in_specs=[
            pl.BlockSpec(memory_space=pltpu.SMEM),
            pl.BlockSpec((GR, bw), qkv_map(0)),
            pl.BlockSpec((GR, bw), qkv_map(H)),
            pl.BlockSpec((GR, bw), qkv_map(2 * H)),
            pl.BlockSpec((GR, bw), late_map(3 * H)),
            pl.BlockSpec((GR, d // 2), tab_map),
            pl.BlockSpec((GR, d // 2), tab_map),
        ],
        out_specs=pl.BlockSpec((GR, bw), late_map(0)),
        scratch_shapes=[
            pltpu.VMEM((LP, bw), F32),
            pltpu.VMEM((HB, d, d), F32),
            pltpu.VMEM((HB, d, d), F32),
            pltpu.VMEM((HB, CHUNK, CHUNK), F32),
            pltpu.VMEM((HB, 6, CHUNK, d), F32),
        ],
        compiler_params=_params(("arbitrary", "arbitrary", "arbitrary")),
        name="retention",
    )(log_gamma, proj, proj, proj, proj, cos_tab, sin_tab)


def _pool_kernel(prev_ref, cur_ref, next_ref, pw_ref, ps_ref, o_ref, *, tiles_per_seq, seq_len):
    i = pl.program_id(0)
    tm = cur_ref.shape[0]
    ext = tm + 2 * POOL_HALO
    n_groups = pw_ref.shape[0]
    cg = pw_ref.shape[1]
    row0 = (i % tiles_per_seq) * tm - PAD
    pos_t = row0 + lax.broadcasted_iota(jnp.int32, (tm, ext), 0)
    pos_s = row0 - POOL_HALO + lax.broadcasted_iota(jnp.int32, (tm, ext), 1)
    pos_c = row0 + lax.broadcasted_iota(jnp.int32, (tm, 1), 0)
    for gi in range(n_groups):
        win = POOL_WINDOWS[gi]
        cols = slice(gi * cg, (gi + 1) * cg)
        lo = jnp.clip(pos_t - win // 2, 0, seq_len)
        hi = jnp.clip(pos_t + (win - win // 2), 0, seq_len)
        band = jnp.where(pos_s >= lo, jnp.where(pos_s < hi, 1.0, 0.0), 0.0).astype(BF16)
        cnt = (jnp.clip(pos_c + (win - win // 2), 0, seq_len) - jnp.clip(pos_c - win // 2, 0, seq_len)).astype(F32)
        inv = jnp.where(pos_c >= 0, 1.0 / jnp.maximum(cnt, 1.0), 0.0)
        u_cur = cur_ref[:, cols]
        u_ext = jnp.concatenate([prev_ref[:, cols], u_cur, next_ref[:, cols]], axis=0)
        wsum = jnp.dot(band, u_ext, preferred_element_type=F32)
        pooled = wsum * inv - u_cur.astype(F32)
        y = jnp.dot(pooled.astype(BF16), pw_ref[gi], preferred_element_type=F32)
        o_ref[:, cols] = (y * ps_ref[:, cols]).astype(o_ref.dtype)


def _pool(proj, pool_w_bf16, pool_scale, *, LP, seq_len, u_col0, tm):
    T = proj.shape[0]
    n_groups, cg, _ = pool_w_bf16.shape
    PW = n_groups * cg
    col_blk = u_col0 // PW
    hb = tm // POOL_HALO
    n_halo = T // POOL_HALO
    return pl.pallas_call(
        functools.partial(_pool_kernel, tiles_per_seq=LP // tm, seq_len=seq_len),
        out_shape=jax.ShapeDtypeStruct((T, PW), BF16),
        grid=(T // tm,),
        in_specs=[
            pl.BlockSpec((POOL_HALO, PW), lambda i: (jnp.maximum(i * hb - 1, 0), col_blk)),
            pl.BlockSpec((tm, PW), lambda i: (i, col_blk)),
            pl.BlockSpec((POOL_HALO, PW), lambda i: (jnp.minimum((i + 1) * hb, n_halo - 1), col_blk)),
            pl.BlockSpec((n_groups, cg, cg), lambda i: (0, 0, 0)),
            pl.BlockSpec((1, PW), lambda i: (0, 0)),
        ],
        out_specs=pl.BlockSpec((tm, PW), lambda i: (i, 0)),
        compiler_params=_params(("arbitrary",)),
        name="pool",
    )(proj, proj, proj, pool_w_bf16, pool_scale.reshape(1, PW))


def _out_proj_kernel(ret_ref, pool_ref, wr_ref, wp_ref, hs_ref, o_ref):
    y = jnp.dot(ret_ref[...], wr_ref[...], preferred_element_type=F32)
    y = y + jnp.dot(pool_ref[...], wp_ref[...], preferred_element_type=F32)
    o_ref[...] = hs_ref[...] + y


def _out_proj(ret, pool, w_out_bf16, hs, *, tm, tn):
    T, RW = ret.shape
    PW = pool.shape[1]
    D = hs.shape[1]
    assert RW == PW, "the two mixer halves share one row-block split of w_out"
    return pl.pallas_call(
        _out_proj_kernel,
        out_shape=jax.ShapeDtypeStruct((T, D), F32),
        grid=(T // tm, D // tn),
        in_specs=[
            pl.BlockSpec((tm, RW), lambda i, j: (i, 0)),
            pl.BlockSpec((tm, PW), lambda i, j: (i, 0)),
            pl.BlockSpec((RW, tn), lambda i, j: (0, j)),
            pl.BlockSpec((PW, tn), lambda i, j: (1, j)),
            pl.BlockSpec((tm, tn), lambda i, j: (i, j)),
        ],
        out_specs=pl.BlockSpec((tm, tn), lambda i, j: (i, j)),
        compiler_params=_params(("arbitrary", "arbitrary")),
        name="out_proj",
    )(ret, pool, w_out_bf16, w_out_bf16, hs)


def _pack_bf16_pair(lo, hi):
    lo_bits = lax.bitcast_convert_type(lo.astype(BF16).astype(F32), jnp.uint32)
    hi_bits = lax.bitcast_convert_type(hi.astype(BF16).astype(F32), jnp.uint32)
    return (hi_bits & jnp.uint32(0xFFFF0000)) | (lo_bits >> jnp.uint32(16))


def _unpack_bf16_pair(word):
    lo = lax.bitcast_convert_type(word << jnp.uint32(16), F32).astype(BF16)
    hi = lax.bitcast_convert_type(word & jnp.uint32(0xFFFF0000), F32).astype(BF16)
    return lo, hi


def _router_kernel(hs_ref, nw_ref, rw_ref, rwhi_ref, rb_ref, valid_ref,
                   fp_ref, eid_ref, gate_ref, rank_ref, cnt_ref,
                   fhi_scr, flo_scr, carry_scr, *, sub):
    i = pl.program_id(0)
    tm, D = hs_ref.shape
    E = rb_ref.shape[0]
    half = D // 2

    @pl.when(i == 0)
    def _():
        carry_scr[...] = jnp.zeros_like(carry_scr)

    def body(r, c):
        rows = pl.ds(pl.multiple_of(r * sub, sub), sub)
        x = hs_ref[rows, :]
        ms = jnp.mean(x * x, axis=-1, keepdims=True)
        f = x * lax.rsqrt(ms + RMS_EPS) * nw_ref[...]
        f_hi = f.astype(BF16)
        fhi_scr[rows, :] = f_hi
        flo_scr[rows, :] = (f - f_hi.astype(F32)).astype(BF16)
        fp_ref[rows, :] = _pack_bf16_pair(f[:, :half], f[:, half:])
        return c
    lax.fori_loop(0, tm // sub, body, 0)

    nt = (((1,), (1,)), ((), ()))
    part = lax.dot_general(rw_ref[...], fhi_scr[...], nt, preferred_element_type=F32)
    logits = part[:E] + part[E:] + lax.dot_general(rwhi_ref[...], flo_scr[...], nt, preferred_element_type=F32)
    logits = logits + rb_ref[...]

    eio = lax.broadcasted_iota(jnp.int32, (E, tm), 0).astype(F32)
    valid = valid_ref[...]

    vals, onehots = [], []
    l = logits
    for k in range(TOP_K):
        m = jnp.max(l, axis=0, keepdims=True)
        idx = jnp.min(jnp.where(l == m, eio, float(E)), axis=0, keepdims=True)
        sel = eio == idx
        vals.append(m)
        onehots.append(jnp.where(sel, valid, 0.0))
        eid_ref[pl.ds(k, 1), :] = idx.astype(jnp.int32)
        l = jnp.where(sel, -jnp.inf, l)

    exps = [jnp.exp(v - vals[0]) for v in vals]
    denom = exps[0] + exps[1] + exps[2] + exps[3]
    for k in range(TOP_K):
        gate_ref[pl.ds(k, 1), :] = exps[k] / denom

    oh = onehots[0] + onehots[1] + onehots[2] + onehots[3]
    tt = lax.broadcasted_iota(jnp.int32, (tm, tm), 0)
    uu = lax.broadcasted_iota(jnp.int32, (tm, tm), 1)
    upper = jnp.where(tt < uu, 1.0, 0.0).astype(BF16)
    carry = carry_scr[...]
    before = jnp.dot(oh.astype(BF16), upper, preferred_element_type=F32) + carry[:, 0:1]
    for k in range(TOP_K):
        rank_ref[pl.ds(k, 1), :] = jnp.sum(onehots[k] * before, axis=0, keepdims=True).astype(jnp.int32)
    carry = carry + jnp.sum(oh, axis=1, keepdims=True)
    carry_scr[...] = carry
    cnt_ref[...] = carry


def _router(hs1, norm_w, router_w, router_b, is_token, *, tm, sub):
    T, D = hs1.shape
    E = router_w.shape[1]
    rw_t = router_w.T.astype(F32)
    rw_hi = rw_t.astype(BF16)
    rw_lo = (rw_t - rw_hi.astype(F32)).astype(BF16)
    rw_stack = jnp.concatenate([rw_hi, rw_lo], axis=0)
    n = T // tm
    return pl.pallas_call(
        functools.partial(_router_kernel, sub=sub),
        out_shape=(
            jax.ShapeDtypeStruct((T, D // 2), jnp.uint32),
            jax.ShapeDtypeStruct((TOP_K, T), jnp.int32),
            jax.ShapeDtypeStruct((TOP_K, T), F32),
            jax.ShapeDtypeStruct((TOP_K, T), jnp.int32),
            jax.ShapeDtypeStruct((E, CHUNK), F32),
        ),
        grid=(n,),
        in_specs=[
            pl.BlockSpec((tm, D), lambda i: (i, 0)),
            pl.BlockSpec((1, D), lambda i: (0, 0)),
            pl.BlockSpec((2 * E, D), lambda i: (0, 0)),
            pl.BlockSpec((E, D), lambda i: (0, 0)),
            pl.BlockSpec((E, 1), lambda i: (0, 0)),
            pl.BlockSpec((1, tm), lambda i: (0, i)),
        ],
        out_specs=(
            pl.BlockSpec((tm, D // 2), lambda i: (i, 0)),
            pl.BlockSpec((TOP_K, tm), lambda i: (0, i)),
            pl.BlockSpec((TOP_K, tm), lambda i: (0, i)),
            pl.BlockSpec((TOP_K, tm), lambda i: (0, i)),
            pl.BlockSpec((E, CHUNK), lambda i: (0, 0)),
        ),
        scratch_shapes=[pltpu.VMEM((tm, D), BF16), pltpu.VMEM((tm, D), BF16), pltpu.VMEM((E, CHUNK), F32)],
        compiler_params=_params(("arbitrary",)),
        name="router",
    )(hs1, norm_w.reshape(1, D), rw_stack, rw_hi, router_b.reshape(E, 1).astype(F32),
      is_token.astype(F32).reshape(1, T))


def _dispatch_kernel(pend_ref, padded_ref, nu_ref, dest_ref, fp_ref, zeros_ref, xs_hbm, zsem, sem, *, moe_tm):
    i = pl.program_id(0)
    tm = fp_ref.shape[0]
    E = pend_ref.shape[0]
    n_blocks = xs_hbm.shape[0] // moe_tm

    def zero_rows(start):
        return pltpu.make_async_copy(zeros_ref, xs_hbm.at[pl.ds(pl.multiple_of(start, moe_tm), moe_tm)], zsem)

    @pl.when(i == 0)
    def _():
        for e in range(E):
            @pl.when(padded_ref[e] > 0)
            def _():
                zero_rows(pend_ref[e] - moe_tm).start()

        def start_tail(b, c):
            zero_rows(b * moe_tm).start()
            return c
        lax.fori_loop(nu_ref[0], n_blocks, start_tail, 0)
        for e in range(E):
            @pl.when(padded_ref[e] > 0)
            def _():
                zero_rows(pend_ref[e] - moe_tm).wait()

        def wait_tail(b, c):
            zero_rows(b * moe_tm).wait()
            return c
        lax.fori_loop(nu_ref[0], n_blocks, wait_tail, 0)

    def body(r, c):
        for k in range(TOP_K):
            pltpu.make_async_copy(fp_ref.at[pl.ds(r, 1)], xs_hbm.at[pl.ds(dest_ref[k, r], 1)], sem).start()
        return c
    lax.fori_loop(0, tm, body, 0)
    for k in range(TOP_K):
        pltpu.make_async_copy(fp_ref, xs_hbm.at[pl.ds(0, tm)], sem).wait()


def _dispatch(fp, dest3, pend, padded, n_used, *, rows_total, moe_tm, tile):
    T, W = fp.shape
    assert rows_total % moe_tm == 0
    zeros = jnp.zeros((moe_tm, W), fp.dtype)
    grid_spec = pltpu.PrefetchScalarGridSpec(
        num_scalar_prefetch=3,
        grid=(T // tile,),
        in_specs=[
            pl.BlockSpec((None, TOP_K, tile), lambda i, *_: (i, 0, 0), memory_space=pltpu.SMEM),
            pl.BlockSpec((tile, W), lambda i, *_: (i, 0)),
            pl.BlockSpec((moe_tm, W), lambda i, *_: (0, 0)),
        ],
        out_specs=pl.BlockSpec(memory_space=pl.ANY),
        scratch_shapes=[pltpu.SemaphoreType.DMA, pltpu.SemaphoreType.DMA],
    )
    return pl.pallas_call(
        functools.partial(_dispatch_kernel, moe_tm=moe_tm),
        out_shape=jax.ShapeDtypeStruct((rows_total, W), fp.dtype),
        grid_spec=grid_spec,
        compiler_params=_params(("arbitrary",)),
        name="dispatch",
    )(pend, padded, n_used, dest3, fp, zeros)


def _gate_up_kernel(te_ref, nu_ref, xs_ref, wg_ref, wu_ref, bg_ref, bu_ref, h_ref):
    i = pl.program_id(1)
    active = i < nu_ref[0]

    @pl.when(active)
    def _():
        lo, hi = _unpack_bf16_pair(xs_ref[...])
        x = jnp.concatenate([lo, hi], axis=-1)
        gate = jnp.dot(x, wg_ref[...].astype(BF16), preferred_element_type=F32) + bg_ref[...]
        up = jnp.dot(x, wu_ref[...].astype(BF16), preferred_element_type=F32) + bu_ref[...]
        gate = jnp.minimum(gate, SWIGLU_LIMIT)
        up = jnp.clip(up, -SWIGLU_LIMIT, SWIGLU_LIMIT)
        h_ref[...] = ((up + 1.0) * gate * jax.nn.sigmoid(SWIGLU_ALPHA * gate)).astype(h_ref.dtype)

    @pl.when(jnp.logical_not(active))
    def _():
        h_ref[...] = jnp.zeros_like(h_ref)


def _gate_up(xs, tile_e, n_used, w_gate_up, b_gate_up, *, n_tiles, tm, tn):
    E, D, F2 = w_gate_up.shape
    FF = F2 // 2
    nj = FF // tn
    W = xs.shape[1]

    def row(i, te, nu):
        return jnp.minimum(i, nu[0] - 1)

    grid_spec = pltpu.PrefetchScalarGridSpec(
        num_scalar_prefetch=2,
        grid=(nj, n_tiles),
        in_specs=[
            pl.BlockSpec((tm, W), lambda j, i, te, nu: (row(i, te, nu), 0)),
            pl.BlockSpec((None, D, tn), lambda j, i, te, nu: (te[row(i, te, nu)], 0, j)),
            pl.BlockSpec((None, D, tn), lambda j, i, te, nu: (te[row(i, te, nu)], 0, nj + j)),
            pl.BlockSpec((None, 1, tn), lambda j, i, te, nu: (te[row(i, te, nu)], 0, j)),
            pl.BlockSpec((None, 1, tn), lambda j, i, te, nu: (te[row(i, te, nu)], 0, nj + j)),
        ],
        out_specs=pl.BlockSpec((tm, tn), lambda j, i, te, nu: (i, j)),
    )
    b3 = b_gate_up.reshape(E, 1, F2)
    return pl.pallas_call(
        _gate_up_kernel,
        out_shape=jax.ShapeDtypeStruct((n_tiles * tm, FF), BF16),
        grid_spec=grid_spec,
        compiler_params=_params(("arbitrary", "arbitrary")),
        name="gate_up",
    )(tile_e, n_used, xs, w_gate_up, w_gate_up, b3, b3)


def _down_kernel(te_ref, nu_ref, h_ref, w_ref, b_ref, y_ref):
    i = pl.program_id(1)
    active = i < nu_ref[0]

    @pl.when(active)
    def _():
        y_ref[...] = jnp.dot(h_ref[...], w_ref[...].astype(BF16), preferred_element_type=F32) + b_ref[...]

    @pl.when(jnp.logical_not(active))
    def _():
        y_ref[...] = jnp.zeros_like(y_ref)


def _down(h, tile_e, n_used, w_down, b_down, *, n_tiles, tm, tn):
    E, FF, D = w_down.shape

    def row(i, te, nu):
        return jnp.minimum(i, nu[0] - 1)

    grid_spec = pltpu.PrefetchScalarGridSpec(
        num_scalar_prefetch=2,
        grid=(D // tn, n_tiles),
        in_specs=[
            pl.BlockSpec((tm, FF), lambda j, i, te, nu: (row(i, te, nu), 0)),
            pl.BlockSpec((None, FF, tn), lambda j, i, te, nu: (te[row(i, te, nu)], 0, j)),
            pl.BlockSpec((None, 1, tn), lambda j, i, te, nu: (te[row(i, te, nu)], 0, j)),
        ],
        out_specs=pl.BlockSpec((tm, tn), lambda j, i, te, nu: (i, j)),
    )
    return pl.pallas_call(
        _down_kernel,
        out_shape=jax.ShapeDtypeStruct((n_tiles * tm, D), F32),
        grid_spec=grid_spec,
        compiler_params=_params(("arbitrary", "arbitrary")),
        name="down",
    )(tile_e, n_used, h, w_down, b_down.reshape(E, 1, D))


def _combine_kernel(dest_ref, dnext_ref, hs_ref, gate_ref, nw_ref, y_hbm, o_ref, ybuf, sem, *, sub):
    s = pl.program_id(0)
    n = pl.num_programs(0)
    tm = hs_ref.shape[0]
    slot = s % 2

    def start_gather(d_ref, to_slot):
        def body(r, c):
            for k in range(TOP_K):
                pltpu.make_async_copy(y_hbm.at[pl.ds(d_ref[k, r], 1)], ybuf.at[to_slot, k, pl.ds(r, 1)],
                                      sem.at[to_slot]).start()
            return c
        lax.fori_loop(0, tm, body, 0)

    @pl.when(s == 0)
    def _():
        start_gather(dest_ref, 0)

    @pl.when(s + 1 < n)
    def _():
        start_gather(dnext_ref, 1 - slot)

    for k in range(TOP_K):
        pltpu.make_async_copy(y_hbm.at[pl.ds(0, tm)], ybuf.at[slot, k], sem.at[slot]).wait()

    def body(r, c):
        rows = pl.ds(pl.multiple_of(r * sub, sub), sub)
        g = gate_ref[rows, :]
        acc = hs_ref[rows, :]
        for k in range(TOP_K):
            acc = acc + ybuf[slot, k, rows, :] * g[:, k:k + 1]
        ms = jnp.mean(acc * acc, axis=-1, keepdims=True)
        o_ref[rows, :] = acc * lax.rsqrt(ms + RMS_EPS) * nw_ref[...]
        return c
    lax.fori_loop(0, tm // sub, body, 0)


def _combine(hs1, gates_t, dest3, y, final_norm_w, *, B, LP, seq_len, tile, sub):
    T, D = hs1.shape
    per_seq = LP // tile
    n_seq_tiles = seq_len // tile
    lead = (LP - seq_len) // tile
    n_steps = B * n_seq_tiles

    def tok(s):
        return (s // n_seq_tiles) * per_seq + lead + s % n_seq_tiles

    return pl.pallas_call(
        functools.partial(_combine_kernel, sub=sub),
        out_shape=jax.ShapeDtypeStruct((B, seq_len, D), F32),
        grid=(n_steps,),
        in_specs=[
            pl.BlockSpec((None, TOP_K, tile), lambda s: (tok(s), 0, 0), memory_space=pltpu.SMEM),
            pl.BlockSpec((None, TOP_K, tile), lambda s: (tok(jnp.minimum(s + 1, n_steps - 1)), 0, 0),
                         memory_space=pltpu.SMEM),
            pl.BlockSpec((tile, D), lambda s: (tok(s), 0)),
            pl.BlockSpec((tile, TOP_K), lambda s: (tok(s), 0)),
            pl.BlockSpec((1, D), lambda s: (0, 0)),
            pl.BlockSpec(memory_space=pl.ANY),
        ],
        out_specs=pl.BlockSpec((None, tile, D), lambda s: (s // n_seq_tiles, s % n_seq_tiles, 0)),
        scratch_shapes=[pltpu.VMEM((2, TOP_K, tile, D), F32), pltpu.SemaphoreType.DMA((2,))],
        compiler_params=_params(("arbitrary",)),
        name="combine",
    )(dest3, dest3, hs1, gates_t, final_norm_w.reshape(1, D), y)


def _forward(x, meta_tokens, norm_mix_w, w_in, ret_log_rate_fwd, ret_log_rate_bwd, pool_w, pool_scale,
             w_out, norm_ffn_w, router_w, router_b, w_gate_up, b_gate_up, w_down, b_down, final_norm_w, tiles):
    B, S, D = x.shape
    depth = norm_mix_w.shape[0]
    assert depth == 1, "single-layer block"
    H = ret_log_rate_fwd.shape[-1]
    n_groups, cg = pool_w.shape[1], pool_w.shape[2]
    PW = n_groups * cg
    RW = w_out.shape[1] - PW
    d = RW // H
    E = router_w.shape[-1]
    assert w_in.shape[-1] == 4 * RW + PW and S % CHUNK == 0
    LP = PAD + N_META + S
    T = B * LP
    t = tiles

    hs0 = jnp.concatenate(
        [jnp.zeros((B, PAD, D), x.dtype), jnp.broadcast_to(meta_tokens.astype(x.dtype)[None], (B, N_META, D)), x],
        axis=1).reshape(T, D)

    proj = _in_proj(hs0, norm_mix_w[0], w_in[0].astype(BF16), tm=t["in_tm"], tn=t["in_tn"], sub=t["norm_rows"])
    pos = (jnp.arange(LP, dtype=jnp.int32) - PAD).astype(F32)
    inv_freq = ROPE_BASE ** (-jnp.arange(d // 2, dtype=F32) / (d // 2))
    ang = pos[:, None] * inv_freq[None, :]
    log_gamma = jnp.stack([-jnp.exp(ret_log_rate_fwd[0].astype(F32)), -jnp.exp(ret_log_rate_bwd[0].astype(F32))])
    ret = _retention(proj, log_gamma, jnp.cos(ang), jnp.sin(ang), B=B, LP=LP, H=H, d=d, GC=t["ret_group_chunks"],
                     HB=t["ret_heads_per_step"])
    pooled = _pool(proj, pool_w[0].astype(BF16), pool_scale[0].astype(F32), LP=LP, seq_len=N_META + S,
                   u_col0=4 * RW, tm=t["pool_tm"])
    hs1 = _out_proj(ret, pooled, w_out[0].astype(BF16), hs0, tm=t["out_tm"], tn=t["out_tn"])

    row_id = jnp.arange(T, dtype=jnp.int32)
    is_token = (row_id % LP) >= PAD
    fp, eid, gates, rank, cnt = _router(hs1, norm_ffn_w[0], router_w[0], router_b[0], is_token,
                                        tm=t["router_tm"], sub=t["norm_rows"])
    moe_tm = t["moe_tm"]
    tile = t["token_tile"]
    counts = cnt[:, 0].astype(jnp.int32)
    padded = (counts + moe_tm - 1) // moe_tm * moe_tm
    pend = jnp.cumsum(padded)
    pstart = pend - padded
    n_assign = B * (N_META + S) * TOP_K
    n_tiles = -(-(n_assign + E * (moe_tm - 1)) // moe_tm)
    rows = n_tiles * moe_tm
    tile_start = jnp.arange(n_tiles, dtype=jnp.int32) * moe_tm
    tile_e = jnp.minimum(jnp.sum((pend[None, :] <= tile_start[:, None]).astype(jnp.int32), axis=1), E - 1)
    n_used = (pend[-1:] // moe_tm).astype(jnp.int32)
    spare = rows + jnp.arange(TOP_K, dtype=jnp.int32)[:, None] * tile + (row_id % tile)[None, :]
    expert_ids = jnp.arange(E, dtype=jnp.int32)[:, None, None]
    first_row = jnp.sum(jnp.where(eid[None] == expert_ids, pstart[:, None, None], 0), axis=0)
    dest = jnp.where(is_token[None, :], first_row + rank, spare)
    dest3 = dest.reshape(TOP_K, T // tile, tile).transpose(1, 0, 2)

    spare_rows = -(-(TOP_K * tile) // moe_tm) * moe_tm
    xs = _dispatch(fp, dest3, pend.astype(jnp.int32), padded, n_used, rows_total=rows + spare_rows,
                   moe_tm=moe_tm, tile=tile)
    hmid = _gate_up(xs, tile_e, n_used, w_gate_up[0], b_gate_up[0], n_tiles=n_tiles, tm=moe_tm, tn=t["gu_tn"])
    y = _down(hmid, tile_e, n_used, w_down[0], b_down[0], n_tiles=n_tiles, tm=moe_tm, tn=t["down_tn"])
    return _combine(hs1, gates.T, dest3, y, final_norm_w, B=B, LP=LP, seq_len=S, tile=tile, sub=t["combine_rows"])


def kernel(x, meta_tokens, norm_mix_w, w_in, ret_log_rate_fwd, ret_log_rate_bwd, pool_w, pool_scale, w_out,
           norm_ffn_w, router_w, router_b, w_gate_up, b_gate_up, w_down, b_down, final_norm_w):
    return _forward(x, meta_tokens, norm_mix_w, w_in, ret_log_rate_fwd, ret_log_rate_bwd, pool_w, pool_scale,
                    w_out, norm_ffn_w, router_w, router_b, w_gate_up, b_gate_up, w_down, b_down, final_norm_w,
                    DEFAULT_TILES)
```

```python
import functools

import jax
import jax.numpy as jnp
from jax import lax
from jax.experimental import pallas as pl
from jax.experimental.pallas import tpu as pltpu

N_META = 16
CHUNK = 128
PAD = CHUNK - N_META
POOL_WINDOWS = (2, 4, 8, 16)
POOL_HALO = 64
TOP_K = 4
GROUP_SIZES = (512, 256, 128)
SWIGLU_LIMIT = 7.0
SWIGLU_ALPHA = 1.702
RMS_EPS = 1e-5
ROPE_BASE = 10000.0

V7X_VMEM_BYTES = 64 * 1024 * 1024
VMEM_LIMIT = V7X_VMEM_BYTES - 8 * 1024 * 1024

F32 = jnp.float32
BF16 = jnp.bfloat16

DEFAULT_TILES = dict(
    in_tm=640, in_tn=1024, norm_rows=32,
    ret_group_chunks=13, ret_heads_per_step=2,
    pool_tm=640,
    out_tm=640, out_tn=1024,
    router_tm=256,
    gu_tn=512, down_tn=2048,
    token_tile=128, combine_rows=32,
)


def _params(semantics):
    return pltpu.CompilerParams(dimension_semantics=semantics, vmem_limit_bytes=VMEM_LIMIT)


def _in_proj_kernel(x_ref, nw_ref, w_ref, o_ref, a_scr, *, sub):
    @pl.when(pl.program_id(1) == 0)
    def _():
        def body(r, c):
            rows = pl.ds(pl.multiple_of(r * sub, sub), sub)
            x = x_ref[rows, :]
            ms = jnp.mean(x * x, axis=-1, keepdims=True)
            a_scr[rows, :] = (x * lax.rsqrt(ms + RMS_EPS) * nw_ref[...]).astype(a_scr.dtype)
            return c
        lax.fori_loop(0, x_ref.shape[0] // sub, body, 0)

    o_ref[...] = jnp.dot(a_scr[...], w_ref[...], preferred_element_type=F32).astype(o_ref.dtype)


def _in_proj(hs, norm_w, w_in_bf16, *, tm, tn, sub):
    T, D = hs.shape
    N = w_in_bf16.shape[1]
    return pl.pallas_call(
        functools.partial(_in_proj_kernel, sub=sub),
        out_shape=jax.ShapeDtypeStruct((T, N), BF16),
        grid=(T // tm, N // tn),
        in_specs=[
            pl.BlockSpec((tm, D), lambda i, j: (i, 0)),
            pl.BlockSpec((1, D), lambda i, j: (0, 0)),
            pl.BlockSpec((D, tn), lambda i, j: (0, j)),
        ],
        out_specs=pl.BlockSpec((tm, tn), lambda i, j: (i, j)),
        scratch_shapes=[pltpu.VMEM((tm, D), BF16)],
        compiler_params=_params(("arbitrary", "arbitrary")),
        name="in_proj",
    )(hs, norm_w.reshape(1, D), w_in_bf16)


def _retention_kernel(lg_ref, q_ref, k_ref, v_ref, g_ref, cos_ref, sin_ref, o_ref,
                      acc_scr, sf_scr, sb_scr, dmat_scr, dvec_scr, *, G, GC):
    hb = pl.program_id(1)
    s = pl.program_id(2)
    phase = s // G
    w = s % G
    gi = w + phase * (G - 1 - 2 * w)
    C = CHUNK
    d = cos_ref.shape[1] * 2
    half = d // 2
    HB = q_ref.shape[1] // d
    GR = GC * C
    k_scale = d ** -0.5
    nt = (((1,), (1,)), ((), ()))
    tn = (((0,), (0,)), ((), ()))

    @pl.when(s == 0)
    def _():
        t = lax.broadcasted_iota(jnp.int32, (C, C), 0).astype(F32)
        u = lax.broadcasted_iota(jnp.int32, (C, C), 1).astype(F32)
        diff = t - u
        idx = lax.broadcasted_iota(jnp.int32, (C, d), 0).astype(F32)
        for hh in range(HB):
            lgf = lg_ref[0, hb * HB + hh]
            lgb = lg_ref[1, hb * HB + hh]
            dmat_scr[hh] = jnp.where(diff >= 0.0, jnp.exp(lgf * jnp.maximum(diff, 0.0)),
                                     jnp.exp(lgb * jnp.maximum(-diff, 0.0)))
            dvec_scr[hh, 0] = jnp.exp(lgf * (idx + 1.0))
            dvec_scr[hh, 1] = jnp.exp(lgf * (C - 1.0 - idx))
            dvec_scr[hh, 2] = jnp.exp(lgb * (C - idx))
            dvec_scr[hh, 3] = jnp.exp(lgb * idx)
            dvec_scr[hh, 4] = jnp.exp(jnp.full((C, d), lgf * C, F32))
            dvec_scr[hh, 5] = jnp.exp(jnp.full((C, d), lgb * C, F32))
        sf_scr[...] = jnp.zeros_like(sf_scr)

    @pl.when(s == G)
    def _():
        sb_scr[...] = jnp.zeros_like(sb_scr)

    def rotary(ref, rows, hh):
        x = ref[rows, hh * d:(hh + 1) * d].astype(F32)
        x1 = x[:, :half]
        x2 = x[:, half:]
        cs = cos_ref[rows, :]
        sn = sin_ref[rows, :]
        return jnp.concatenate([x1 * cs - x2 * sn, x2 * cs + x1 * sn], axis=-1)

    @pl.when(phase == 0)
    def _():
        def body(c, carry):
            rows = pl.ds(pl.multiple_of(c * C, C), C)
            arows = pl.ds(pl.multiple_of(gi * GR + c * C, C), C)
            for hh in range(HB):
                cols = slice(hh * d, (hh + 1) * d)
                qc = rotary(q_ref, rows, hh).astype(BF16)
                kf = rotary(k_ref, rows, hh) * k_scale
                vc = v_ref[rows, cols]
                sc = lax.dot_general(qc, kf.astype(BF16), nt, preferred_element_type=F32)
                p = (sc * dmat_scr[hh]).astype(BF16)
                inner = jnp.dot(p, vc, preferred_element_type=F32)
                sf = sf_scr[hh]
                cross = jnp.dot(qc, sf.astype(BF16), preferred_element_type=F32) * dvec_scr[hh, 0]
                acc_scr[arows, cols] = inner + cross
                kd = (kf * dvec_scr[hh, 1]).astype(BF16)
                sf_scr[hh] = sf * dvec_scr[hh, 4, 0:1, :] + lax.dot_general(kd, vc, tn, preferred_element_type=F32)
            return carry
        lax.fori_loop(0, GC, body, 0)

    @pl.when(phase == 1)
    def _():
        def body(cc, carry):
            c = GC - 1 - cc
            rows = pl.ds(pl.multiple_of(c * C, C), C)
            arows = pl.ds(pl.multiple_of(gi * GR + c * C, C), C)
            for hh in range(HB):
                cols = slice(hh * d, (hh + 1) * d)
                qc = rotary(q_ref, rows, hh).astype(BF16)
                kf = rotary(k_ref, rows, hh) * k_scale
                vc = v_ref[rows, cols]
                sb = sb_scr[hh]
                r = acc_scr[arows, cols] + jnp.dot(qc, sb.astype(BF16), preferred_element_type=F32) * dvec_scr[hh, 2]
                kd = (kf * dvec_scr[hh, 3]).astype(BF16)
                sb_scr[hh] = sb * dvec_scr[hh, 5, 0:1, :] + lax.dot_general(kd, vc, tn, preferred_element_type=F32)
                rn = r * lax.rsqrt(jnp.mean(r * r, axis=-1, keepdims=True) + RMS_EPS)
                gg = g_ref[rows, cols].astype(F32)
                o_ref[rows, cols] = (gg * jax.nn.sigmoid(gg) * rn).astype(o_ref.dtype)
            return carry
        lax.fori_loop(0, GC, body, 0)


def _retention(proj, log_gamma, cos_tab, sin_tab, *, B, LP, H, d, GC, HB):
    T = proj.shape[0]
    GR = GC * CHUNK
    G = LP // GR
    assert H % HB == 0
    H = H // HB
    bw = HB * d

    def grp(s):
        phase = s // G
        w = s % G
        return phase, w + phase * (G - 1 - 2 * w)

    def qkv_map(col0):
        def m(b, h, s):
            _, gi = grp(s)
            return (b * G + gi, col0 + h)
        return m

    def late_map(col0):
        def m(b, h, s):
            phase, gi = grp(s)
            return (b * G + jnp.where(phase == 0, G - 1, gi), col0 + h)
        return m

    def tab_map(b, h, s):
        _, gi = grp(s)
        return (gi, 0)

    return pl.pallas_call(
        functools.partial(_retention_kernel, G=G, GC=GC),
        out_shape=jax.ShapeDtypeStruct((T, H * bw), BF16),
        grid=(B, H, 2 * G),
        in_specs=[
            pl.BlockSpec(memory_space=pltpu.SMEM),
            pl.BlockSpec((GR, bw), qkv_map(0)),
            pl.BlockSpec((GR, bw), qkv_map(H)),
            pl.BlockSpec((GR, bw), qkv_map(2 * H)),
            pl.BlockSpec((GR, bw), late_map(3 * H)),
            pl.BlockSpec((GR, d // 2), tab_map),
            pl.BlockSpec((GR, d // 2), tab_map),
        ],
        out_specs=pl.BlockSpec((GR, bw), late_map(0)),
        scratch_shapes=[
            pltpu.VMEM((LP, bw), F32),
            pltpu.VMEM((HB, d, d), F32),
            pltpu.VMEM((HB, d, d), F32),
            pltpu.VMEM((HB, CHUNK, CHUNK), F32),
            pltpu.VMEM((HB, 6, CHUNK, d), F32),
        ],
        compiler_params=_params(("arbitrary", "arbitrary", "arbitrary")),
        name="retention",
    )(log_gamma, proj, proj, proj, proj, cos_tab, sin_tab)


def _pool_kernel(prev_ref, cur_ref, next_ref, pw_ref, ps_ref, o_ref, *, tiles_per_seq, seq_len):
    i = pl.program_id(0)
    tm = cur_ref.shape[0]
    ext = tm + 2 * POOL_HALO
    n_groups = pw_ref.shape[0]
    cg = pw_ref.shape[1]
    row0 = (i % tiles_per_seq) * tm - PAD
    pos_t = row0 + lax.broadcasted_iota(jnp.int32, (tm, ext), 0)
    pos_s = row0 - POOL_HALO + lax.broadcasted_iota(jnp.int32, (tm, ext), 1)
    pos_c = row0 + lax.broadcasted_iota(jnp.int32, (tm, 1), 0)
    for gi in range(n_groups):
        win = POOL_WINDOWS[gi]
        cols = slice(gi * cg, (gi + 1) * cg)
        lo = jnp.clip(pos_t - win // 2, 0, seq_len)
        hi = jnp.clip(pos_t + (win - win // 2), 0, seq_len)
        band = jnp.where(pos_s >= lo, jnp.where(pos_s < hi, 1.0, 0.0), 0.0).astype(BF16)
        cnt = (jnp.clip(pos_c + (win - win // 2), 0, seq_len) - jnp.clip(pos_c - win // 2, 0, seq_len)).astype(F32)
        inv = jnp.where(pos_c >= 0, 1.0 / jnp.maximum(cnt, 1.0), 0.0)
        u_cur = cur_ref[:, cols]
        u_ext = jnp.concatenate([prev_ref[:, cols], u_cur, next_ref[:, cols]], axis=0)
        wsum = jnp.dot(band, u_ext, preferred_element_type=F32)
        pooled = wsum * inv - u_cur.astype(F32)
        y = jnp.dot(pooled.astype(BF16), pw_ref[gi], preferred_element_type=F32)
        o_ref[:, cols] = (y * ps_ref[:, cols]).astype(o_ref.dtype)


def _pool(proj, pool_w_bf16, pool_scale, *, LP, seq_len, u_col0, tm):
    T = proj.shape[0]
    n_groups, cg, _ = pool_w_bf16.shape
    PW = n_groups * cg
    col_blk = u_col0 // PW
    hb = tm // POOL_HALO
    n_halo = T // POOL_HALO
    return pl.pallas_call(
        functools.partial(_pool_kernel, tiles_per_seq=LP // tm, seq_len=seq_len),
        out_shape=jax.ShapeDtypeStruct((T, PW), BF16),
        grid=(T // tm,),
        in_specs=[
            pl.BlockSpec((POOL_HALO, PW), lambda i: (jnp.maximum(i * hb - 1, 0), col_blk)),
            pl.BlockSpec((tm, PW), lambda i: (i, col_blk)),
            pl.BlockSpec((POOL_HALO, PW), lambda i: (jnp.minimum((i + 1) * hb, n_halo - 1), col_blk)),
            pl.BlockSpec((n_groups, cg, cg), lambda i: (0, 0, 0)),
            pl.BlockSpec((1, PW), lambda i: (0, 0)),
        ],
        out_specs=pl.BlockSpec((tm, PW), lambda i: (i, 0)),
        compiler_params=_params(("arbitrary",)),
        name="pool",
    )(proj, proj, proj, pool_w_bf16, pool_scale.reshape(1, PW))


def _out_proj_kernel(ret_ref, pool_ref, wr_ref, wp_ref, hs_ref, o_ref):
    y = jnp.dot(ret_ref[...], wr_ref[...], preferred_element_type=F32)
    y = y + jnp.dot(pool_ref[...], wp_ref[...], preferred_element_type=F32)
    o_ref[...] = hs_ref[...] + y


def _out_proj(ret, pool, w_out_bf16, hs, *, tm, tn):
    T, RW = ret.shape
    PW = pool.shape[1]
    D = hs.shape[1]
    assert RW == PW, "the two mixer halves share one row-block split of w_out"
    return pl.pallas_call(
        _out_proj_kernel,
        out_shape=jax.ShapeDtypeStruct((T, D), F32),
        grid=(T // tm, D // tn),
        in_specs=[
            pl.BlockSpec((tm, RW), lambda i, j: (i, 0)),
            pl.BlockSpec((tm, PW), lambda i, j: (i, 0)),
            pl.BlockSpec((RW, tn), lambda i, j: (0, j)),
            pl.BlockSpec((PW, tn), lambda i, j: (1, j)),
            pl.BlockSpec((tm, tn), lambda i, j: (i, j)),
        ],
        out_specs=pl.BlockSpec((tm, tn), lambda i, j: (i, j)),
        compiler_params=_params(("arbitrary", "arbitrary")),
        name="out_proj",
    )(ret, pool, w_out_bf16, w_out_bf16, hs)


def _pack_bf16_pair(lo, hi):
    lo_bits = lax.bitcast_convert_type(lo.astype(BF16).astype(F32), jnp.uint32)
    hi_bits = lax.bitcast_convert_type(hi.astype(BF16).astype(F32), jnp.uint32)
    return (hi_bits & jnp.uint32(0xFFFF0000)) | (lo_bits >> jnp.uint32(16))


def _unpack_bf16_pair(word):
    lo = lax.bitcast_convert_type(word << jnp.uint32(16), F32).astype(BF16)
    hi = lax.bitcast_convert_type(word & jnp.uint32(0xFFFF0000), F32).astype(BF16)
    return lo, hi


def _router_kernel(hs_ref, nw_ref, rw_ref, rwhi_ref, rb_ref, valid_ref,
                   fp_ref, eid_ref, gate_ref, rank_ref, cnt_ref,
                   fhi_scr, flo_scr, carry_scr, *, sub):
    i = pl.program_id(0)
    tm, D = hs_ref.shape
    E = rb_ref.shape[0]
    half = D // 2

    @pl.when(i == 0)
    def _():
        carry_scr[...] = jnp.zeros_like(carry_scr)

    def body(r, c):
        rows = pl.ds(pl.multiple_of(r * sub, sub), sub)
        x = hs_ref[rows, :]
        ms = jnp.mean(x * x, axis=-1, keepdims=True)
        f = x * lax.rsqrt(ms + RMS_EPS) * nw_ref[...]
        f_hi = f.astype(BF16)
        fhi_scr[rows, :] = f_hi
        flo_scr[rows, :] = (f - f_hi.astype(F32)).astype(BF16)
        fp_ref[rows, :] = _pack_bf16_pair(f[:, :half], f[:, half:])
        return c
    lax.fori_loop(0, tm // sub, body, 0)

    nt = (((1,), (1,)), ((), ()))
    part = lax.dot_general(rw_ref[...], fhi_scr[...], nt, preferred_element_type=F32)
    logits = part[:E] + part[E:] + lax.dot_general(rwhi_ref[...], flo_scr[...], nt, preferred_element_type=F32)
    logits = logits + rb_ref[...]

    eio = lax.broadcasted_iota(jnp.int32, (E, tm), 0).astype(F32)
    valid = valid_ref[...]

    vals, onehots = [], []
    l = logits
    for k in range(TOP_K):
        m = jnp.max(l, axis=0, keepdims=True)
        idx = jnp.min(jnp.where(l == m, eio, float(E)), axis=0, keepdims=True)
        sel = eio == idx
        vals.append(m)
        onehots.append(jnp.where(sel, valid, 0.0))
        eid_ref[pl.ds(k, 1), :] = idx.astype(jnp.int32)
        l = jnp.where(sel, -jnp.inf, l)

    exps = [jnp.exp(v - vals[0]) for v in vals]
    denom = exps[0] + exps[1] + exps[2] + exps[3]
    for k in range(TOP_K):
        gate_ref[pl.ds(k, 1), :] = exps[k] / denom

    oh = onehots[0] + onehots[1] + onehots[2] + onehots[3]
    tt = lax.broadcasted_iota(jnp.int32, (tm, tm), 0)
    uu = lax.broadcasted_iota(jnp.int32, (tm, tm), 1)
    upper = jnp.where(tt < uu, 1.0, 0.0).astype(BF16)
    carry = carry_scr[...]
    before = jnp.dot(oh.astype(BF16), upper, preferred_element_type=F32) + carry[:, 0:1]
    for k in range(TOP_K):
        rank_ref[pl.ds(k, 1), :] = jnp.sum(onehots[k] * before, axis=0, keepdims=True).astype(jnp.int32)
    carry = carry + jnp.sum(oh, axis=1, keepdims=True)
    carry_scr[...] = carry
    cnt_ref[...] = carry


def _router(hs1, norm_w, router_w, router_b, is_token, *, tm, sub):
    T, D = hs1.shape
    E = router_w.shape[1]
    rw_t = router_w.T.astype(F32)
    rw_hi = rw_t.astype(BF16)
    rw_lo = (rw_t - rw_hi.astype(F32)).astype(BF16)
    rw_stack = jnp.concatenate([rw_hi, rw_lo], axis=0)
    n = T // tm
    return pl.pallas_call(
        functools.partial(_router_kernel, sub=sub),
        out_shape=(
            jax.ShapeDtypeStruct((T, D // 2), jnp.uint32),
            jax.ShapeDtypeStruct((TOP_K, T), jnp.int32),
            jax.ShapeDtypeStruct((TOP_K, T), F32),
            jax.ShapeDtypeStruct((TOP_K, T), jnp.int32),
            jax.ShapeDtypeStruct((E, CHUNK), F32),
        ),
        grid=(n,),
        in_specs=[
            pl.BlockSpec((tm, D), lambda i: (i, 0)),
            pl.BlockSpec((1, D), lambda i: (0, 0)),
            pl.BlockSpec((2 * E, D), lambda i: (0, 0)),
            pl.BlockSpec((E, D), lambda i: (0, 0)),
            pl.BlockSpec((E, 1), lambda i: (0, 0)),
            pl.BlockSpec((1, tm), lambda i: (0, i)),
        ],
        out_specs=(
            pl.BlockSpec((tm, D // 2), lambda i: (i, 0)),
            pl.BlockSpec((TOP_K, tm), lambda i: (0, i)),
            pl.BlockSpec((TOP_K, tm), lambda i: (0, i)),
            pl.BlockSpec((TOP_K, tm), lambda i: (0, i)),
            pl.BlockSpec((E, CHUNK), lambda i: (0, 0)),
        ),
        scratch_shapes=[pltpu.VMEM((tm, D), BF16), pltpu.VMEM((tm, D), BF16), pltpu.VMEM((E, CHUNK), F32)],
        compiler_params=_params(("arbitrary",)),
        name="router",
    )(hs1, norm_w.reshape(1, D), rw_stack, rw_hi, router_b.reshape(E, 1).astype(F32),
      is_token.astype(F32).reshape(1, T))


def _dispatch_kernel(pend_ref, padded_ref, nu_ref, dest_ref, fp_ref, zeros_ref, xs_hbm, zsem, sem, *, moe_tm):
    i = pl.program_id(0)
    tm = fp_ref.shape[0]
    E = pend_ref.shape[0]
    n_blocks = xs_hbm.shape[0] // moe_tm

    def zero_rows(start):
        return pltpu.make_async_copy(zeros_ref, xs_hbm.at[pl.ds(pl.multiple_of(start, moe_tm), moe_tm)], zsem)

    @pl.when(i == 0)
    def _():
        for e in range(E):
            @pl.when(padded_ref[e] > 0)
            def _():
                zero_rows(pend_ref[e] - moe_tm).start()

        def start_tail(b, c):
            zero_rows(b * moe_tm).start()
            return c
        lax.fori_loop(nu_ref[0], n_blocks, start_tail, 0)
        for e in range(E):
            @pl.when(padded_ref[e] > 0)
            def _():
                zero_rows(pend_ref[e] - moe_tm).wait()

        def wait_tail(b, c):
            zero_rows(b * moe_tm).wait()
            return c
        lax.fori_loop(nu_ref[0], n_blocks, wait_tail, 0)

    def body(r, c):
        for k in range(TOP_K):
            pltpu.make_async_copy(fp_ref.at[pl.ds(r, 1)], xs_hbm.at[pl.ds(dest_ref[k, r], 1)], sem).start()
        return c
    lax.fori_loop(0, tm, body, 0)
    for k in range(TOP_K):
        pltpu.make_async_copy(fp_ref, xs_hbm.at[pl.ds(0, tm)], sem).wait()


def _dispatch(fp, dest3, pend, padded, n_used, *, rows_total, moe_tm, tile):
    T, W = fp.shape
    assert rows_total % moe_tm == 0
    zeros = jnp.zeros((moe_tm, W), fp.dtype)
    grid_spec = pltpu.PrefetchScalarGridSpec(
        num_scalar_prefetch=3,
        grid=(T // tile,),
        in_specs=[
            pl.BlockSpec((None, TOP_K, tile), lambda i, *_: (i, 0, 0), memory_space=pltpu.SMEM),
            pl.BlockSpec((tile, W), lambda i, *_: (i, 0)),
            pl.BlockSpec((moe_tm, W), lambda i, *_: (0, 0)),
        ],
        out_specs=pl.BlockSpec(memory_space=pl.ANY),
        scratch_shapes=[pltpu.SemaphoreType.DMA, pltpu.SemaphoreType.DMA],
    )
    return pl.pallas_call(
        functools.partial(_dispatch_kernel, moe_tm=moe_tm),
        out_shape=jax.ShapeDtypeStruct((rows_total, W), fp.dtype),
        grid_spec=grid_spec,
        compiler_params=_params(("arbitrary",)),
        name="dispatch",
    )(pend, padded, n_used, dest3, fp, zeros)


def _expert_rows_pipeline(rs_ref, pr_ref, x_hbm, o_hbm, xbuf, obuf, isem, osem, *, col0, compute):
    big, mid, small = GROUP_SIZES
    j = pl.program_id(0)
    e = pl.program_id(1)
    nj = pl.num_programs(0)
    E = pl.num_programs(1)
    tn = obuf.shape[2]

    def items_of(expert):
        pr = pr_ref[expert]
        n_full = pr // big
        rem = pr - n_full * big
        has_mid = rem >= mid
        has_small = rem - jnp.where(has_mid, mid, 0) >= small
        return n_full, has_mid, n_full + has_mid.astype(jnp.int32) + has_small.astype(jnp.int32)

    def item(t, n_full, has_mid):
        cls = jnp.where(t < n_full, 0, jnp.where(jnp.logical_and(t == n_full, has_mid), 1, 2))
        off = jnp.minimum(t, n_full) * big + jnp.where(t > n_full, mid, 0)
        return cls, off

    def in_copy(row0, size, slot):
        rows = pl.ds(pl.multiple_of(row0, small), size)
        return pltpu.make_async_copy(x_hbm.at[rows], xbuf.at[slot, pl.ds(0, size)], isem.at[slot])

    def out_copy(row0, size, slot):
        rows = pl.ds(pl.multiple_of(row0, small), size)
        return pltpu.make_async_copy(obuf.at[slot, pl.ds(0, size)], o_hbm.at[rows, pl.ds(col0, tn)], osem.at[slot])

    def by_size(cls, fn):
        for c, size in enumerate(GROUP_SIZES):
            @pl.when(cls == c)
            def _():
                fn(size)

    def start_first_item(expert):
        n_full, has_mid, n_items = items_of(expert)
        cls, _ = item(0, n_full, has_mid)

        @pl.when(n_items > 0)
        def _():
            by_size(cls, lambda size: in_copy(rs_ref[expert], size, 0).start())

    rs = rs_ref[e]
    n_full, has_mid, n_items = items_of(e)

    @pl.when(jnp.logical_and(j == 0, e == 0))
    def _():
        start_first_item(e)

    def body(t, carry):
        slot = t % 2
        cls, off = item(t, n_full, has_mid)
        by_size(cls, lambda size: in_copy(rs + off, size, slot).wait())

        @pl.when(t + 1 < n_items)
        def _():
            cls1, off1 = item(t + 1, n_full, has_mid)
            by_size(cls1, lambda size: in_copy(rs + off1, size, 1 - slot).start())

        @pl.when(t >= 2)
        def _():
            out_copy(rs, big, slot).wait()

        def run(size):
            compute(size, slot)
            out_copy(rs + off, size, slot).start()
        by_size(cls, run)
        return carry
    lax.fori_loop(0, n_items, body, 0)

    for back in (2, 1):
        t = n_items - back

        @pl.when(t >= 0)
        def _():
            cls, off = item(t, n_full, has_mid)
            by_size(cls, lambda size: out_copy(rs + off, size, t % 2).wait())

    @pl.when(jnp.logical_not(jnp.logical_and(j == nj - 1, e == E - 1)))
    def _():
        start_first_item((e + 1) % E)

    @pl.when(e == E - 1)
    def _():
        obuf[0, pl.ds(0, small), :] = jnp.zeros((small, tn), obuf.dtype)
        first = (rs + pr_ref[e]) // small
        last = o_hbm.shape[0] // small

        def start_fill(b, c):
            out_copy(b * small, small, 0).start()
            return c

        def wait_fill(b, c):
            out_copy(b * small, small, 0).wait()
            return c
        lax.fori_loop(first, last, start_fill, 0)
        lax.fori_loop(first, last, wait_fill, 0)


def _gate_up_kernel(rs_ref, pr_ref, xs_hbm, wg_ref, wu_ref, bg_ref, bu_ref, h_hbm, xbuf, obuf, isem, osem):
    tn = obuf.shape[2]

    def compute(size, slot):
        lo, hi = _unpack_bf16_pair(xbuf[slot, pl.ds(0, size), :])
        x = jnp.concatenate([lo, hi], axis=-1)
        gate = jnp.dot(x, wg_ref[...].astype(BF16), preferred_element_type=F32) + bg_ref[...]
        up = jnp.dot(x, wu_ref[...].astype(BF16), preferred_element_type=F32) + bu_ref[...]
        gate = jnp.minimum(gate, SWIGLU_LIMIT)
        up = jnp.clip(up, -SWIGLU_LIMIT, SWIGLU_LIMIT)
        obuf[slot, pl.ds(0, size), :] = ((up + 1.0) * gate * jax.nn.sigmoid(SWIGLU_ALPHA * gate)).astype(obuf.dtype)

    _expert_rows_pipeline(rs_ref, pr_ref, xs_hbm, h_hbm, xbuf, obuf, isem, osem,
                          col0=pl.multiple_of(pl.program_id(0) * tn, tn), compute=compute)


def _gate_up(xs, row_start, row_count, w_gate_up, b_gate_up, *, rows, tn):
    E, D, F2 = w_gate_up.shape
    FF = F2 // 2
    nj = FF // tn
    W = xs.shape[1]
    big = GROUP_SIZES[0]
    grid_spec = pltpu.PrefetchScalarGridSpec(
        num_scalar_prefetch=2,
        grid=(nj, E),
        in_specs=[
            pl.BlockSpec(memory_space=pl.ANY),
            pl.BlockSpec((None, D, tn), lambda j, e, rs, pr: (e, 0, j)),
            pl.BlockSpec((None, D, tn), lambda j, e, rs, pr: (e, 0, nj + j)),
            pl.BlockSpec((None, 1, tn), lambda j, e, rs, pr: (e, 0, j)),
            pl.BlockSpec((None, 1, tn), lambda j, e, rs, pr: (e, 0, nj + j)),
        ],
        out_specs=pl.BlockSpec(memory_space=pl.ANY),
        scratch_shapes=[pltpu.VMEM((2, big, W), xs.dtype), pltpu.VMEM((2, big, tn), BF16),
                        pltpu.SemaphoreType.DMA((2,)), pltpu.SemaphoreType.DMA((2,))],
    )
    b3 = b_gate_up.reshape(E, 1, F2)
    return pl.pallas_call(
        _gate_up_kernel,
        out_shape=jax.ShapeDtypeStruct((rows, FF), BF16),
        grid_spec=grid_spec,
        compiler_params=_params(("arbitrary", "arbitrary")),
        name="gate_up",
    )(row_start, row_count, xs, w_gate_up, w_gate_up, b3, b3)


def _down_kernel(rs_ref, pr_ref, h_hbm, w_ref, b_ref, y_hbm, xbuf, obuf, isem, osem):
    tn = obuf.shape[2]

    def compute(size, slot):
        hrows = xbuf[slot, pl.ds(0, size), :]
        obuf[slot, pl.ds(0, size), :] = jnp.dot(hrows, w_ref[...].astype(BF16), preferred_element_type=F32) + b_ref[...]

    _expert_rows_pipeline(rs_ref, pr_ref, h_hbm, y_hbm, xbuf, obuf, isem, osem,
                          col0=pl.multiple_of(pl.program_id(0) * tn, tn), compute=compute)


def _down(h, row_start, row_count, w_down, b_down, *, tn):
    E, FF, D = w_down.shape
    rows = h.shape[0]
    big = GROUP_SIZES[0]
    grid_spec = pltpu.PrefetchScalarGridSpec(
        num_scalar_prefetch=2,
        grid=(D // tn, E),
        in_specs=[
            pl.BlockSpec(memory_space=pl.ANY),
            pl.BlockSpec((None, FF, tn), lambda j, e, rs, pr: (e, 0, j)),
            pl.BlockSpec((None, 1, tn), lambda j, e, rs, pr: (e, 0, j)),
        ],
        out_specs=pl.BlockSpec(memory_space=pl.ANY),
        scratch_shapes=[pltpu.VMEM((2, big, FF), h.dtype), pltpu.VMEM((2, big, tn), F32),
                        pltpu.SemaphoreType.DMA((2,)), pltpu.SemaphoreType.DMA((2,))],
    )
    return pl.pallas_call(
        _down_kernel,
        out_shape=jax.ShapeDtypeStruct((rows, D), F32),
        grid_spec=grid_spec,
        compiler_params=_params(("arbitrary", "arbitrary")),
        name="down",
    )(row_start, row_count, h, w_down, b_down.reshape(E, 1, D))


def _combine_kernel(dest_ref, dnext_ref, hs_ref, gate_ref, nw_ref, y_hbm, o_ref, ybuf, sem, *, sub):
    s = pl.program_id(0)
    n = pl.num_programs(0)
    tm = hs_ref.shape[0]
    slot = s % 2

    def start_gather(d_ref, to_slot):
        def body(r, c):
            for k in range(TOP_K):
                pltpu.make_async_copy(y_hbm.at[pl.ds(d_ref[k, r], 1)], ybuf.at[to_slot, k, pl.ds(r, 1)],
                                      sem.at[to_slot]).start()
            return c
        lax.fori_loop(0, tm, body, 0)

    @pl.when(s == 0)
    def _():
        start_gather(dest_ref, 0)

    @pl.when(s + 1 < n)
    def _():
        start_gather(dnext_ref, 1 - slot)

    for k in range(TOP_K):
        pltpu.make_async_copy(y_hbm.at[pl.ds(0, tm)], ybuf.at[slot, k], sem.at[slot]).wait()

    def body(r, c):
        rows = pl.ds(pl.multiple_of(r * sub, sub), sub)
        g = gate_ref[rows, :]
        acc = hs_ref[rows, :]
        for k in range(TOP_K):
            acc = acc + ybuf[slot, k, rows, :] * g[:, k:k + 1]
        ms = jnp.mean(acc * acc, axis=-1, keepdims=True)
        o_ref[rows, :] = acc * lax.rsqrt(ms + RMS_EPS) * nw_ref[...]
        return c
    lax.fori_loop(0, tm // sub, body, 0)


def _combine(hs1, gates_t, dest3, y, final_norm_w, *, B, LP, seq_len, tile, sub):
    T, D = hs1.shape
    per_seq = LP // tile
    n_seq_tiles = seq_len // tile
    lead = (LP - seq_len) // tile
    n_steps = B * n_seq_tiles

    def tok(s):
        return (s // n_seq_tiles) * per_seq + lead + s % n_seq_tiles

    return pl.pallas_call(
        functools.partial(_combine_kernel, sub=sub),
        out_shape=jax.ShapeDtypeStruct((B, seq_len, D), F32),
        grid=(n_steps,),
        in_specs=[
            pl.BlockSpec((None, TOP_K, tile), lambda s: (tok(s), 0, 0), memory_space=pltpu.SMEM),
            pl.BlockSpec((None, TOP_K, tile), lambda s: (tok(jnp.minimum(s + 1, n_steps - 1)), 0, 0),
                         memory_space=pltpu.SMEM),
            pl.BlockSpec((tile, D), lambda s: (tok(s), 0)),
            pl.BlockSpec((tile, TOP_K), lambda s: (tok(s), 0)),
            pl.BlockSpec((1, D), lambda s: (0, 0)),
            pl.BlockSpec(memory_space=pl.ANY),
        ],
        out_specs=pl.BlockSpec((None, tile, D), lambda s: (s // n_seq_tiles, s % n_seq_tiles, 0)),
        scratch_shapes=[pltpu.VMEM((2, TOP_K, tile, D), F32), pltpu.SemaphoreType.DMA((2,))],
        compiler_params=_params(("arbitrary",)),
        name="combine",
    )(dest3, dest3, hs1, gates_t, final_norm_w.reshape(1, D), y)


def _forward(x, meta_tokens, norm_mix_w, w_in, ret_log_rate_fwd, ret_log_rate_bwd, pool_w, pool_scale,
             w_out, norm_ffn_w, router_w, router_b, w_gate_up, b_gate_up, w_down, b_down, final_norm_w, tiles):
    B, S, D = x.shape
    depth = norm_mix_w.shape[0]
    assert depth == 1, "single-layer block"
    H = ret_log_rate_fwd.shape[-1]
    n_groups, cg = pool_w.shape[1], pool_w.shape[2]
    PW = n_groups * cg
    RW = w_out.shape[1] - PW
    d = RW // H
    E = router_w.shape[-1]
    assert w_in.shape[-1] == 4 * RW + PW and S % CHUNK == 0
    LP = PAD + N_META + S
    T = B * LP
    t = tiles

    hs0 = jnp.concatenate(
        [jnp.zeros((B, PAD, D), x.dtype), jnp.broadcast_to(meta_tokens.astype(x.dtype)[None], (B, N_META, D)), x],
        axis=1).reshape(T, D)

    proj = _in_proj(hs0, norm_mix_w[0], w_in[0].astype(BF16), tm=t["in_tm"], tn=t["in_tn"], sub=t["norm_rows"])
    pos = (jnp.arange(LP, dtype=jnp.int32) - PAD).astype(F32)
    inv_freq = ROPE_BASE ** (-jnp.arange(d // 2, dtype=F32) / (d // 2))
    ang = pos[:, None] * inv_freq[None, :]
    log_gamma = jnp.stack([-jnp.exp(ret_log_rate_fwd[0].astype(F32)), -jnp.exp(ret_log_rate_bwd[0].astype(F32))])
    ret = _retention(proj, log_gamma, jnp.cos(ang), jnp.sin(ang), B=B, LP=LP, H=H, d=d, GC=t["ret_group_chunks"],
                     HB=t["ret_heads_per_step"])
    pooled = _pool(proj, pool_w[0].astype(BF16), pool_scale[0].astype(F32), LP=LP, seq_len=N_META + S,
                   u_col0=4 * RW, tm=t["pool_tm"])
    hs1 = _out_proj(ret, pooled, w_out[0].astype(BF16), hs0, tm=t["out_tm"], tn=t["out_tn"])

    row_id = jnp.arange(T, dtype=jnp.int32)
    is_token = (row_id % LP) >= PAD
    fp, eid, gates, rank, cnt = _router(hs1, norm_ffn_w[0], router_w[0], router_b[0], is_token,
                                        tm=t["router_tm"], sub=t["norm_rows"])
    align = GROUP_SIZES[-1]
    tile = t["token_tile"]
    counts = cnt[:, 0].astype(jnp.int32)
    padded = (counts + align - 1) // align * align
    pend = jnp.cumsum(padded).astype(jnp.int32)
    pstart = pend - padded
    n_assign = B * (N_META + S) * TOP_K
    rows = -(-(n_assign + E * (align - 1)) // align) * align
    n_used = pend[-1:] // align
    spare = rows + jnp.arange(TOP_K, dtype=jnp.int32)[:, None] * tile + (row_id % tile)[None, :]
    expert_ids = jnp.arange(E, dtype=jnp.int32)[:, None, None]
    first_row = jnp.sum(jnp.where(eid[None] == expert_ids, pstart[:, None, None], 0), axis=0)
    dest = jnp.where(is_token[None, :], first_row + rank, spare)
    dest3 = dest.reshape(TOP_K, T // tile, tile).transpose(1, 0, 2)

    spare_rows = -(-(TOP_K * tile) // align) * align
    xs = _dispatch(fp, dest3, pend, padded, n_used, rows_total=rows + spare_rows, moe_tm=align, tile=tile)
    hmid = _gate_up(xs, pstart, padded, w_gate_up[0], b_gate_up[0], rows=rows, tn=t["gu_tn"])
    y = _down(hmid, pstart, padded, w_down[0], b_down[0], tn=t["down_tn"])
    return _combine(hs1, gates.T, dest3, y, final_norm_w, B=B, LP=LP, seq_len=S, tile=tile, sub=t["combine_rows"])


def kernel(x, meta_tokens, norm_mix_w, w_in, ret_log_rate_fwd, ret_log_rate_bwd, pool_w, pool_scale, w_out,
           norm_ffn_w, router_w, router_b, w_gate_up, b_gate_up, w_down, b_down, final_norm_w):
    return _forward(x, meta_tokens, norm_mix_w, w_in, ret_log_rate_fwd, ret_log_rate_bwd, pool_w, pool_scale,
                    w_out, norm_ffn_w, router_w, router_b, w_gate_up, b_gate_up, w_down, b_down, final_norm_w,
                    DEFAULT_TILES)
```

```python
import functools

import jax
import jax.numpy as jnp
from jax import lax
from jax.experimental import pallas as pl
from jax.experimental.pallas import tpu as pltpu

N_META = 16
CHUNK = 128
PAD = CHUNK - N_META
POOL_WINDOWS = (2, 4, 8, 16)
POOL_HALO = 64
TOP_K = 4
GROUP_SIZES = (1024, 512, 256, 128)
SWIGLU_LIMIT = 7.0
SWIGLU_ALPHA = 1.702
RMS_EPS = 1e-5
ROPE_BASE = 10000.0

V7X_VMEM_BYTES = 64 * 1024 * 1024
VMEM_LIMIT = V7X_VMEM_BYTES - 2 * 1024 * 1024

F32 = jnp.float32
BF16 = jnp.bfloat16

DEFAULT_TILES = dict(
    in_tm=640, in_tn=1024, norm_rows=32,
    ret_group_chunks=13, ret_heads_per_step=2,
    pool_tm=640,
    out_tm=1280, out_tn=512,
    router_tm=256,
    gu_tn=512, down_tn=1024,
    token_tile=128, combine_rows=32,
)


def _params(semantics):
    return pltpu.CompilerParams(dimension_semantics=semantics, vmem_limit_bytes=VMEM_LIMIT)


def _in_proj_kernel(x_ref, nw_ref, w_ref, o_ref, a_scr, *, sub):
    @pl.when(pl.program_id(1) == 0)
    def _():
        def body(r, c):
            rows = pl.ds(pl.multiple_of(r * sub, sub), sub)
            x = x_ref[rows, :]
            ms = jnp.mean(x * x, axis=-1, keepdims=True)
            a_scr[rows, :] = (x * lax.rsqrt(ms + RMS_EPS) * nw_ref[...]).astype(a_scr.dtype)
            return c
        lax.fori_loop(0, x_ref.shape[0] // sub, body, 0)

    o_ref[...] = jnp.dot(a_scr[...], w_ref[...], preferred_element_type=F32).astype(o_ref.dtype)


def _in_proj(hs, norm_w, w_in_bf16, *, tm, tn, sub):
    T, D = hs.shape
    N = w_in_bf16.shape[1]
    return pl.pallas_call(
        functools.partial(_in_proj_kernel, sub=sub),
        out_shape=jax.ShapeDtypeStruct((T, N), BF16),
        grid=(T // tm, N // tn),
        in_specs=[
            pl.BlockSpec((tm, D), lambda i, j: (i, 0)),
            pl.BlockSpec((1, D), lambda i, j: (0, 0)),
            pl.BlockSpec((D, tn), lambda i, j: (0, j)),
        ],
        out_specs=pl.BlockSpec((tm, tn), lambda i, j: (i, j)),
        scratch_shapes=[pltpu.VMEM((tm, D), BF16)],
        compiler_params=_params(("arbitrary", "arbitrary")),
        name="in_proj",
    )(hs, norm_w.reshape(1, D), w_in_bf16)


def _retention_kernel(lg_ref, q_ref, k_ref, v_ref, g_ref, cos_ref, sin_ref, o_ref,
                      acc_scr, sf_scr, sb_scr, dmat_scr, dvec_scr, *, G, GC):
    hb = pl.program_id(1)
    s = pl.program_id(2)
    phase = s // G
    w = s % G
    gi = w + phase * (G - 1 - 2 * w)
    C = CHUNK
    d = cos_ref.shape[1] * 2
    half = d // 2
    HB = q_ref.shape[1] // d
    GR = GC * C
    k_scale = d ** -0.5
    nt = (((1,), (1,)), ((), ()))
    tn = (((0,), (0,)), ((), ()))

    @pl.when(s == 0)
    def _():
        t = lax.broadcasted_iota(jnp.int32, (C, C), 0).astype(F32)
        u = lax.broadcasted_iota(jnp.int32, (C, C), 1).astype(F32)
        diff = t - u
        idx = lax.broadcasted_iota(jnp.int32, (C, d), 0).astype(F32)
        for hh in range(HB):
            lgf = lg_ref[0, hb * HB + hh]
            lgb = lg_ref[1, hb * HB + hh]
            dmat_scr[hh] = jnp.where(diff >= 0.0, jnp.exp(lgf * jnp.maximum(diff, 0.0)),
                                     jnp.exp(lgb * jnp.maximum(-diff, 0.0)))
            dvec_scr[hh, 0] = jnp.exp(lgf * (idx + 1.0))
            dvec_scr[hh, 1] = jnp.exp(lgf * (C - 1.0 - idx))
            dvec_scr[hh, 2] = jnp.exp(lgb * (C - idx))
            dvec_scr[hh, 3] = jnp.exp(lgb * idx)
            dvec_scr[hh, 4] = jnp.exp(jnp.full((C, d), lgf * C, F32))
            dvec_scr[hh, 5] = jnp.exp(jnp.full((C, d), lgb * C, F32))
        sf_scr[...] = jnp.zeros_like(sf_scr)

    @pl.when(s == G)
    def _():
        sb_scr[...] = jnp.zeros_like(sb_scr)

    def rotary(ref, rows, hh):
        x = ref[rows, hh * d:(hh + 1) * d].astype(F32)
        x1 = x[:, :half]
        x2 = x[:, half:]
        cs = cos_ref[rows, :]
        sn = sin_ref[rows, :]
        return jnp.concatenate([x1 * cs - x2 * sn, x2 * cs + x1 * sn], axis=-1)

    @pl.when(phase == 0)
    def _():
        def body(c, carry):
            rows = pl.ds(pl.multiple_of(c * C, C), C)
            arows = pl.ds(pl.multiple_of(gi * GR + c * C, C), C)
            for hh in range(HB):
                cols = slice(hh * d, (hh + 1) * d)
                qc = rotary(q_ref, rows, hh).astype(BF16)
                kf = rotary(k_ref, rows, hh) * k_scale
                vc = v_ref[rows, cols]
                sc = lax.dot_general(qc, kf.astype(BF16), nt, preferred_element_type=F32)
                p = (sc * dmat_scr[hh]).astype(BF16)
                inner = jnp.dot(p, vc, preferred_element_type=F32)
                sf = sf_scr[hh]
                cross = jnp.dot(qc, sf.astype(BF16), preferred_element_type=F32) * dvec_scr[hh, 0]
                acc_scr[arows, cols] = inner + cross
                kd = (kf * dvec_scr[hh, 1]).astype(BF16)
                sf_scr[hh] = sf * dvec_scr[hh, 4, 0:1, :] + lax.dot_general(kd, vc, tn, preferred_element_type=F32)
            return carry
        lax.fori_loop(0, GC, body, 0)

    @pl.when(phase == 1)
    def _():
        def body(cc, carry):
            c = GC - 1 - cc
            rows = pl.ds(pl.multiple_of(c * C, C), C)
            arows = pl.ds(pl.multiple_of(gi * GR + c * C, C), C)
            for hh in range(HB):
                cols = slice(hh * d, (hh + 1) * d)
                qc = rotary(q_ref, rows, hh).astype(BF16)
                kf = rotary(k_ref, rows, hh) * k_scale
                vc = v_ref[rows, cols]
                sb = sb_scr[hh]
                r = acc_scr[arows, cols] + jnp.dot(qc, sb.astype(BF16), preferred_element_type=F32) * dvec_scr[hh, 2]
                kd = (kf * dvec_scr[hh, 3]).astype(BF16)
                sb_scr[hh] = sb * dvec_scr[hh, 5, 0:1, :] + lax.dot_general(kd, vc, tn, preferred_element_type=F32)
                rn = r * lax.rsqrt(jnp.mean(r * r, axis=-1, keepdims=True) + RMS_EPS)
                gg = g_ref[rows, cols].astype(F32)
                o_ref[rows, cols] = (gg * jax.nn.sigmoid(gg) * rn).astype(o_ref.dtype)
            return carry
        lax.fori_loop(0, GC, body, 0)


def _retention(proj, log_gamma, cos_tab, sin_tab, *, B, LP, H, d, GC, HB):
    T = proj.shape[0]
    GR = GC * CHUNK
    G = LP // GR
    assert H % HB == 0
    H = H // HB
    bw = HB * d

    def grp(s):
        phase = s // G
        w = s % G
        return phase, w + phase * (G - 1 - 2 * w)

    def qkv_map(col0):
        def m(b, h, s):
            _, gi = grp(s)
            return (b * G + gi, col0 + h)
        return m

    def late_map(col0):
        def m(b, h, s):
            phase, gi = grp(s)
            return (b * G + jnp.where(phase == 0, G - 1, gi), col0 + h)
        return m

    def tab_map(b, h, s):
        _, gi = grp(s)
        return (gi, 0)

    return pl.pallas_call(
        functools.partial(_retention_kernel, G=G, GC=GC),
        out_shape=jax.ShapeDtypeStruct((T, H * bw), BF16),
        grid=(B, H, 2 * G),
        in_specs=[
            pl.BlockSpec(memory_space=pltpu.SMEM),
            pl.BlockSpec((GR, bw), qkv_map(0)),
            pl.BlockSpec((GR, bw), qkv_map(H)),
            pl.BlockSpec((GR, bw), qkv_map(2 * H)),
            pl.BlockSpec((GR, bw), late_map(3 * H)),
            pl.BlockSpec((GR, d // 2), tab_map),
            pl.BlockSpec((GR, d // 2), tab_map),
        ],
        out_specs=pl.BlockSpec((GR, bw), late_map(0)),
        scratch_shapes=[
            pltpu.VMEM((LP, bw), F32),
            pltpu.VMEM((HB, d, d), F32),
            pltpu.VMEM((HB, d, d), F32),
            pltpu.VMEM((HB, CHUNK, CHUNK), F32),
            pltpu.VMEM((HB, 6, CHUNK, d), F32),
        ],
        compiler_params=_params(("arbitrary", "arbitrary", "arbitrary")),
        name="retention",
    )(log_gamma, proj, proj, proj, proj, cos_tab, sin_tab)


def _pool_kernel(prev_ref, cur_ref, next_ref, pw_ref, ps_ref, o_ref, *, tiles_per_seq, seq_len):
    i = pl.program_id(0)
    tm = cur_ref.shape[0]
    ext = tm + 2 * POOL_HALO
    n_groups = pw_ref.shape[0]
    cg = pw_ref.shape[1]
    row0 = (i % tiles_per_seq) * tm - PAD
    pos_t = row0 + lax.broadcasted_iota(jnp.int32, (tm, ext), 0)
    pos_s = row0 - POOL_HALO + lax.broadcasted_iota(jnp.int32, (tm, ext), 1)
    pos_c = row0 + lax.broadcasted_iota(jnp.int32, (tm, 1), 0)
    for gi in range(n_groups):
        win = POOL_WINDOWS[gi]
        cols = slice(gi * cg, (gi + 1) * cg)
        lo = jnp.clip(pos_t - win // 2, 0, seq_len)
        hi = jnp.clip(pos_t + (win - win // 2), 0, seq_len)
        band = jnp.where(pos_s >= lo, jnp.where(pos_s < hi, 1.0, 0.0), 0.0).astype(BF16)
        cnt = (jnp.clip(pos_c + (win - win // 2), 0, seq_len) - jnp.clip(pos_c - win // 2, 0, seq_len)).astype(F32)
        inv = jnp.where(pos_c >= 0, 1.0 / jnp.maximum(cnt, 1.0), 0.0)
        u_cur = cur_ref[:, cols]
        u_ext = jnp.concatenate([prev_ref[:, cols], u_cur, next_ref[:, cols]], axis=0)
        wsum = jnp.dot(band, u_ext, preferred_element_type=F32)
        pooled = wsum * inv - u_cur.astype(F32)
        y = jnp.dot(pooled.astype(BF16), pw_ref[gi], preferred_element_type=F32)
        o_ref[:, cols] = (y * ps_ref[:, cols]).astype(o_ref.dtype)


def _pool(proj, pool_w_bf16, pool_scale, *, LP, seq_len, u_col0, tm):
    T = proj.shape[0]
    n_groups, cg, _ = pool_w_bf16.shape
    PW = n_groups * cg
    col_blk = u_col0 // PW
    hb = tm // POOL_HALO
    n_halo = T // POOL_HALO
    return pl.pallas_call(
        functools.partial(_pool_kernel, tiles_per_seq=LP // tm, seq_len=seq_len),
        out_shape=jax.ShapeDtypeStruct((T, PW), BF16),
        grid=(T // tm,),
        in_specs=[
            pl.BlockSpec((POOL_HALO, PW), lambda i: (jnp.maximum(i * hb - 1, 0), col_blk)),
            pl.BlockSpec((tm, PW), lambda i: (i, col_blk)),
            pl.BlockSpec((POOL_HALO, PW), lambda i: (jnp.minimum((i + 1) * hb, n_halo - 1), col_blk)),
            pl.BlockSpec((n_groups, cg, cg), lambda i: (0, 0, 0)),
            pl.BlockSpec((1, PW), lambda i: (0, 0)),
        ],
        out_specs=pl.BlockSpec((tm, PW), lambda i: (i, 0)),
        compiler_params=_params(("arbitrary",)),
        name="pool",
    )(proj, proj, proj, pool_w_bf16, pool_scale.reshape(1, PW))


def _out_proj_kernel(ret_ref, pool_ref, wr_ref, wp_ref, hs_ref, o_ref):
    y = jnp.dot(ret_ref[...], wr_ref[...], preferred_element_type=F32)
    y = y + jnp.dot(pool_ref[...], wp_ref[...], preferred_element_type=F32)
    o_ref[...] = hs_ref[...] + y


def _out_proj(ret, pool, w_out_bf16, hs, *, tm, tn):
    T, RW = ret.shape
    PW = pool.shape[1]
    D = hs.shape[1]
    assert RW == PW, "the two mixer halves share one row-block split of w_out"
    return pl.pallas_call(
        _out_proj_kernel,
        out_shape=jax.ShapeDtypeStruct((T, D), F32),
        grid=(T // tm, D // tn),
        in_specs=[
            pl.BlockSpec((tm, RW), lambda i, j: (i, 0)),
            pl.BlockSpec((tm, PW), lambda i, j: (i, 0)),
            pl.BlockSpec((RW, tn), lambda i, j: (0, j)),
            pl.BlockSpec((PW, tn), lambda i, j: (1, j)),
            pl.BlockSpec((tm, tn), lambda i, j: (i, j)),
        ],
        out_specs=pl.BlockSpec((tm, tn), lambda i, j: (i, j)),
        compiler_params=_params(("arbitrary", "arbitrary")),
        name="out_proj",
    )(ret, pool, w_out_bf16, w_out_bf16, hs)


def _pack_bf16_pair(lo, hi):
    lo_bits = lax.bitcast_convert_type(lo.astype(BF16).astype(F32), jnp.uint32)
    hi_bits = lax.bitcast_convert_type(hi.astype(BF16).astype(F32), jnp.uint32)
    return (hi_bits & jnp.uint32(0xFFFF0000)) | (lo_bits >> jnp.uint32(16))


def _unpack_bf16_pair(word):
    lo = lax.bitcast_convert_type(word << jnp.uint32(16), F32).astype(BF16)
    hi = lax.bitcast_convert_type(word & jnp.uint32(0xFFFF0000), F32).astype(BF16)
    return lo, hi


def _router_kernel(hs_ref, nw_ref, rw_ref, rwhi_ref, rb_ref, valid_ref,
                   fp_ref, eid_ref, gate_ref, rank_ref, cnt_ref,
                   fhi_scr, flo_scr, carry_scr, *, sub):
    i = pl.program_id(0)
    tm, D = hs_ref.shape
    E = rb_ref.shape[0]
    half = D // 2

    @pl.when(i == 0)
    def _():
        carry_scr[...] = jnp.zeros_like(carry_scr)

    def body(r, c):
        rows = pl.ds(pl.multiple_of(r * sub, sub), sub)
        x = hs_ref[rows, :]
        ms = jnp.mean(x * x, axis=-1, keepdims=True)
        f = x * lax.rsqrt(ms + RMS_EPS) * nw_ref[...]
        f_hi = f.astype(BF16)
        fhi_scr[rows, :] = f_hi
        flo_scr[rows, :] = (f - f_hi.astype(F32)).astype(BF16)
        fp_ref[rows, :] = _pack_bf16_pair(f[:, :half], f[:, half:])
        return c
    lax.fori_loop(0, tm // sub, body, 0)

    nt = (((1,), (1,)), ((), ()))
    part = lax.dot_general(rw_ref[...], fhi_scr[...], nt, preferred_element_type=F32)
    logits = part[:E] + part[E:] + lax.dot_general(rwhi_ref[...], flo_scr[...], nt, preferred_element_type=F32)
    logits = logits + rb_ref[...]

    eio = lax.broadcasted_iota(jnp.int32, (E, tm), 0).astype(F32)
    valid = valid_ref[...]

    vals, onehots = [], []
    l = logits
    for k in range(TOP_K):
        m = jnp.max(l, axis=0, keepdims=True)
        idx = jnp.min(jnp.where(l == m, eio, float(E)), axis=0, keepdims=True)
        sel = eio == idx
        vals.append(m)
        onehots.append(jnp.where(sel, valid, 0.0))
        eid_ref[pl.ds(k, 1), :] = idx.astype(jnp.int32)
        l = jnp.where(sel, -jnp.inf, l)

    exps = [jnp.exp(v - vals[0]) for v in vals]
    denom = exps[0] + exps[1] + exps[2] + exps[3]
    for k in range(TOP_K):
        gate_ref[pl.ds(k, 1), :] = exps[k] / denom

    oh = onehots[0] + onehots[1] + onehots[2] + onehots[3]
    tt = lax.broadcasted_iota(jnp.int32, (tm, tm), 0)
    uu = lax.broadcasted_iota(jnp.int32, (tm, tm), 1)
    upper = jnp.where(tt < uu, 1.0, 0.0).astype(BF16)
    carry = carry_scr[...]
    before = jnp.dot(oh.astype(BF16), upper, preferred_element_type=F32) + carry[:, 0:1]
    for k in range(TOP_K):
        rank_ref[pl.ds(k, 1), :] = jnp.sum(onehots[k] * before, axis=0, keepdims=True).astype(jnp.int32)
    carry = carry + jnp.sum(oh, axis=1, keepdims=True)
    carry_scr[...] = carry
    cnt_ref[...] = carry


def _router(hs1, norm_w, router_w, router_b, is_token, *, tm, sub):
    T, D = hs1.shape
    E = router_w.shape[1]
    rw_t = router_w.T.astype(F32)
    rw_hi = rw_t.astype(BF16)
    rw_lo = (rw_t - rw_hi.astype(F32)).astype(BF16)
    rw_stack = jnp.concatenate([rw_hi, rw_lo], axis=0)
    n = T // tm
    return pl.pallas_call(
        functools.partial(_router_kernel, sub=sub),
        out_shape=(
            jax.ShapeDtypeStruct((T, D // 2), jnp.uint32),
            jax.ShapeDtypeStruct((TOP_K, T), jnp.int32),
            jax.ShapeDtypeStruct((TOP_K, T), F32),
            jax.ShapeDtypeStruct((TOP_K, T), jnp.int32),
            jax.ShapeDtypeStruct((E, CHUNK), F32),
        ),
        grid=(n,),
        in_specs=[
            pl.BlockSpec((tm, D), lambda i: (i, 0)),
            pl.BlockSpec((1, D), lambda i: (0, 0)),
            pl.BlockSpec((2 * E, D), lambda i: (0, 0)),
            pl.BlockSpec((E, D), lambda i: (0, 0)),
            pl.BlockSpec((E, 1), lambda i: (0, 0)),
            pl.BlockSpec((1, tm), lambda i: (0, i)),
        ],
        out_specs=(
            pl.BlockSpec((tm, D // 2), lambda i: (i, 0)),
            pl.BlockSpec((TOP_K, tm), lambda i: (0, i)),
            pl.BlockSpec((TOP_K, tm), lambda i: (0, i)),
            pl.BlockSpec((TOP_K, tm), lambda i: (0, i)),
            pl.BlockSpec((E, CHUNK), lambda i: (0, 0)),
        ),
        scratch_shapes=[pltpu.VMEM((tm, D), BF16), pltpu.VMEM((tm, D), BF16), pltpu.VMEM((E, CHUNK), F32)],
        compiler_params=_params(("arbitrary",)),
        name="router",
    )(hs1, norm_w.reshape(1, D), rw_stack, rw_hi, router_b.reshape(E, 1).astype(F32),
      is_token.astype(F32).reshape(1, T))


def _dispatch_kernel(pend_ref, padded_ref, nu_ref, dest_ref, fp_ref, zeros_ref, xs_hbm, zsem, sem, *, moe_tm):
    i = pl.program_id(0)
    tm = fp_ref.shape[0]
    E = pend_ref.shape[0]
    n_blocks = xs_hbm.shape[0] // moe_tm

    def zero_rows(start):
        return pltpu.make_async_copy(zeros_ref, xs_hbm.at[pl.ds(pl.multiple_of(start, moe_tm), moe_tm)], zsem)

    @pl.when(i == 0)
    def _():
        for e in range(E):
            @pl.when(padded_ref[e] > 0)
            def _():
                zero_rows(pend_ref[e] - moe_tm).start()

        def start_tail(b, c):
            zero_rows(b * moe_tm).start()
            return c
        lax.fori_loop(nu_ref[0], n_blocks, start_tail, 0)
        for e in range(E):
            @pl.when(padded_ref[e] > 0)
            def _():
                zero_rows(pend_ref[e] - moe_tm).wait()

        def wait_tail(b, c):
            zero_rows(b * moe_tm).wait()
            return c
        lax.fori_loop(nu_ref[0], n_blocks, wait_tail, 0)

    def body(r, c):
        for k in range(TOP_K):
            pltpu.make_async_copy(fp_ref.at[pl.ds(r, 1)], xs_hbm.at[pl.ds(dest_ref[k, r], 1)], sem).start(
                priority=k % 2)
        return c
    lax.fori_loop(0, tm, body, 0)
    for k in range(TOP_K):
        pltpu.make_async_copy(fp_ref, xs_hbm.at[pl.ds(0, tm)], sem).wait()


def _dispatch(fp, dest3, pend, padded, n_used, *, rows_total, moe_tm, tile):
    T, W = fp.shape
    assert rows_total % moe_tm == 0
    zeros = jnp.zeros((moe_tm, W), fp.dtype)
    grid_spec = pltpu.PrefetchScalarGridSpec(
        num_scalar_prefetch=3,
        grid=(T // tile,),
        in_specs=[
            pl.BlockSpec((None, TOP_K, tile), lambda i, *_: (i, 0, 0), memory_space=pltpu.SMEM),
            pl.BlockSpec((tile, W), lambda i, *_: (i, 0)),
            pl.BlockSpec((moe_tm, W), lambda i, *_: (0, 0)),
        ],
        out_specs=pl.BlockSpec(memory_space=pl.ANY),
        scratch_shapes=[pltpu.SemaphoreType.DMA, pltpu.SemaphoreType.DMA],
    )
    return pl.pallas_call(
        functools.partial(_dispatch_kernel, moe_tm=moe_tm),
        out_shape=jax.ShapeDtypeStruct((rows_total, W), fp.dtype),
        grid_spec=grid_spec,
        compiler_params=_params(("arbitrary",)),
        name="dispatch",
    )(pend, padded, n_used, dest3, fp, zeros)


def _expert_rows_pipeline(rs_ref, pr_ref, x_hbm, o_hbm, xbuf, obuf, isem, osem, *, col0, compute):
    big, small = GROUP_SIZES[0], GROUP_SIZES[-1]
    j = pl.program_id(0)
    e = pl.program_id(1)
    nj = pl.num_programs(0)
    E = pl.num_programs(1)
    tn = obuf.shape[2]

    def items_of(expert):
        pr = pr_ref[expert]
        n_full = pr // big
        rem = pr - n_full * big
        n_items = n_full
        for size in GROUP_SIZES[1:]:
            n_items = n_items + (rem // size) % 2
        return n_full, rem, n_items

    def item(t, n_full, rem):
        cls = jnp.int32(0)
        off = jnp.minimum(t, n_full) * big
        seen = jnp.int32(0)
        before = jnp.int32(0)
        for c, size in enumerate(GROUP_SIZES[1:], start=1):
            has = (rem // size) % 2
            here = jnp.logical_and(has == 1, t == n_full + seen)
            cls = jnp.where(here, c, cls)
            off = jnp.where(here, n_full * big + before, off)
            seen = seen + has
            before = before + has * size
        return cls, off

    def in_copy(row0, size, slot):
        rows = pl.ds(pl.multiple_of(row0, small), size)
        return pltpu.make_async_copy(x_hbm.at[rows], xbuf.at[slot, pl.ds(0, size)], isem.at[slot])

    def out_copy(row0, size, slot):
        rows = pl.ds(pl.multiple_of(row0, small), size)
        return pltpu.make_async_copy(obuf.at[slot, pl.ds(0, size)], o_hbm.at[rows, pl.ds(col0, tn)], osem.at[slot])

    def by_size(cls, fn):
        for c, size in enumerate(GROUP_SIZES):
            @pl.when(cls == c)
            def _():
                fn(size)

    def start_first_item(expert):
        n_full, rem, n_items = items_of(expert)
        cls, _ = item(0, n_full, rem)

        @pl.when(n_items > 0)
        def _():
            by_size(cls, lambda size: in_copy(rs_ref[expert], size, 0).start())

    rs = rs_ref[e]
    n_full, rem, n_items = items_of(e)

    def wait_out(t):
        cls, off = item(t, n_full, rem)
        by_size(cls, lambda size: out_copy(rs + off, size, t % 2).wait())

    @pl.when(jnp.logical_and(j == 0, e == 0))
    def _():
        start_first_item(e)

    def body(t, carry):
        slot = t % 2
        cls, off = item(t, n_full, rem)
        by_size(cls, lambda size: in_copy(rs + off, size, slot).wait())

        @pl.when(t + 1 < n_items)
        def _():
            cls1, off1 = item(t + 1, n_full, rem)
            by_size(cls1, lambda size: in_copy(rs + off1, size, 1 - slot).start())

        @pl.when(t >= 2)
        def _():
            wait_out(t - 2)

        def run(size):
            compute(size, slot)
            out_copy(rs + off, size, slot).start()
        by_size(cls, run)
        return carry
    lax.fori_loop(0, n_items, body, 0)

    for back in (2, 1):
        @pl.when(n_items - back >= 0)
        def _():
            wait_out(n_items - back)

    @pl.when(jnp.logical_not(jnp.logical_and(j == nj - 1, e == E - 1)))
    def _():
        start_first_item((e + 1) % E)

    @pl.when(e == E - 1)
    def _():
        obuf[0, pl.ds(0, small), :] = jnp.zeros((small, tn), obuf.dtype)
        first = (rs + pr_ref[e]) // small
        last = o_hbm.shape[0] // small

        def start_fill(b, c):
            out_copy(b * small, small, 0).start()
            return c

        def wait_fill(b, c):
            out_copy(b * small, small, 0).wait()
            return c
        lax.fori_loop(first, last, start_fill, 0)
        lax.fori_loop(first, last, wait_fill, 0)


def _gate_up_kernel(rs_ref, pr_ref, xs_hbm, wg_ref, wu_ref, bg_ref, bu_ref, h_hbm, xbuf, obuf, isem, osem):
    tn = obuf.shape[2]

    def compute(size, slot):
        lo, hi = _unpack_bf16_pair(xbuf[slot, pl.ds(0, size), :])
        x = jnp.concatenate([lo, hi], axis=-1)
        gate = jnp.dot(x, wg_ref[...].astype(BF16), preferred_element_type=F32) + bg_ref[...]
        up = jnp.dot(x, wu_ref[...].astype(BF16), preferred_element_type=F32) + bu_ref[...]
        gate = jnp.minimum(gate, SWIGLU_LIMIT)
        up = jnp.clip(up, -SWIGLU_LIMIT, SWIGLU_LIMIT)
        obuf[slot, pl.ds(0, size), :] = ((up + 1.0) * gate * jax.nn.sigmoid(SWIGLU_ALPHA * gate)).astype(obuf.dtype)

    _expert_rows_pipeline(rs_ref, pr_ref, xs_hbm, h_hbm, xbuf, obuf, isem, osem,
                          col0=pl.multiple_of(pl.program_id(0) * tn, tn), compute=compute)


def _gate_up(xs, row_start, row_count, w_gate_up, b_gate_up, *, rows, tn):
    E, D, F2 = w_gate_up.shape
    FF = F2 // 2
    nj = FF // tn
    W = xs.shape[1]
    big = GROUP_SIZES[0]
    grid_spec = pltpu.PrefetchScalarGridSpec(
        num_scalar_prefetch=2,
        grid=(nj, E),
        in_specs=[
            pl.BlockSpec(memory_space=pl.ANY),
            pl.BlockSpec((None, D, tn), lambda j, e, rs, pr: (e, 0, j)),
            pl.BlockSpec((None, D, tn), lambda j, e, rs, pr: (e, 0, nj + j)),
            pl.BlockSpec((None, 1, tn), lambda j, e, rs, pr: (e, 0, j)),
            pl.BlockSpec((None, 1, tn), lambda j, e, rs, pr: (e, 0, nj + j)),
        ],
        out_specs=pl.BlockSpec(memory_space=pl.ANY),
        scratch_shapes=[pltpu.VMEM((2, big, W), xs.dtype), pltpu.VMEM((2, big, tn), BF16),
                        pltpu.SemaphoreType.DMA((2,)), pltpu.SemaphoreType.DMA((2,))],
    )
    b3 = b_gate_up.reshape(E, 1, F2)
    return pl.pallas_call(
        _gate_up_kernel,
        out_shape=jax.ShapeDtypeStruct((rows, FF), BF16),
        grid_spec=grid_spec,
        compiler_params=_params(("arbitrary", "arbitrary")),
        name="gate_up",
    )(row_start, row_count, xs, w_gate_up, w_gate_up, b3, b3)


def _down_kernel(rs_ref, pr_ref, h_hbm, w_ref, b_ref, y_hbm, xbuf, obuf, isem, osem):
    tn = obuf.shape[2]

    def compute(size, slot):
        hrows = xbuf[slot, pl.ds(0, size), :]
        obuf[slot, pl.ds(0, size), :] = jnp.dot(hrows, w_ref[...].astype(BF16), preferred_element_type=F32) + b_ref[...]

    _expert_rows_pipeline(rs_ref, pr_ref, h_hbm, y_hbm, xbuf, obuf, isem, osem,
                          col0=pl.multiple_of(pl.program_id(0) * tn, tn), compute=compute)


def _down(h, row_start, row_count, w_down, b_down, *, tn):
    E, FF, D = w_down.shape
    rows = h.shape[0]
    big = GROUP_SIZES[0]
    grid_spec = pltpu.PrefetchScalarGridSpec(
        num_scalar_prefetch=2,
        grid=(D // tn, E),
        in_specs=[
            pl.BlockSpec(memory_space=pl.ANY),
            pl.BlockSpec((None, FF, tn), lambda j, e, rs, pr: (e, 0, j)),
            pl.BlockSpec((None, 1, tn), lambda j, e, rs, pr: (e, 0, j)),
        ],
        out_specs=pl.BlockSpec(memory_space=pl.ANY),
        scratch_shapes=[pltpu.VMEM((2, big, FF), h.dtype), pltpu.VMEM((2, big, tn), F32),
                        pltpu.SemaphoreType.DMA((2,)), pltpu.SemaphoreType.DMA((2,))],
    )
    return pl.pallas_call(
        _down_kernel,
        out_shape=jax.ShapeDtypeStruct((rows, D), F32),
        grid_spec=grid_spec,
        compiler_params=_params(("arbitrary", "arbitrary")),
        name="down",
    )(row_start, row_count, h, w_down, b_down.reshape(E, 1, D))


def _combine_kernel(dest_ref, dnext_ref, hs_ref, gate_ref, nw_ref, y_hbm, o_ref, ybuf, sem, *, sub):
    s = pl.program_id(0)
    n = pl.num_programs(0)
    tm = hs_ref.shape[0]
    slot = s % 2

    def start_gather(d_ref, to_slot):
        def body(r, c):
            for k in range(TOP_K):
                pltpu.make_async_copy(y_hbm.at[pl.ds(d_ref[k, r], 1)], ybuf.at[to_slot, k, pl.ds(r, 1)],
                                      sem.at[to_slot]).start(priority=k % 2)
            return c
        lax.fori_loop(0, tm, body, 0)

    @pl.when(s == 0)
    def _():
        start_gather(dest_ref, 0)

    @pl.when(s + 1 < n)
    def _():
        start_gather(dnext_ref, 1 - slot)

    for k in range(TOP_K):
        pltpu.make_async_copy(y_hbm.at[pl.ds(0, tm)], ybuf.at[slot, k], sem.at[slot]).wait()

    def body(r, c):
        rows = pl.ds(pl.multiple_of(r * sub, sub), sub)
        g = gate_ref[rows, :]
        acc = hs_ref[rows, :]
        for k in range(TOP_K):
            acc = acc + ybuf[slot, k, rows, :] * g[:, k:k + 1]
        ms = jnp.mean(acc * acc, axis=-1, keepdims=True)
        o_ref[rows, :] = acc * lax.rsqrt(ms + RMS_EPS) * nw_ref[...]
        return c
    lax.fori_loop(0, tm // sub, body, 0)


def _combine(hs1, gates_t, dest3, y, final_norm_w, *, B, LP, seq_len, tile, sub):
    T, D = hs1.shape
    per_seq = LP // tile
    n_seq_tiles = seq_len // tile
    lead = (LP - seq_len) // tile
    n_steps = B * n_seq_tiles

    def tok(s):
        return (s // n_seq_tiles) * per_seq + lead + s % n_seq_tiles

    return pl.pallas_call(
        functools.partial(_combine_kernel, sub=sub),
        out_shape=jax.ShapeDtypeStruct((B, seq_len, D), F32),
        grid=(n_steps,),
        in_specs=[
            pl.BlockSpec((None, TOP_K, tile), lambda s: (tok(s), 0, 0), memory_space=pltpu.SMEM),
            pl.BlockSpec((None, TOP_K, tile), lambda s: (tok(jnp.minimum(s + 1, n_steps - 1)), 0, 0),
                         memory_space=pltpu.SMEM),
            pl.BlockSpec((tile, D), lambda s: (tok(s), 0)),
            pl.BlockSpec((tile, TOP_K), lambda s: (tok(s), 0)),
            pl.BlockSpec((1, D), lambda s: (0, 0)),
            pl.BlockSpec(memory_space=pl.ANY),
        ],
        out_specs=pl.BlockSpec((None, tile, D), lambda s: (s // n_seq_tiles, s % n_seq_tiles, 0)),
        scratch_shapes=[pltpu.VMEM((2, TOP_K, tile, D), F32), pltpu.SemaphoreType.DMA((2,))],
        compiler_params=_params(("arbitrary",)),
        name="combine",
    )(dest3, dest3, hs1, gates_t, final_norm_w.reshape(1, D), y)


def _forward(x, meta_tokens, norm_mix_w, w_in, ret_log_rate_fwd, ret_log_rate_bwd, pool_w, pool_scale,
             w_out, norm_ffn_w, router_w, router_b, w_gate_up, b_gate_up, w_down, b_down, final_norm_w, tiles):
    B, S, D = x.shape
    depth = norm_mix_w.shape[0]
    assert depth == 1, "single-layer block"
    H = ret_log_rate_fwd.shape[-1]
    n_groups, cg = pool_w.shape[1], pool_w.shape[2]
    PW = n_groups * cg
    RW = w_out.shape[1] - PW
    d = RW // H
    E = router_w.shape[-1]
    assert w_in.shape[-1] == 4 * RW + PW and S % CHUNK == 0
    LP = PAD + N_META + S
    T = B * LP
    t = tiles

    hs0 = jnp.concatenate(
        [jnp.zeros((B, PAD, D), x.dtype), jnp.broadcast_to(meta_tokens.astype(x.dtype)[None], (B, N_META, D)), x],
        axis=1).reshape(T, D)

    proj = _in_proj(hs0, norm_mix_w[0], w_in[0].astype(BF16), tm=t["in_tm"], tn=t["in_tn"], sub=t["norm_rows"])
    pos = (jnp.arange(LP, dtype=jnp.int32) - PAD).astype(F32)
    inv_freq = ROPE_BASE ** (-jnp.arange(d // 2, dtype=F32) / (d // 2))
    ang = pos[:, None] * inv_freq[None, :]
    log_gamma = jnp.stack([-jnp.exp(ret_log_rate_fwd[0].astype(F32)), -jnp.exp(ret_log_rate_bwd[0].astype(F32))])
    ret = _retention(proj, log_gamma, jnp.cos(ang), jnp.sin(ang), B=B, LP=LP, H=H, d=d, GC=t["ret_group_chunks"],
                     HB=t["ret_heads_per_step"])
    pooled = _pool(proj, pool_w[0].astype(BF16), pool_scale[0].astype(F32), LP=LP, seq_len=N_META + S,
                   u_col0=4 * RW, tm=t["pool_tm"])
    hs1 = _out_proj(ret, pooled, w_out[0].astype(BF16), hs0, tm=t["out_tm"], tn=t["out_tn"])

    row_id = jnp.arange(T, dtype=jnp.int32)
    is_token = (row_id % LP) >= PAD
    fp, eid, gates, rank, cnt = _router(hs1, norm_ffn_w[0], router_w[0], router_b[0], is_token,
                                        tm=t["router_tm"], sub=t["norm_rows"])
    align = GROUP_SIZES[-1]
    tile = t["token_tile"]
    counts = cnt[:, 0].astype(jnp.int32)
    padded = (counts + align - 1) // align * align
    pend = jnp.cumsum(padded).astype(jnp.int32)
    pstart = pend - padded
    n_assign = B * (N_META + S) * TOP_K
    rows = -(-(n_assign + E * (align - 1)) // align) * align
    n_used = pend[-1:] // align
    spare = rows + jnp.arange(TOP_K, dtype=jnp.int32)[:, None] * tile + (row_id % tile)[None, :]
    expert_ids = jnp.arange(E, dtype=jnp.int32)[:, None, None]
    first_row = jnp.sum(jnp.where(eid[None] == expert_ids, pstart[:, None, None], 0), axis=0)
    dest = jnp.where(is_token[None, :], first_row + rank, spare)
    dest3 = dest.reshape(TOP_K, T // tile, tile).transpose(1, 0, 2)

    spare_rows = -(-(TOP_K * tile) // align) * align
    xs = _dispatch(fp, dest3, pend, padded, n_used, rows_total=rows + spare_rows, moe_tm=align, tile=tile)
    hmid = _gate_up(xs, pstart, padded, w_gate_up[0], b_gate_up[0], rows=rows, tn=t["gu_tn"])
    y = _down(hmid, pstart, padded, w_down[0], b_down[0], tn=t["down_tn"])
    return _combine(hs1, gates.T, dest3, y, final_norm_w, B=B, LP=LP, seq_len=S, tile=tile, sub=t["combine_rows"])


def kernel(x, meta_tokens, norm_mix_w, w_in, ret_log_rate_fwd, ret_log_rate_bwd, pool_w, pool_scale, w_out,
           norm_ffn_w, router_w, router_b, w_gate_up, b_gate_up, w_down, b_down, final_norm_w):
    return _forward(x, meta_tokens, norm_mix_w, w_in, ret_log_rate_fwd, ret_log_rate_bwd, pool_w, pool_scale,
                    w_out, norm_ffn_w, router_w, router_b, w_gate_up, b_gate_up, w_down, b_down, final_norm_w,
                    DEFAULT_TILES)
```

```python
import functools

import jax
import jax.numpy as jnp
from jax import lax
from jax.experimental import pallas as pl
from jax.experimental.pallas import tpu as pltpu

N_META = 16
CHUNK = 128
PAD = CHUNK - N_META
POOL_WINDOWS = (2, 4, 8, 16)
POOL_HALO = 64
TOP_K = 4
GROUP_SIZES = (1024, 512, 256, 128)
SWIGLU_LIMIT = 7.0
SWIGLU_ALPHA = 1.702
RMS_EPS = 1e-5
ROPE_BASE = 10000.0

V7X_VMEM_BYTES = 64 * 1024 * 1024
VMEM_LIMIT = V7X_VMEM_BYTES - 2 * 1024 * 1024

F32 = jnp.float32
BF16 = jnp.bfloat16

DEFAULT_TILES = dict(
    in_tm=640, in_tn=1024, norm_rows=32,
    ret_group_chunks=13, ret_heads_per_step=2,
    pool_tm=640,
    out_tm=1280, out_tn=512,
    router_tm=256,
    gu_tn=512, down_tn=1024,
    token_tile=128, combine_rows=32,
)


def _params(semantics):
    return pltpu.CompilerParams(dimension_semantics=semantics, vmem_limit_bytes=VMEM_LIMIT)


def _in_proj_kernel(x_ref, nw_ref, w_ref, o_ref, a_scr, *, sub):
    @pl.when(pl.program_id(1) == 0)
    def _():
        def body(r, c):
            rows = pl.ds(pl.multiple_of(r * sub, sub), sub)
            x = x_ref[rows, :]
            ms = jnp.mean(x * x, axis=-1, keepdims=True)
            a_scr[rows, :] = (x * lax.rsqrt(ms + RMS_EPS) * nw_ref[...]).astype(a_scr.dtype)
            return c
        lax.fori_loop(0, x_ref.shape[0] // sub, body, 0)

    o_ref[...] = jnp.dot(a_scr[...], w_ref[...], preferred_element_type=F32).astype(o_ref.dtype)


def _in_proj(hs, norm_w, w_in_bf16, *, tm, tn, sub):
    T, D = hs.shape
    N = w_in_bf16.shape[1]
    return pl.pallas_call(
        functools.partial(_in_proj_kernel, sub=sub),
        out_shape=jax.ShapeDtypeStruct((T, N), BF16),
        grid=(T // tm, N // tn),
        in_specs=[
            pl.BlockSpec((tm, D), lambda i, j: (i, 0)),
            pl.BlockSpec((1, D), lambda i, j: (0, 0)),
            pl.BlockSpec((D, tn), lambda i, j: (0, j)),
        ],
        out_specs=pl.BlockSpec((tm, tn), lambda i, j: (i, j)),
        scratch_shapes=[pltpu.VMEM((tm, D), BF16)],
        compiler_params=_params(("arbitrary", "arbitrary")),
        name="in_proj",
    )(hs, norm_w.reshape(1, D), w_in_bf16)


def _retention_kernel(lg_ref, q_ref, k_ref, v_ref, g_ref, cos_ref, sin_ref, o_ref,
                      acc_scr, sf_scr, sb_scr, dmat_scr, dvec_scr, *, G, GC):
    hb = pl.program_id(1)
    s = pl.program_id(2)
    phase = s // G
    w = s % G
    gi = w + phase * (G - 1 - 2 * w)
    C = CHUNK
    d = cos_ref.shape[1] * 2
    half = d // 2
    HB = q_ref.shape[1] // d
    GR = GC * C
    k_scale = d ** -0.5
    nt = (((1,), (1,)), ((), ()))
    tn = (((0,), (0,)), ((), ()))

    @pl.when(s == 0)
    def _():
        t = lax.broadcasted_iota(jnp.int32, (C, C), 0).astype(F32)
        u = lax.broadcasted_iota(jnp.int32, (C, C), 1).astype(F32)
        diff = t - u
        idx = lax.broadcasted_iota(jnp.int32, (C, d), 0).astype(F32)
        for hh in range(HB):
            lgf = lg_ref[0, hb * HB + hh]
            lgb = lg_ref[1, hb * HB + hh]
            dmat_scr[hh] = jnp.where(diff >= 0.0, jnp.exp(lgf * jnp.maximum(diff, 0.0)),
                                     jnp.exp(lgb * jnp.maximum(-diff, 0.0)))
            dvec_scr[hh, 0] = jnp.exp(lgf * (idx + 1.0))
            dvec_scr[hh, 1] = jnp.exp(lgf * (C - 1.0 - idx))
            dvec_scr[hh, 2] = jnp.exp(lgb * (C - idx))
            dvec_scr[hh, 3] = jnp.exp(lgb * idx)
            dvec_scr[hh, 4] = jnp.exp(jnp.full((C, d), lgf * C, F32))
            dvec_scr[hh, 5] = jnp.exp(jnp.full((C, d), lgb * C, F32))
        sf_scr[...] = jnp.zeros_like(sf_scr)

    @pl.when(s == G)
    def _():
        sb_scr[...] = jnp.zeros_like(sb_scr)

    def rotary(ref, rows, hh):
        x = ref[rows, hh * d:(hh + 1) * d].astype(F32)
        x1 = x[:, :half]
        x2 = x[:, half:]
        cs = cos_ref[rows, :]
        sn = sin_ref[rows, :]
        return jnp.concatenate([x1 * cs - x2 * sn, x2 * cs + x1 * sn], axis=-1)

    @pl.when(phase == 0)
    def _():
        def body(c, carry):
            rows = pl.ds(pl.multiple_of(c * C, C), C)
            arows = pl.ds(pl.multiple_of(gi * GR + c * C, C), C)
            for hh in range(HB):
                cols = slice(hh * d, (hh + 1) * d)
                qc = rotary(q_ref, rows, hh).astype(BF16)
                kf = rotary(k_ref, rows, hh) * k_scale
                vc = v_ref[rows, cols]
                sc = lax.dot_general(qc, kf.astype(BF16), nt, preferred_element_type=F32)
                p = (sc * dmat_scr[hh]).astype(BF16)
                inner = jnp.dot(p, vc, preferred_element_type=F32)
                sf = sf_scr[hh]
                cross = jnp.dot(qc, sf.astype(BF16), preferred_element_type=F32) * dvec_scr[hh, 0]
                acc_scr[arows, cols] = inner + cross
                kd = (kf * dvec_scr[hh, 1]).astype(BF16)
                sf_scr[hh] = sf * dvec_scr[hh, 4, 0:1, :] + lax.dot_general(kd, vc, tn, preferred_element_type=F32)
            return carry
        lax.fori_loop(0, GC, body, 0)

    @pl.when(phase == 1)
    def _():
        def body(cc, carry):
            c = GC - 1 - cc
            rows = pl.ds(pl.multiple_of(c * C, C), C)
            arows = pl.ds(pl.multiple_of(gi * GR + c * C, C), C)
            for hh in range(HB):
                cols = slice(hh * d, (hh + 1) * d)
                qc = rotary(q_ref, rows, hh).astype(BF16)
                kf = rotary(k_ref, rows, hh) * k_scale
                vc = v_ref[rows, cols]
                sb = sb_scr[hh]
                r = acc_scr[arows, cols] + jnp.dot(qc, sb.astype(BF16), preferred_element_type=F32) * dvec_scr[hh, 2]
                kd = (kf * dvec_scr[hh, 3]).astype(BF16)
                sb_scr[hh] = sb * dvec_scr[hh, 5, 0:1, :] + lax.dot_general(kd, vc, tn, preferred_element_type=F32)
                rn = r * lax.rsqrt(jnp.mean(r * r, axis=-1, keepdims=True) + RMS_EPS)
                gg = g_ref[rows, cols].astype(F32)
                o_ref[rows, cols] = (gg * jax.nn.sigmoid(gg) * rn).astype(o_ref.dtype)
            return carry
        lax.fori_loop(0, GC, body, 0)


def _retention(proj, log_gamma, cos_tab, sin_tab, *, B, LP, H, d, GC, HB):
    T = proj.shape[0]
    GR = GC * CHUNK
    G = LP // GR
    assert H % HB == 0
    H = H // HB
    bw = HB * d

    def grp(s):
        phase = s // G
        w = s % G
        return phase, w + phase * (G - 1 - 2 * w)

    def qkv_map(col0):
        def m(b, h, s):
            _, gi = grp(s)
            return (b * G + gi, col0 + h)
        return m

    def late_map(col0):
        def m(b, h, s):
            phase, gi = grp(s)
            return (b * G + jnp.where(phase == 0, G - 1, gi), col0 + h)
        return m

    def tab_map(b, h, s):
        _, gi = grp(s)
        return (gi, 0)

    return pl.pallas_call(
        functools.partial(_retention_kernel, G=G, GC=GC),
        out_shape=jax.ShapeDtypeStruct((T, H * bw), BF16),
        grid=(B, H, 2 * G),
        in_specs=[
            pl.BlockSpec(memory_space=pltpu.SMEM),
            pl.BlockSpec((GR, bw), qkv_map(0)),
            pl.BlockSpec((GR, bw), qkv_map(H)),
            pl.BlockSpec((GR, bw), qkv_map(2 * H)),
            pl.BlockSpec((GR, bw), late_map(3 * H)),
            pl.BlockSpec((GR, d // 2), tab_map),
            pl.BlockSpec((GR, d // 2), tab_map),
        ],
        out_specs=pl.BlockSpec((GR, bw), late_map(0)),
        scratch_shapes=[
            pltpu.VMEM((LP, bw), F32),
            pltpu.VMEM((HB, d, d), F32),
            pltpu.VMEM((HB, d, d), F32),
            pltpu.VMEM((HB, CHUNK, CHUNK), F32),
            pltpu.VMEM((HB, 6, CHUNK, d), F32),
        ],
        compiler_params=_params(("arbitrary", "arbitrary", "arbitrary")),
        name="retention",
    )(log_gamma, proj, proj, proj, proj, cos_tab, sin_tab)


def _pool_kernel(prev_ref, cur_ref, next_ref, pw_ref, ps_ref, o_ref, *, tiles_per_seq, seq_len):
    i = pl.program_id(0)
    tm = cur_ref.shape[0]
    ext = tm + 2 * POOL_HALO
    n_groups = pw_ref.shape[0]
    cg = pw_ref.shape[1]
    row0 = (i % tiles_per_seq) * tm - PAD
    pos_t = row0 + lax.broadcasted_iota(jnp.int32, (tm, ext), 0)
    pos_s = row0 - POOL_HALO + lax.broadcasted_iota(jnp.int32, (tm, ext), 1)
    pos_c = row0 + lax.broadcasted_iota(jnp.int32, (tm, 1), 0)
    for gi in range(n_groups):
        win = POOL_WINDOWS[gi]
        cols = slice(gi * cg, (gi + 1) * cg)
        lo = jnp.clip(pos_t - win // 2, 0, seq_len)
        hi = jnp.clip(pos_t + (win - win // 2), 0, seq_len)
        band = jnp.where(pos_s >= lo, jnp.where(pos_s < hi, 1.0, 0.0), 0.0).astype(BF16)
        cnt = (jnp.clip(pos_c + (win - win // 2), 0, seq_len) - jnp.clip(pos_c - win // 2, 0, seq_len)).astype(F32)
        inv = jnp.where(pos_c >= 0, 1.0 / jnp.maximum(cnt, 1.0), 0.0)
        u_cur = cur_ref[:, cols]
        u_ext = jnp.concatenate([prev_ref[:, cols], u_cur, next_ref[:, cols]], axis=0)
        wsum = jnp.dot(band, u_ext, preferred_element_type=F32)
        pooled = wsum * inv - u_cur.astype(F32)
        y = jnp.dot(pooled.astype(BF16), pw_ref[gi], preferred_element_type=F32)
        o_ref[:, cols] = (y * ps_ref[:, cols]).astype(o_ref.dtype)


def _pool(proj, pool_w_bf16, pool_scale, *, LP, seq_len, u_col0, tm):
    T = proj.shape[0]
    n_groups, cg, _ = pool_w_bf16.shape
    PW = n_groups * cg
    col_blk = u_col0 // PW
    hb = tm // POOL_HALO
    n_halo = T // POOL_HALO
    return pl.pallas_call(
        functools.partial(_pool_kernel, tiles_per_seq=LP // tm, seq_len=seq_len),
        out_shape=jax.ShapeDtypeStruct((T, PW), BF16),
        grid=(T // tm,),
        in_specs=[
            pl.BlockSpec((POOL_HALO, PW), lambda i: (jnp.maximum(i * hb - 1, 0), col_blk)),
            pl.BlockSpec((tm, PW), lambda i: (i, col_blk)),
            pl.BlockSpec((POOL_HALO, PW), lambda i: (jnp.minimum((i + 1) * hb, n_halo - 1), col_blk)),
            pl.BlockSpec((n_groups, cg, cg), lambda i: (0, 0, 0)),
            pl.BlockSpec((1, PW), lambda i: (0, 0)),
        ],
        out_specs=pl.BlockSpec((tm, PW), lambda i: (i, 0)),
        compiler_params=_params(("arbitrary",)),
        name="pool",
    )(proj, proj, proj, pool_w_bf16, pool_scale.reshape(1, PW))


def _out_proj_kernel(ret_ref, pool_ref, wr_ref, wp_ref, hs_ref, o_ref):
    y = jnp.dot(ret_ref[...], wr_ref[...], preferred_element_type=F32)
    y = y + jnp.dot(pool_ref[...], wp_ref[...], preferred_element_type=F32)
    o_ref[...] = hs_ref[...] + y


def _out_proj(ret, pool, w_out_bf16, hs, *, tm, tn):
    T, RW = ret.shape
    PW = pool.shape[1]
    D = hs.shape[1]
    assert RW == PW, "the two mixer halves share one row-block split of w_out"
    return pl.pallas_call(
        _out_proj_kernel,
        out_shape=jax.ShapeDtypeStruct((T, D), F32),
        grid=(T // tm, D // tn),
        in_specs=[
            pl.BlockSpec((tm, RW), lambda i, j: (i, 0)),
            pl.BlockSpec((tm, PW), lambda i, j: (i, 0)),
            pl.BlockSpec((RW, tn), lambda i, j: (0, j)),
            pl.BlockSpec((PW, tn), lambda i, j: (1, j)),
            pl.BlockSpec((tm, tn), lambda i, j: (i, j)),
        ],
        out_specs=pl.BlockSpec((tm, tn), lambda i, j: (i, j)),
        compiler_params=_params(("arbitrary", "arbitrary")),
        name="out_proj",
    )(ret, pool, w_out_bf16, w_out_bf16, hs)


def _pack_bf16_pair(lo, hi):
    lo_bits = lax.bitcast_convert_type(lo.astype(BF16).astype(F32), jnp.uint32)
    hi_bits = lax.bitcast_convert_type(hi.astype(BF16).astype(F32), jnp.uint32)
    return (hi_bits & jnp.uint32(0xFFFF0000)) | (lo_bits >> jnp.uint32(16))


def _unpack_bf16_pair(word):
    lo = lax.bitcast_convert_type(word << jnp.uint32(16), F32).astype(BF16)
    hi = lax.bitcast_convert_type(word & jnp.uint32(0xFFFF0000), F32).astype(BF16)
    return lo, hi


def _router_kernel(hs_ref, nw_ref, rw_ref, rwhi_ref, rb_ref, valid_ref,
                   fp_ref, eid_ref, gate_ref, rank_ref, cnt_ref,
                   fhi_scr, flo_scr, carry_scr, *, sub):
    i = pl.program_id(0)
    tm, D = hs_ref.shape
    E = rb_ref.shape[0]
    half = D // 2

    @pl.when(i == 0)
    def _():
        carry_scr[...] = jnp.zeros_like(carry_scr)

    def body(r, c):
        rows = pl.ds(pl.multiple_of(r * sub, sub), sub)
        x = hs_ref[rows, :]
        ms = jnp.mean(x * x, axis=-1, keepdims=True)
        f = x * lax.rsqrt(ms + RMS_EPS) * nw_ref[...]
        f_hi = f.astype(BF16)
        fhi_scr[rows, :] = f_hi
        flo_scr[rows, :] = (f - f_hi.astype(F32)).astype(BF16)
        fp_ref[rows, :] = _pack_bf16_pair(f[:, :half], f[:, half:])
        return c
    lax.fori_loop(0, tm // sub, body, 0)

    nt = (((1,), (1,)), ((), ()))
    part = lax.dot_general(rw_ref[...], fhi_scr[...], nt, preferred_element_type=F32)
    logits = part[:E] + part[E:] + lax.dot_general(rwhi_ref[...], flo_scr[...], nt, preferred_element_type=F32)
    logits = logits + rb_ref[...]

    eio = lax.broadcasted_iota(jnp.int32, (E, tm), 0).astype(F32)
    valid = valid_ref[...]

    vals, onehots = [], []
    l = logits
    for k in range(TOP_K):
        m = jnp.max(l, axis=0, keepdims=True)
        idx = jnp.min(jnp.where(l == m, eio, float(E)), axis=0, keepdims=True)
        sel = eio == idx
        vals.append(m)
        onehots.append(jnp.where(sel, valid, 0.0))
        eid_ref[pl.ds(k, 1), :] = idx.astype(jnp.int32)
        l = jnp.where(sel, -jnp.inf, l)

    exps = [jnp.exp(v - vals[0]) for v in vals]
    denom = exps[0] + exps[1] + exps[2] + exps[3]
    for k in range(TOP_K):
        gate_ref[pl.ds(k, 1), :] = exps[k] / denom

    oh = onehots[0] + onehots[1] + onehots[2] + onehots[3]
    tt = lax.broadcasted_iota(jnp.int32, (tm, tm), 0)
    uu = lax.broadcasted_iota(jnp.int32, (tm, tm), 1)
    upper = jnp.where(tt < uu, 1.0, 0.0).astype(BF16)
    carry = carry_scr[...]
    before = jnp.dot(oh.astype(BF16), upper, preferred_element_type=F32) + carry[:, 0:1]
    for k in range(TOP_K):
        rank_ref[pl.ds(k, 1), :] = jnp.sum(onehots[k] * before, axis=0, keepdims=True).astype(jnp.int32)
    carry = carry + jnp.sum(oh, axis=1, keepdims=True)
    carry_scr[...] = carry
    cnt_ref[...] = carry


def _router(hs1, norm_w, router_w, router_b, is_token, *, tm, sub):
    T, D = hs1.shape
    E = router_w.shape[1]
    rw_t = router_w.T.astype(F32)
    rw_hi = rw_t.astype(BF16)
    rw_lo = (rw_t - rw_hi.astype(F32)).astype(BF16)
    rw_stack = jnp.concatenate([rw_hi, rw_lo], axis=0)
    n = T // tm
    return pl.pallas_call(
        functools.partial(_router_kernel, sub=sub),
        out_shape=(
            jax.ShapeDtypeStruct((T, D // 2), jnp.uint32),
            jax.ShapeDtypeStruct((TOP_K, T), jnp.int32),
            jax.ShapeDtypeStruct((TOP_K, T), F32),
            jax.ShapeDtypeStruct((TOP_K, T), jnp.int32),
            jax.ShapeDtypeStruct((E, CHUNK), F32),
        ),
        grid=(n,),
        in_specs=[
            pl.BlockSpec((tm, D), lambda i: (i, 0)),
            pl.BlockSpec((1, D), lambda i: (0, 0)),
            pl.BlockSpec((2 * E, D), lambda i: (0, 0)),
            pl.BlockSpec((E, D), lambda i: (0, 0)),
            pl.BlockSpec((E, 1), lambda i: (0, 0)),
            pl.BlockSpec((1, tm), lambda i: (0, i)),
        ],
        out_specs=(
            pl.BlockSpec((tm, D // 2), lambda i: (i, 0)),
            pl.BlockSpec((TOP_K, tm), lambda i: (0, i)),
            pl.BlockSpec((TOP_K, tm), lambda i: (0, i)),
            pl.BlockSpec((TOP_K, tm), lambda i: (0, i)),
            pl.BlockSpec((E, CHUNK), lambda i: (0, 0)),
        ),
        scratch_shapes=[pltpu.VMEM((tm, D), BF16), pltpu.VMEM((tm, D), BF16), pltpu.VMEM((E, CHUNK), F32)],
        compiler_params=_params(("arbitrary",)),
        name="router",
    )(hs1, norm_w.reshape(1, D), rw_stack, rw_hi, router_b.reshape(E, 1).astype(F32),
      is_token.astype(F32).reshape(1, T))


def _dispatch_kernel(pend_ref, padded_ref, nu_ref, dest_ref, fp_ref, zeros_ref, xs_hbm, zsem, sem, *, moe_tm):
    i = pl.program_id(0)
    tm = fp_ref.shape[0]
    E = pend_ref.shape[0]
    n_blocks = xs_hbm.shape[0] // moe_tm

    def zero_rows(start):
        return pltpu.make_async_copy(zeros_ref, xs_hbm.at[pl.ds(pl.multiple_of(start, moe_tm), moe_tm)], zsem)

    @pl.when(i == 0)
    def _():
        for e in range(E):
            @pl.when(padded_ref[e] > 0)
            def _():
                zero_rows(pend_ref[e] - moe_tm).start()

        def start_tail(b, c):
            zero_rows(b * moe_tm).start()
            return c
        lax.fori_loop(nu_ref[0], n_blocks, start_tail, 0)
        for e in range(E):
            @pl.when(padded_ref[e] > 0)
            def _():
                zero_rows(pend_ref[e] - moe_tm).wait()

        def wait_tail(b, c):
            zero_rows(b * moe_tm).wait()
            return c
        lax.fori_loop(nu_ref[0], n_blocks, wait_tail, 0)

    def body(r, c):
        for k in range(TOP_K):
            pltpu.make_async_copy(fp_ref.at[pl.ds(r, 1)], xs_hbm.at[pl.ds(dest_ref[k, r], 1)], sem).start()
        return c
    lax.fori_loop(0, tm, body, 0)
    for k in range(TOP_K):
        pltpu.make_async_copy(fp_ref, xs_hbm.at[pl.ds(0, tm)], sem).wait()


def _dispatch(fp, dest3, pend, padded, n_used, *, rows_total, moe_tm, tile):
    T, W = fp.shape
    assert rows_total % moe_tm == 0
    zeros = jnp.zeros((moe_tm, W), fp.dtype)
    grid_spec = pltpu.PrefetchScalarGridSpec(
        num_scalar_prefetch=3,
        grid=(T // tile,),
        in_specs=[
            pl.BlockSpec((None, TOP_K, tile), lambda i, *_: (i, 0, 0), memory_space=pltpu.SMEM),
            pl.BlockSpec((tile, W), lambda i, *_: (i, 0)),
            pl.BlockSpec((moe_tm, W), lambda i, *_: (0, 0)),
        ],
        out_specs=pl.BlockSpec(memory_space=pl.ANY),
        scratch_shapes=[pltpu.SemaphoreType.DMA, pltpu.SemaphoreType.DMA],
    )
    return pl.pallas_call(
        functools.partial(_dispatch_kernel, moe_tm=moe_tm),
        out_shape=jax.ShapeDtypeStruct((rows_total, W), fp.dtype),
        grid_spec=grid_spec,
        compiler_params=_params(("arbitrary",)),
        name="dispatch",
    )(pend, padded, n_used, dest3, fp, zeros)


def _expert_rows_pipeline(rs_ref, pr_ref, nx_ref, x_hbm, o_hbm, xbuf, obuf, zbuf, isem, osem, zsem, st,
                          *, col0, compute):
    big, small = GROUP_SIZES[0], GROUP_SIZES[-1]
    j = pl.program_id(0)
    e = pl.program_id(1)
    nj = pl.num_programs(0)
    E = pl.num_programs(1)
    tn = obuf.shape[2]
    first_step = jnp.logical_and(j == 0, e == 0)
    last_step = jnp.logical_and(j == nj - 1, e == E - 1)

    def items_of(expert):
        pr = pr_ref[expert]
        n_full = pr // big
        rem = pr - n_full * big
        n_items = n_full
        for size in GROUP_SIZES[1:]:
            n_items = n_items + (rem // size) % 2
        return n_full, rem, n_items

    def item(t, n_full, rem):
        cls = jnp.int32(0)
        off = jnp.minimum(t, n_full) * big
        seen = jnp.int32(0)
        before = jnp.int32(0)
        for c, size in enumerate(GROUP_SIZES[1:], start=1):
            has = (rem // size) % 2
            here = jnp.logical_and(has == 1, t == n_full + seen)
            cls = jnp.where(here, c, cls)
            off = jnp.where(here, n_full * big + before, off)
            seen = seen + has
            before = before + has * size
        return cls, off

    def in_copy(row0, size, slot):
        rows = pl.ds(pl.multiple_of(row0, small), size)
        return pltpu.make_async_copy(x_hbm.at[rows], xbuf.at[slot, pl.ds(0, size)], isem.at[slot])

    def out_copy(row0, size, slot):
        rows = pl.ds(pl.multiple_of(row0, small), size)
        return pltpu.make_async_copy(obuf.at[slot, pl.ds(0, size)], o_hbm.at[rows, pl.ds(col0, tn)], osem.at[slot])

    def by_size(cls, fn):
        for c, size in enumerate(GROUP_SIZES):
            @pl.when(cls == c)
            def _():
                fn(size)

    def start_first_item(expert, slot):
        n_full, rem, n_items = items_of(expert)
        cls, _ = item(0, n_full, rem)

        @pl.when(n_items > 0)
        def _():
            by_size(cls, lambda size: in_copy(rs_ref[expert], size, slot).start())

    def wait_pending_out(slot):
        by_size(st[1 + slot], lambda size: out_copy(rs_ref[0], size, slot).wait())
        st[1 + slot] = -1

    @pl.when(first_step)
    def _():
        st[0] = 0
        st[1] = -1
        st[2] = -1
        start_first_item(jnp.where(pr_ref[0] > 0, 0, nx_ref[0]), 0)

    rs = rs_ref[e]
    n_full, rem, n_items = items_of(e)
    nxt = nx_ref[e]
    stream_continues = jnp.logical_not(jnp.logical_and(nxt <= e, j == nj - 1))

    def body(t, carry):
        g = st[0]
        slot = g % 2
        cls, off = item(t, n_full, rem)
        by_size(cls, lambda size: in_copy(rs + off, size, slot).wait())

        @pl.when(t + 1 < n_items)
        def _():
            cls1, off1 = item(t + 1, n_full, rem)
            by_size(cls1, lambda size: in_copy(rs + off1, size, 1 - slot).start())

        @pl.when(jnp.logical_and(t + 1 == n_items, stream_continues))
        def _():
            start_first_item(nxt, 1 - slot)

        wait_pending_out(slot)

        def run(size):
            compute(size, slot)
            out_copy(rs + off, size, slot).start()
        by_size(cls, run)
        st[1 + slot] = cls
        st[0] = g + 1
        return carry
    lax.fori_loop(0, n_items, body, 0)

    @pl.when(last_step)
    def _():
        wait_pending_out(0)
        wait_pending_out(1)

    @pl.when(e == E - 1)
    def _():
        zbuf[...] = jnp.zeros_like(zbuf)
        first = (rs + pr_ref[e]) // small
        last = o_hbm.shape[0] // small

        def fill_copy(b):
            rows = pl.ds(pl.multiple_of(b * small, small), small)
            return pltpu.make_async_copy(zbuf, o_hbm.at[rows, pl.ds(col0, tn)], zsem)

        def start_fill(b, c):
            fill_copy(b).start()
            return c

        def wait_fill(b, c):
            fill_copy(b).wait()
            return c
        lax.fori_loop(first, last, start_fill, 0)
        lax.fori_loop(first, last, wait_fill, 0)


def _next_nonempty(padded):
    E = padded.shape[0]
    ids = jnp.arange(E, dtype=jnp.int32)
    dist = jnp.where(padded[None, :] > 0, (ids[None, :] - ids[:, None] - 1) % E, E)
    return ((ids + 1 + jnp.min(dist, axis=1)) % E).astype(jnp.int32)


def _expert_scratch(big, width_in, dtype_in, tn, dtype_out):
    small = GROUP_SIZES[-1]
    return [pltpu.VMEM((2, big, width_in), dtype_in), pltpu.VMEM((2, big, tn), dtype_out),
            pltpu.VMEM((small, tn), dtype_out),
            pltpu.SemaphoreType.DMA((2,)), pltpu.SemaphoreType.DMA((2,)), pltpu.SemaphoreType.DMA,
            pltpu.SMEM((3,), jnp.int32)]


def _gate_up_kernel(rs_ref, pr_ref, nx_ref, xs_hbm, wg_ref, wu_ref, bg_ref, bu_ref, h_hbm,
                    xbuf, obuf, zbuf, isem, osem, zsem, st):
    tn = obuf.shape[2]

    def compute(size, slot):
        lo, hi = _unpack_bf16_pair(xbuf[slot, pl.ds(0, size), :])
        x = jnp.concatenate([lo, hi], axis=-1)
        gate = jnp.dot(x, wg_ref[...].astype(BF16), preferred_element_type=F32) + bg_ref[...]
        up = jnp.dot(x, wu_ref[...].astype(BF16), preferred_element_type=F32) + bu_ref[...]
        gate = jnp.minimum(gate, SWIGLU_LIMIT)
        up = jnp.clip(up, -SWIGLU_LIMIT, SWIGLU_LIMIT)
        obuf[slot, pl.ds(0, size), :] = ((up + 1.0) * gate * jax.nn.sigmoid(SWIGLU_ALPHA * gate)).astype(obuf.dtype)

    _expert_rows_pipeline(rs_ref, pr_ref, nx_ref, xs_hbm, h_hbm, xbuf, obuf, zbuf, isem, osem, zsem, st,
                          col0=pl.multiple_of(pl.program_id(0) * tn, tn), compute=compute)


def _gate_up(xs, row_start, row_count, w_gate_up, b_gate_up, *, rows, tn):
    E, D, F2 = w_gate_up.shape
    FF = F2 // 2
    nj = FF // tn
    W = xs.shape[1]
    grid_spec = pltpu.PrefetchScalarGridSpec(
        num_scalar_prefetch=3,
        grid=(nj, E),
        in_specs=[
            pl.BlockSpec(memory_space=pl.ANY),
            pl.BlockSpec((None, D, tn), lambda j, e, *_: (e, 0, j)),
            pl.BlockSpec((None, D, tn), lambda j, e, *_: (e, 0, nj + j)),
            pl.BlockSpec((None, 1, tn), lambda j, e, *_: (e, 0, j)),
            pl.BlockSpec((None, 1, tn), lambda j, e, *_: (e, 0, nj + j)),
        ],
        out_specs=pl.BlockSpec(memory_space=pl.ANY),
        scratch_shapes=_expert_scratch(GROUP_SIZES[0], W, xs.dtype, tn, BF16),
    )
    b3 = b_gate_up.reshape(E, 1, F2)
    return pl.pallas_call(
        _gate_up_kernel,
        out_shape=jax.ShapeDtypeStruct((rows, FF), BF16),
        grid_spec=grid_spec,
        compiler_params=_params(("arbitrary", "arbitrary")),
        name="gate_up",
    )(row_start, row_count, _next_nonempty(row_count), xs, w_gate_up, w_gate_up, b3, b3)


def _down_kernel(rs_ref, pr_ref, nx_ref, h_hbm, w_ref, b_ref, y_hbm, xbuf, obuf, zbuf, isem, osem, zsem, st):
    tn = obuf.shape[2]

    def compute(size, slot):
        hrows = xbuf[slot, pl.ds(0, size), :]
        obuf[slot, pl.ds(0, size), :] = jnp.dot(hrows, w_ref[...].astype(BF16), preferred_element_type=F32) + b_ref[...]

    _expert_rows_pipeline(rs_ref, pr_ref, nx_ref, h_hbm, y_hbm, xbuf, obuf, zbuf, isem, osem, zsem, st,
                          col0=pl.multiple_of(pl.program_id(0) * tn, tn), compute=compute)


def _down(h, row_start, row_count, w_down, b_down, *, tn):
    E, FF, D = w_down.shape
    rows = h.shape[0]
    grid_spec = pltpu.PrefetchScalarGridSpec(
        num_scalar_prefetch=3,
        grid=(D // tn, E),
        in_specs=[
            pl.BlockSpec(memory_space=pl.ANY),
            pl.BlockSpec((None, FF, tn), lambda j, e, *_: (e, 0, j)),
            pl.BlockSpec((None, 1, tn), lambda j, e, *_: (e, 0, j)),
        ],
        out_specs=pl.BlockSpec(memory_space=pl.ANY),
        scratch_shapes=_expert_scratch(GROUP_SIZES[0], FF, h.dtype, tn, F32),
    )
    return pl.pallas_call(
        _down_kernel,
        out_shape=jax.ShapeDtypeStruct((rows, D), F32),
        grid_spec=grid_spec,
        compiler_params=_params(("arbitrary", "arbitrary")),
        name="down",
    )(row_start, row_count, _next_nonempty(row_count), h, w_down, b_down.reshape(E, 1, D))


def _combine_kernel(dest_ref, dnext_ref, hs_ref, gate_ref, nw_ref, y_hbm, o_ref, ybuf, sem, *, sub):
    s = pl.program_id(0)
    n = pl.num_programs(0)
    tm = hs_ref.shape[0]
    slot = s % 2

    def start_gather(d_ref, to_slot):
        def body(r, c):
            for k in range(TOP_K):
                pltpu.make_async_copy(y_hbm.at[pl.ds(d_ref[k, r], 1)], ybuf.at[to_slot, k, pl.ds(r, 1)],
                                      sem.at[to_slot]).start()
            return c
        lax.fori_loop(0, tm, body, 0)

    @pl.when(s == 0)
    def _():
        start_gather(dest_ref, 0)

    @pl.when(s + 1 < n)
    def _():
        start_gather(dnext_ref, 1 - slot)

    for k in range(TOP_K):
        pltpu.make_async_copy(y_hbm.at[pl.ds(0, tm)], ybuf.at[slot, k], sem.at[slot]).wait()

    def body(r, c):
        rows = pl.ds(pl.multiple_of(r * sub, sub), sub)
        g = gate_ref[rows, :]
        acc = hs_ref[rows, :]
        for k in range(TOP_K):
            acc = acc + ybuf[slot, k, rows, :] * g[:, k:k + 1]
        ms = jnp.mean(acc * acc, axis=-1, keepdims=True)
        o_ref[rows, :] = acc * lax.rsqrt(ms + RMS_EPS) * nw_ref[...]
        return c
    lax.fori_loop(0, tm // sub, body, 0)


def _combine(hs1, gates_t, dest3, y, final_norm_w, *, B, LP, seq_len, tile, sub):
    T, D = hs1.shape
    per_seq = LP // tile
    n_seq_tiles = seq_len // tile
    lead = (LP - seq_len) // tile
    n_steps = B * n_seq_tiles

    def tok(s):
        return (s // n_seq_tiles) * per_seq + lead + s % n_seq_tiles

    return pl.pallas_call(
        functools.partial(_combine_kernel, sub=sub),
        out_shape=jax.ShapeDtypeStruct((B, seq_len, D), F32),
        grid=(n_steps,),
        in_specs=[
            pl.BlockSpec((None, TOP_K, tile), lambda s: (tok(s), 0, 0), memory_space=pltpu.SMEM),
            pl.BlockSpec((None, TOP_K, tile), lambda s: (tok(jnp.minimum(s + 1, n_steps - 1)), 0, 0),
                         memory_space=pltpu.SMEM),
            pl.BlockSpec((tile, D), lambda s: (tok(s), 0)),
            pl.BlockSpec((tile, TOP_K), lambda s: (tok(s), 0)),
            pl.BlockSpec((1, D), lambda s: (0, 0)),
            pl.BlockSpec(memory_space=pl.ANY),
        ],
        out_specs=pl.BlockSpec((None, tile, D), lambda s: (s // n_seq_tiles, s % n_seq_tiles, 0)),
        scratch_shapes=[pltpu.VMEM((2, TOP_K, tile, D), F32), pltpu.SemaphoreType.DMA((2,))],
        compiler_params=_params(("arbitrary",)),
        name="combine",
    )(dest3, dest3, hs1, gates_t, final_norm_w.reshape(1, D), y)


def _forward(x, meta_tokens, norm_mix_w, w_in, ret_log_rate_fwd, ret_log_rate_bwd, pool_w, pool_scale,
             w_out, norm_ffn_w, router_w, router_b, w_gate_up, b_gate_up, w_down, b_down, final_norm_w, tiles):
    B, S, D = x.shape
    depth = norm_mix_w.shape[0]
    assert depth == 1, "single-layer block"
    H = ret_log_rate_fwd.shape[-1]
    n_groups, cg = pool_w.shape[1], pool_w.shape[2]
    PW = n_groups * cg
    RW = w_out.shape[1] - PW
    d = RW // H
    E = router_w.shape[-1]
    assert w_in.shape[-1] == 4 * RW + PW and S % CHUNK == 0
    LP = PAD + N_META + S
    T = B * LP
    t = tiles

    hs0 = jnp.concatenate(
        [jnp.zeros((B, PAD, D), x.dtype), jnp.broadcast_to(meta_tokens.astype(x.dtype)[None], (B, N_META, D)), x],
        axis=1).reshape(T, D)

    proj = _in_proj(hs0, norm_mix_w[0], w_in[0].astype(BF16), tm=t["in_tm"], tn=t["in_tn"], sub=t["norm_rows"])
    pos = (jnp.arange(LP, dtype=jnp.int32) - PAD).astype(F32)
    inv_freq = ROPE_BASE ** (-jnp.arange(d // 2, dtype=F32) / (d // 2))
    ang = pos[:, None] * inv_freq[None, :]
    log_gamma = jnp.stack([-jnp.exp(ret_log_rate_fwd[0].astype(F32)), -jnp.exp(ret_log_rate_bwd[0].astype(F32))])
    ret = _retention(proj, log_gamma, jnp.cos(ang), jnp.sin(ang), B=B, LP=LP, H=H, d=d, GC=t["ret_group_chunks"],
                     HB=t["ret_heads_per_step"])
    pooled = _pool(proj, pool_w[0].astype(BF16), pool_scale[0].astype(F32), LP=LP, seq_len=N_META + S,
                   u_col0=4 * RW, tm=t["pool_tm"])
    hs1 = _out_proj(ret, pooled, w_out[0].astype(BF16), hs0, tm=t["out_tm"], tn=t["out_tn"])

    row_id = jnp.arange(T, dtype=jnp.int32)
    is_token = (row_id % LP) >= PAD
    fp, eid, gates, rank, cnt = _router(hs1, norm_ffn_w[0], router_w[0], router_b[0], is_token,
                                        tm=t["router_tm"], sub=t["norm_rows"])
    align = GROUP_SIZES[-1]
    tile = t["token_tile"]
    counts = cnt[:, 0].astype(jnp.int32)
    padded = (counts + align - 1) // align * align
    pend = jnp.cumsum(padded).astype(jnp.int32)
    pstart = pend - padded
    n_assign = B * (N_META + S) * TOP_K
    rows = -(-(n_assign + E * (align - 1)) // align) * align
    n_used = pend[-1:] // align
    spare = rows + jnp.arange(TOP_K, dtype=jnp.int32)[:, None] * tile + (row_id % tile)[None, :]
    expert_ids = jnp.arange(E, dtype=jnp.int32)[:, None, None]
    first_row = jnp.sum(jnp.where(eid[None] == expert_ids, pstart[:, None, None], 0), axis=0)
    dest = jnp.where(is_token[None, :], first_row + rank, spare)
    dest3 = dest.reshape(TOP_K, T // tile, tile).transpose(1, 0, 2)

    spare_rows = -(-(TOP_K * tile) // align) * align
    xs = _dispatch(fp, dest3, pend, padded, n_used, rows_total=rows + spare_rows, moe_tm=align, tile=tile)
    hmid = _gate_up(xs, pstart, padded, w_gate_up[0], b_gate_up[0], rows=rows, tn=t["gu_tn"])
    y = _down(hmid, pstart, padded, w_down[0], b_down[0], tn=t["down_tn"])
    return _combine(hs1, gates.T, dest3, y, final_norm_w, B=B, LP=LP, seq_len=S, tile=tile, sub=t["combine_rows"])


def kernel(x, meta_tokens, norm_mix_w, w_in, ret_log_rate_fwd, ret_log_rate_bwd, pool_w, pool_scale, w_out,
           norm_ffn_w, router_w, router_b, w_gate_up, b_gate_up, w_down, b_down, final_norm_w):
    return _forward(x, meta_tokens, norm_mix_w, w_in, ret_log_rate_fwd, ret_log_rate_bwd, pool_w, pool_scale,
                    w_out, norm_ffn_w, router_w, router_b, w_gate_up, b_gate_up, w_down, b_down, final_norm_w,
                    DEFAULT_TILES)
```

```python
import functools

import jax
import jax.numpy as jnp
from jax import lax
from jax.experimental import pallas as pl
from jax.experimental.pallas import tpu as pltpu

N_META = 16
CHUNK = 128
PAD = CHUNK - N_META
POOL_WINDOWS = (2, 4, 8, 16)
POOL_HALO = 64
TOP_K = 4
GROUP_SIZES = (1024, 512, 256, 128)
SWIGLU_LIMIT = 7.0
SWIGLU_ALPHA = 1.702
RMS_EPS = 1e-5
ROPE_BASE = 10000.0

V7X_VMEM_BYTES = 64 * 1024 * 1024
VMEM_LIMIT = V7X_VMEM_BYTES - 2 * 1024 * 1024

F32 = jnp.float32
BF16 = jnp.bfloat16

DEFAULT_TILES = dict(
    in_tm=640, in_tn=1024, norm_rows=32,
    ret_group_chunks=13, ret_heads_per_step=2,
    pool_tm=640,
    out_tm=1280, out_tn=512,
    router_tm=256,
    gu_tn=512, down_tn=2048,
    token_tile=128, combine_rows=32,
)


def _params(semantics):
    return pltpu.CompilerParams(dimension_semantics=semantics, vmem_limit_bytes=VMEM_LIMIT)


def _in_proj_kernel(x_ref, nw_ref, w_ref, o_ref, a_scr, *, sub):
    @pl.when(pl.program_id(1) == 0)
    def _():
        def body(r, c):
            rows = pl.ds(pl.multiple_of(r * sub, sub), sub)
            x = x_ref[rows, :]
            ms = jnp.mean(x * x, axis=-1, keepdims=True)
            a_scr[rows, :] = (x * lax.rsqrt(ms + RMS_EPS) * nw_ref[...]).astype(a_scr.dtype)
            return c
        lax.fori_loop(0, x_ref.shape[0] // sub, body, 0)

    o_ref[...] = jnp.dot(a_scr[...], w_ref[...], preferred_element_type=F32).astype(o_ref.dtype)


def _in_proj(hs, norm_w, w_in_bf16, *, tm, tn, sub):
    T, D = hs.shape
    N = w_in_bf16.shape[1]
    return pl.pallas_call(
        functools.partial(_in_proj_kernel, sub=sub),
        out_shape=jax.ShapeDtypeStruct((T, N), BF16),
        grid=(T // tm, N // tn),
        in_specs=[
            pl.BlockSpec((tm, D), lambda i, j: (i, 0)),
            pl.BlockSpec((1, D), lambda i, j: (0, 0)),
            pl.BlockSpec((D, tn), lambda i, j: (0, j)),
        ],
        out_specs=pl.BlockSpec((tm, tn), lambda i, j: (i, j)),
        scratch_shapes=[pltpu.VMEM((tm, D), BF16)],
        compiler_params=_params(("arbitrary", "arbitrary")),
        name="in_proj",
    )(hs, norm_w.reshape(1, D), w_in_bf16)


def _retention_kernel(lg_ref, q_ref, k_ref, v_ref, g_ref, cos_ref, sin_ref, o_ref,
                      acc_scr, sf_scr, sb_scr, dmat_scr, dvec_scr, *, G, GC):
    hb = pl.program_id(1)
    s = pl.program_id(2)
    phase = s // G
    w = s % G
    gi = w + phase * (G - 1 - 2 * w)
    C = CHUNK
    d = cos_ref.shape[1] * 2
    half = d // 2
    HB = q_ref.shape[1] // d
    GR = GC * C
    k_scale = d ** -0.5
    nt = (((1,), (1,)), ((), ()))
    tn = (((0,), (0,)), ((), ()))

    @pl.when(s == 0)
    def _():
        t = lax.broadcasted_iota(jnp.int32, (C, C), 0).astype(F32)
        u = lax.broadcasted_iota(jnp.int32, (C, C), 1).astype(F32)
        diff = t - u
        idx = lax.broadcasted_iota(jnp.int32, (C, d), 0).astype(F32)
        for hh in range(HB):
            lgf = lg_ref[0, hb * HB + hh]
            lgb = lg_ref[1, hb * HB + hh]
            dmat_scr[hh] = jnp.where(diff >= 0.0, jnp.exp(lgf * jnp.maximum(diff, 0.0)),
                                     jnp.exp(lgb * jnp.maximum(-diff, 0.0)))
            dvec_scr[hh, 0] = jnp.exp(lgf * (idx + 1.0))
            dvec_scr[hh, 1] = jnp.exp(lgf * (C - 1.0 - idx))
            dvec_scr[hh, 2] = jnp.exp(lgb * (C - idx))
            dvec_scr[hh, 3] = jnp.exp(lgb * idx)
            dvec_scr[hh, 4] = jnp.exp(jnp.full((C, d), lgf * C, F32))
            dvec_scr[hh, 5] = jnp.exp(jnp.full((C, d), lgb * C, F32))
        sf_scr[...] = jnp.zeros_like(sf_scr)

    @pl.when(s == G)
    def _():
        sb_scr[...] = jnp.zeros_like(sb_scr)

    def rotary(ref, rows, hh):
        x = ref[rows, hh * d:(hh + 1) * d].astype(F32)
        x1 = x[:, :half]
        x2 = x[:, half:]
        cs = cos_ref[rows, :]
        sn = sin_ref[rows, :]
        return jnp.concatenate([x1 * cs - x2 * sn, x2 * cs + x1 * sn], axis=-1)

    @pl.when(phase == 0)
    def _():
        def body(c, carry):
            rows = pl.ds(pl.multiple_of(c * C, C), C)
            arows = pl.ds(pl.multiple_of(gi * GR + c * C, C), C)
            for hh in range(HB):
                cols = slice(hh * d, (hh + 1) * d)
                qc = rotary(q_ref, rows, hh).astype(BF16)
                kf = rotary(k_ref, rows, hh) * k_scale
                vc = v_ref[rows, cols]
                sc = lax.dot_general(qc, kf.astype(BF16), nt, preferred_element_type=F32)
                p = (sc * dmat_scr[hh]).astype(BF16)
                inner = jnp.dot(p, vc, preferred_element_type=F32)
                sf = sf_scr[hh]
                cross = jnp.dot(qc, sf.astype(BF16), preferred_element_type=F32) * dvec_scr[hh, 0]
                acc_scr[arows, cols] = inner + cross
                kd = (kf * dvec_scr[hh, 1]).astype(BF16)
                sf_scr[hh] = sf * dvec_scr[hh, 4, 0:1, :] + lax.dot_general(kd, vc, tn, preferred_element_type=F32)
            return carry
        lax.fori_loop(0, GC, body, 0)

    @pl.when(phase == 1)
    def _():
        def body(cc, carry):
            c = GC - 1 - cc
            rows = pl.ds(pl.multiple_of(c * C, C), C)
            arows = pl.ds(pl.multiple_of(gi * GR + c * C, C), C)
            for hh in range(HB):
                cols = slice(hh * d, (hh + 1) * d)
                qc = rotary(q_ref, rows, hh).astype(BF16)
                kf = rotary(k_ref, rows, hh) * k_scale
                vc = v_ref[rows, cols]
                sb = sb_scr[hh]
                r = acc_scr[arows, cols] + jnp.dot(qc, sb.astype(BF16), preferred_element_type=F32) * dvec_scr[hh, 2]
                kd = (kf * dvec_scr[hh, 3]).astype(BF16)
                sb_scr[hh] = sb * dvec_scr[hh, 5, 0:1, :] + lax.dot_general(kd, vc, tn, preferred_element_type=F32)
                rn = r * lax.rsqrt(jnp.mean(r * r, axis=-1, keepdims=True) + RMS_EPS)
                gg = g_ref[rows, cols].astype(F32)
                o_ref[rows, cols] = (gg * jax.nn.sigmoid(gg) * rn).astype(o_ref.dtype)
            return carry
        lax.fori_loop(0, GC, body, 0)


def _retention(proj, log_gamma, cos_tab, sin_tab, *, B, LP, H, d, GC, HB):
    T = proj.shape[0]
    GR = GC * CHUNK
    G = LP // GR
    assert H % HB == 0
    H = H // HB
    bw = HB * d

    def grp(s):
        phase = s // G
        w = s % G
        return phase, w + phase * (G - 1 - 2 * w)

    def qkv_map(col0):
        def m(b, h, s):
            _, gi = grp(s)
            return (b * G + gi, col0 + h)
        return m

    def late_map(col0):
        def m(b, h, s):
            phase, gi = grp(s)
            return (b * G + jnp.where(phase == 0, G - 1, gi), col0 + h)
        return m

    def tab_map(b, h, s):
        _, gi = grp(s)
        return (gi, 0)

    return pl.pallas_call(
        functools.partial(_retention_kernel, G=G, GC=GC),
        out_shape=jax.ShapeDtypeStruct((T, H * bw), BF16),
        grid=(B, H, 2 * G),
        in_specs=[
            pl.BlockSpec(memory_space=pltpu.SMEM),
            pl.BlockSpec((GR, bw), qkv_map(0)),
            pl.BlockSpec((GR, bw), qkv_map(H)),
            pl.BlockSpec((GR, bw), qkv_map(2 * H)),
            pl.BlockSpec((GR, bw), late_map(3 * H)),
            pl.BlockSpec((GR, d // 2), tab_map),
            pl.BlockSpec((GR, d // 2), tab_map),
        ],
        out_specs=pl.BlockSpec((GR, bw), late_map(0)),
        scratch_shapes=[
            pltpu.VMEM((LP, bw), F32),
            pltpu.VMEM((HB, d, d), F32),
            pltpu.VMEM((HB, d, d), F32),
            pltpu.VMEM((HB, CHUNK, CHUNK), F32),
            pltpu.VMEM((HB, 6, CHUNK, d), F32),
        ],
        compiler_params=_params(("arbitrary", "arbitrary", "arbitrary")),
        name="retention",
    )(log_gamma, proj, proj, proj, proj, cos_tab, sin_tab)


def _pool_kernel(prev_ref, cur_ref, next_ref, pw_ref, ps_ref, o_ref, *, tiles_per_seq, seq_len):
    i = pl.program_id(0)
    tm = cur_ref.shape[0]
    ext = tm + 2 * POOL_HALO
    n_groups = pw_ref.shape[0]
    cg = pw_ref.shape[1]
    row0 = (i % tiles_per_seq) * tm - PAD
    pos_t = row0 + lax.broadcasted_iota(jnp.int32, (tm, ext), 0)
    pos_s = row0 - POOL_HALO + lax.broadcasted_iota(jnp.int32, (tm, ext), 1)
    pos_c = row0 + lax.broadcasted_iota(jnp.int32, (tm, 1), 0)
    for gi in range(n_groups):
        win = POOL_WINDOWS[gi]
        cols = slice(gi * cg, (gi + 1) * cg)
        lo = jnp.clip(pos_t - win // 2, 0, seq_len)
        hi = jnp.clip(pos_t + (win - win // 2), 0, seq_len)
        band = jnp.where(pos_s >= lo, jnp.where(pos_s < hi, 1.0, 0.0), 0.0).astype(BF16)
        cnt = (jnp.clip(pos_c + (win - win // 2), 0, seq_len) - jnp.clip(pos_c - win // 2, 0, seq_len)).astype(F32)
        inv = jnp.where(pos_c >= 0, 1.0 / jnp.maximum(cnt, 1.0), 0.0)
        u_cur = cur_ref[:, cols]
        u_ext = jnp.concatenate([prev_ref[:, cols], u_cur, next_ref[:, cols]], axis=0)
        wsum = jnp.dot(band, u_ext, preferred_element_type=F32)
        pooled = wsum * inv - u_cur.astype(F32)
        y = jnp.dot(pooled.astype(BF16), pw_ref[gi], preferred_element_type=F32)
        o_ref[:, cols] = (y * ps_ref[:, cols]).astype(o_ref.dtype)


def _pool(proj, pool_w_bf16, pool_scale, *, LP, seq_len, u_col0, tm):
    T = proj.shape[0]
    n_groups, cg, _ = pool_w_bf16.shape
    PW = n_groups * cg
    col_blk = u_col0 // PW
    hb = tm // POOL_HALO
    n_halo = T // POOL_HALO
    return pl.pallas_call(
        functools.partial(_pool_kernel, tiles_per_seq=LP // tm, seq_len=seq_len),
        out_shape=jax.ShapeDtypeStruct((T, PW), BF16),
        grid=(T // tm,),
        in_specs=[
            pl.BlockSpec((POOL_HALO, PW), lambda i: (jnp.maximum(i * hb - 1, 0), col_blk)),
            pl.BlockSpec((tm, PW), lambda i: (i, col_blk)),
            pl.BlockSpec((POOL_HALO, PW), lambda i: (jnp.minimum((i + 1) * hb, n_halo - 1), col_blk)),
            pl.BlockSpec((n_groups, cg, cg), lambda i: (0, 0, 0)),
            pl.BlockSpec((1, PW), lambda i: (0, 0)),
        ],
        out_specs=pl.BlockSpec((tm, PW), lambda i: (i, 0)),
        compiler_params=_params(("arbitrary",)),
        name="pool",
    )(proj, proj, proj, pool_w_bf16, pool_scale.reshape(1, PW))


def _out_proj_kernel(ret_ref, pool_ref, wr_ref, wp_ref, hs_ref, o_ref):
    y = jnp.dot(ret_ref[...], wr_ref[...], preferred_element_type=F32)
    y = y + jnp.dot(pool_ref[...], wp_ref[...], preferred_element_type=F32)
    o_ref[...] = hs_ref[...] + y


def _out_proj(ret, pool, w_out_bf16, hs, *, tm, tn):
    T, RW = ret.shape
    PW = pool.shape[1]
    D = hs.shape[1]
    assert RW == PW, "the two mixer halves share one row-block split of w_out"
    return pl.pallas_call(
        _out_proj_kernel,
        out_shape=jax.ShapeDtypeStruct((T, D), F32),
        grid=(T // tm, D // tn),
        in_specs=[
            pl.BlockSpec((tm, RW), lambda i, j: (i, 0)),
            pl.BlockSpec((tm, PW), lambda i, j: (i, 0)),
            pl.BlockSpec((RW, tn), lambda i, j: (0, j)),
            pl.BlockSpec((PW, tn), lambda i, j: (1, j)),
            pl.BlockSpec((tm, tn), lambda i, j: (i, j)),
        ],
        out_specs=pl.BlockSpec((tm, tn), lambda i, j: (i, j)),
        compiler_params=_params(("arbitrary", "arbitrary")),
        name="out_proj",
    )(ret, pool, w_out_bf16, w_out_bf16, hs)


def _pack_bf16_pair(lo, hi):
    lo_bits = lax.bitcast_convert_type(lo.astype(BF16).astype(F32), jnp.uint32)
    hi_bits = lax.bitcast_convert_type(hi.astype(BF16).astype(F32), jnp.uint32)
    return (hi_bits & jnp.uint32(0xFFFF0000)) | (lo_bits >> jnp.uint32(16))


def _unpack_bf16_pair(word):
    lo = lax.bitcast_convert_type(word << jnp.uint32(16), F32).astype(BF16)
    hi = lax.bitcast_convert_type(word & jnp.uint32(0xFFFF0000), F32).astype(BF16)
    return lo, hi


def _router_kernel(hs_ref, nw_ref, rw_ref, rwhi_ref, rb_ref, valid_ref,
                   fp_ref, eid_ref, gate_ref, rank_ref, cnt_ref,
                   fhi_scr, flo_scr, carry_scr, *, sub):
    i = pl.program_id(0)
    tm, D = hs_ref.shape
    E = rb_ref.shape[0]
    half = D // 2

    @pl.when(i == 0)
    def _():
        carry_scr[...] = jnp.zeros_like(carry_scr)

    def body(r, c):
        rows = pl.ds(pl.multiple_of(r * sub, sub), sub)
        x = hs_ref[rows, :]
        ms = jnp.mean(x * x, axis=-1, keepdims=True)
        f = x * lax.rsqrt(ms + RMS_EPS) * nw_ref[...]
        f_hi = f.astype(BF16)
        fhi_scr[rows, :] = f_hi
        flo_scr[rows, :] = (f - f_hi.astype(F32)).astype(BF16)
        fp_ref[rows, :] = _pack_bf16_pair(f[:, :half], f[:, half:])
        return c
    lax.fori_loop(0, tm // sub, body, 0)

    nt = (((1,), (1,)), ((), ()))
    part = lax.dot_general(rw_ref[...], fhi_scr[...], nt, preferred_element_type=F32)
    logits = part[:E] + part[E:] + lax.dot_general(rwhi_ref[...], flo_scr[...], nt, preferred_element_type=F32)
    logits = logits + rb_ref[...]

    eio = lax.broadcasted_iota(jnp.int32, (E, tm), 0).astype(F32)
    valid = valid_ref[...]

    vals, onehots = [], []
    l = logits
    for k in range(TOP_K):
        m = jnp.max(l, axis=0, keepdims=True)
        idx = jnp.min(jnp.where(l == m, eio, float(E)), axis=0, keepdims=True)
        sel = eio == idx
        vals.append(m)
        onehots.append(jnp.where(sel, valid, 0.0))
        eid_ref[pl.ds(k, 1), :] = idx.astype(jnp.int32)
        l = jnp.where(sel, -jnp.inf, l)

    exps = [jnp.exp(v - vals[0]) for v in vals]
    denom = exps[0] + exps[1] + exps[2] + exps[3]
    for k in range(TOP_K):
        gate_ref[pl.ds(k, 1), :] = exps[k] / denom

    oh = onehots[0] + onehots[1] + onehots[2] + onehots[3]
    tt = lax.broadcasted_iota(jnp.int32, (tm, tm), 0)
    uu = lax.broadcasted_iota(jnp.int32, (tm, tm), 1)
    upper = jnp.where(tt < uu, 1.0, 0.0).astype(BF16)
    carry = carry_scr[...]
    before = jnp.dot(oh.astype(BF16), upper, preferred_element_type=F32) + carry[:, 0:1]
    for k in range(TOP_K):
        rank_ref[pl.ds(k, 1), :] = jnp.sum(onehots[k] * before, axis=0, keepdims=True).astype(jnp.int32)
    carry = carry + jnp.sum(oh, axis=1, keepdims=True)
    carry_scr[...] = carry
    cnt_ref[...] = carry


def _router(hs1, norm_w, router_w, router_b, is_token, *, tm, sub):
    T, D = hs1.shape
    E = router_w.shape[1]
    rw_t = router_w.T.astype(F32)
    rw_hi = rw_t.astype(BF16)
    rw_lo = (rw_t - rw_hi.astype(F32)).astype(BF16)
    rw_stack = jnp.concatenate([rw_hi, rw_lo], axis=0)
    n = T // tm
    return pl.pallas_call(
        functools.partial(_router_kernel, sub=sub),
        out_shape=(
            jax.ShapeDtypeStruct((T, D // 2), jnp.uint32),
            jax.ShapeDtypeStruct((TOP_K, T), jnp.int32),
            jax.ShapeDtypeStruct((TOP_K, T), F32),
            jax.ShapeDtypeStruct((TOP_K, T), jnp.int32),
            jax.ShapeDtypeStruct((E, CHUNK), F32),
        ),
        grid=(n,),
        in_specs=[
            pl.BlockSpec((tm, D), lambda i: (i, 0)),
            pl.BlockSpec((1, D), lambda i: (0, 0)),
            pl.BlockSpec((2 * E, D), lambda i: (0, 0)),
            pl.BlockSpec((E, D), lambda i: (0, 0)),
            pl.BlockSpec((E, 1), lambda i: (0, 0)),
            pl.BlockSpec((1, tm), lambda i: (0, i)),
        ],
        out_specs=(
            pl.BlockSpec((tm, D // 2), lambda i: (i, 0)),
            pl.BlockSpec((TOP_K, tm), lambda i: (0, i)),
            pl.BlockSpec((TOP_K, tm), lambda i: (0, i)),
            pl.BlockSpec((TOP_K, tm), lambda i: (0, i)),
            pl.BlockSpec((E, CHUNK), lambda i: (0, 0)),
        ),
        scratch_shapes=[pltpu.VMEM((tm, D), BF16), pltpu.VMEM((tm, D), BF16), pltpu.VMEM((E, CHUNK), F32)],
        compiler_params=_params(("arbitrary",)),
        name="router",
    )(hs1, norm_w.reshape(1, D), rw_stack, rw_hi, router_b.reshape(E, 1).astype(F32),
      is_token.astype(F32).reshape(1, T))


def _dispatch_kernel(pend_ref, padded_ref, nu_ref, dest_ref, fp_ref, zeros_ref, xs_hbm, zsem, sem, *, moe_tm):
    i = pl.program_id(0)
    tm = fp_ref.shape[0]
    E = pend_ref.shape[0]
    n_blocks = xs_hbm.shape[0] // moe_tm

    def zero_rows(start):
        return pltpu.make_async_copy(zeros_ref, xs_hbm.at[pl.ds(pl.multiple_of(start, moe_tm), moe_tm)], zsem)

    @pl.when(i == 0)
    def _():
        for e in range(E):
            @pl.when(padded_ref[e] > 0)
            def _():
                zero_rows(pend_ref[e] - moe_tm).start()

        def start_tail(b, c):
            zero_rows(b * moe_tm).start()
            return c
        lax.fori_loop(nu_ref[0], n_blocks, start_tail, 0)
        for e in range(E):
            @pl.when(padded_ref[e] > 0)
            def _():
                zero_rows(pend_ref[e] - moe_tm).wait()

        def wait_tail(b, c):
            zero_rows(b * moe_tm).wait()
            return c
        lax.fori_loop(nu_ref[0], n_blocks, wait_tail, 0)

    def body(r, c):
        for k in range(TOP_K):
            pltpu.make_async_copy(fp_ref.at[pl.ds(r, 1)], xs_hbm.at[pl.ds(dest_ref[k, r], 1)], sem).start()
        return c
    lax.fori_loop(0, tm, body, 0)
    for k in range(TOP_K):
        pltpu.make_async_copy(fp_ref, xs_hbm.at[pl.ds(0, tm)], sem).wait()


def _dispatch(fp, dest3, pend, padded, n_used, *, rows_total, moe_tm, tile):
    T, W = fp.shape
    assert rows_total % moe_tm == 0
    zeros = jnp.zeros((moe_tm, W), fp.dtype)
    grid_spec = pltpu.PrefetchScalarGridSpec(
        num_scalar_prefetch=3,
        grid=(T // tile,),
        in_specs=[
            pl.BlockSpec((None, TOP_K, tile), lambda i, *_: (i, 0, 0), memory_space=pltpu.SMEM),
            pl.BlockSpec((tile, W), lambda i, *_: (i, 0)),
            pl.BlockSpec((moe_tm, W), lambda i, *_: (0, 0)),
        ],
        out_specs=pl.BlockSpec(memory_space=pl.ANY),
        scratch_shapes=[pltpu.SemaphoreType.DMA, pltpu.SemaphoreType.DMA],
    )
    return pl.pallas_call(
        functools.partial(_dispatch_kernel, moe_tm=moe_tm),
        out_shape=jax.ShapeDtypeStruct((rows_total, W), fp.dtype),
        grid_spec=grid_spec,
        compiler_params=_params(("arbitrary",)),
        name="dispatch",
    )(pend, padded, n_used, dest3, fp, zeros)


def _expert_rows_pipeline(rs_ref, pr_ref, nx_ref, x_hbm, o_hbm, xbuf, obuf, zbuf, isem, osem, zsem, st,
                          *, col0, compute):
    big, small = GROUP_SIZES[0], GROUP_SIZES[-1]
    j = pl.program_id(0)
    e = pl.program_id(1)
    nj = pl.num_programs(0)
    E = pl.num_programs(1)
    tn = obuf.shape[2]
    first_step = jnp.logical_and(j == 0, e == 0)
    last_step = jnp.logical_and(j == nj - 1, e == E - 1)

    def items_of(expert):
        pr = pr_ref[expert]
        n_full = pr // big
        rem = pr - n_full * big
        n_items = n_full
        for size in GROUP_SIZES[1:]:
            n_items = n_items + (rem // size) % 2
        return n_full, rem, n_items

    def item(t, n_full, rem):
        cls = jnp.int32(0)
        off = jnp.minimum(t, n_full) * big
        seen = jnp.int32(0)
        before = jnp.int32(0)
        for c, size in enumerate(GROUP_SIZES[1:], start=1):
            has = (rem // size) % 2
            here = jnp.logical_and(has == 1, t == n_full + seen)
            cls = jnp.where(here, c, cls)
            off = jnp.where(here, n_full * big + before, off)
            seen = seen + has
            before = before + has * size
        return cls, off

    def in_copy(row0, size, slot):
        rows = pl.ds(pl.multiple_of(row0, small), size)
        return pltpu.make_async_copy(x_hbm.at[rows], xbuf.at[slot, pl.ds(0, size)], isem.at[slot])

    def out_copy(row0, size, slot):
        rows = pl.ds(pl.multiple_of(row0, small), size)
        return pltpu.make_async_copy(obuf.at[slot, pl.ds(0, size)], o_hbm.at[rows, pl.ds(col0, tn)], osem.at[slot])

    def by_size(cls, fn):
        for c, size in enumerate(GROUP_SIZES):
            @pl.when(cls == c)
            def _():
                fn(size)

    def start_first_item(expert, slot):
        n_full, rem, n_items = items_of(expert)
        cls, _ = item(0, n_full, rem)

        @pl.when(n_items > 0)
        def _():
            by_size(cls, lambda size: in_copy(rs_ref[expert], size, slot).start())

    def wait_pending_out(slot):
        by_size(st[1 + slot], lambda size: out_copy(rs_ref[0], size, slot).wait())
        st[1 + slot] = -1

    @pl.when(first_step)
    def _():
        st[0] = 0
        st[1] = -1
        st[2] = -1
        start_first_item(jnp.where(pr_ref[0] > 0, 0, nx_ref[0]), 0)

    rs = rs_ref[e]
    n_full, rem, n_items = items_of(e)
    nxt = nx_ref[e]
    stream_continues = jnp.logical_not(jnp.logical_and(nxt <= e, j == nj - 1))

    def body(t, carry):
        g = st[0]
        slot = g % 2
        cls, off = item(t, n_full, rem)
        by_size(cls, lambda size: in_copy(rs + off, size, slot).wait())

        @pl.when(t + 1 < n_items)
        def _():
            cls1, off1 = item(t + 1, n_full, rem)
            by_size(cls1, lambda size: in_copy(rs + off1, size, 1 - slot).start())

        @pl.when(jnp.logical_and(t + 1 == n_items, stream_continues))
        def _():
            start_first_item(nxt, 1 - slot)

        wait_pending_out(slot)

        def run(size):
            compute(size, slot)
            out_copy(rs + off, size, slot).start()
        by_size(cls, run)
        st[1 + slot] = cls
        st[0] = g + 1
        return carry
    lax.fori_loop(0, n_items, body, 0)

    @pl.when(last_step)
    def _():
        wait_pending_out(0)
        wait_pending_out(1)

    @pl.when(e == E - 1)
    def _():
        zbuf[...] = jnp.zeros_like(zbuf)
        first = (rs + pr_ref[e]) // small
        last = o_hbm.shape[0] // small

        def fill_copy(b):
            rows = pl.ds(pl.multiple_of(b * small, small), small)
            return pltpu.make_async_copy(zbuf, o_hbm.at[rows, pl.ds(col0, tn)], zsem)

        def start_fill(b, c):
            fill_copy(b).start()
            return c

        def wait_fill(b, c):
            fill_copy(b).wait()
            return c
        lax.fori_loop(first, last, start_fill, 0)
        lax.fori_loop(first, last, wait_fill, 0)


def _next_nonempty(padded):
    E = padded.shape[0]
    ids = jnp.arange(E, dtype=jnp.int32)
    dist = jnp.where(padded[None, :] > 0, (ids[None, :] - ids[:, None] - 1) % E, E)
    return ((ids + 1 + jnp.min(dist, axis=1)) % E).astype(jnp.int32)


def _expert_scratch(big, width_in, dtype_in, tn, dtype_out):
    small = GROUP_SIZES[-1]
    return [pltpu.VMEM((2, big, width_in), dtype_in), pltpu.VMEM((2, big, tn), dtype_out),
            pltpu.VMEM((small, tn), dtype_out),
            pltpu.SemaphoreType.DMA((2,)), pltpu.SemaphoreType.DMA((2,)), pltpu.SemaphoreType.DMA,
            pltpu.SMEM((3,), jnp.int32)]


def _gate_up_kernel(rs_ref, pr_ref, nx_ref, xs_hbm, wg_ref, wu_ref, bg_ref, bu_ref, h_hbm,
                    xbuf, obuf, zbuf, isem, osem, zsem, st):
    tn = obuf.shape[2]

    def compute(size, slot):
        lo, hi = _unpack_bf16_pair(xbuf[slot, pl.ds(0, size), :])
        x = jnp.concatenate([lo, hi], axis=-1)
        gate = jnp.dot(x, wg_ref[...].astype(BF16), preferred_element_type=F32) + bg_ref[...]
        up = jnp.dot(x, wu_ref[...].astype(BF16), preferred_element_type=F32) + bu_ref[...]
        gate = jnp.minimum(gate, SWIGLU_LIMIT)
        up = jnp.clip(up, -SWIGLU_LIMIT, SWIGLU_LIMIT)
        obuf[slot, pl.ds(0, size), :] = ((up + 1.0) * gate * jax.nn.sigmoid(SWIGLU_ALPHA * gate)).astype(obuf.dtype)

    _expert_rows_pipeline(rs_ref, pr_ref, nx_ref, xs_hbm, h_hbm, xbuf, obuf, zbuf, isem, osem, zsem, st,
                          col0=pl.multiple_of(pl.program_id(0) * tn, tn), compute=compute)


def _gate_up(xs, row_start, row_count, w_gate_up, b_gate_up, *, rows, tn):
    E, D, F2 = w_gate_up.shape
    FF = F2 // 2
    nj = FF // tn
    W = xs.shape[1]
    grid_spec = pltpu.PrefetchScalarGridSpec(
        num_scalar_prefetch=3,
        grid=(nj, E),
        in_specs=[
            pl.BlockSpec(memory_space=pl.ANY),
            pl.BlockSpec((None, D, tn), lambda j, e, *_: (e, 0, j)),
            pl.BlockSpec((None, D, tn), lambda j, e, *_: (e, 0, nj + j)),
            pl.BlockSpec((None, 1, tn), lambda j, e, *_: (e, 0, j)),
            pl.BlockSpec((None, 1, tn), lambda j, e, *_: (e, 0, nj + j)),
        ],
        out_specs=pl.BlockSpec(memory_space=pl.ANY),
        scratch_shapes=_expert_scratch(GROUP_SIZES[0], W, xs.dtype, tn, BF16),
    )
    b3 = b_gate_up.reshape(E, 1, F2)
    return pl.pallas_call(
        _gate_up_kernel,
        out_shape=jax.ShapeDtypeStruct((rows, FF), BF16),
        grid_spec=grid_spec,
        compiler_params=_params(("arbitrary", "arbitrary")),
        name="gate_up",
    )(row_start, row_count, _next_nonempty(row_count), xs, w_gate_up, w_gate_up, b3, b3)


def _down_kernel(rs_ref, pr_ref, nx_ref, h_hbm, w_ref, b_ref, y_hbm, xbuf, obuf, zbuf, isem, osem, zsem, st):
    tn = obuf.shape[2]

    def compute(size, slot):
        hrows = xbuf[slot, pl.ds(0, size), :]
        y = jnp.dot(hrows, w_ref[...].astype(BF16), preferred_element_type=F32) + b_ref[...]
        obuf[slot, pl.ds(0, size), :] = _pack_bf16_pair(y[:, :tn], y[:, tn:])

    _expert_rows_pipeline(rs_ref, pr_ref, nx_ref, h_hbm, y_hbm, xbuf, obuf, zbuf, isem, osem, zsem, st,
                          col0=pl.multiple_of(pl.program_id(0) * tn, tn), compute=compute)


def _down(h, row_start, row_count, w_down, b_down, *, tn):
    E, FF, D = w_down.shape
    rows = h.shape[0]
    grid_spec = pltpu.PrefetchScalarGridSpec(
        num_scalar_prefetch=3,
        grid=(D // tn, E),
        in_specs=[
            pl.BlockSpec(memory_space=pl.ANY),
            pl.BlockSpec((None, FF, tn), lambda j, e, *_: (e, 0, j)),
            pl.BlockSpec((None, 1, tn), lambda j, e, *_: (e, 0, j)),
        ],
        out_specs=pl.BlockSpec(memory_space=pl.ANY),
        scratch_shapes=_expert_scratch(GROUP_SIZES[0], FF, h.dtype, tn // 2, jnp.uint32),
    )
    return pl.pallas_call(
        _down_kernel,
        out_shape=jax.ShapeDtypeStruct((rows, D // 2), jnp.uint32),
        grid_spec=grid_spec,
        compiler_params=_params(("arbitrary", "arbitrary")),
        name="down",
    )(row_start, row_count, _next_nonempty(row_count), h, w_down, b_down.reshape(E, 1, D))


def _combine_kernel(dest_ref, dnext_ref, hs_ref, gate_ref, nw_ref, y_hbm, o_ref, ybuf, sem, *, sub, half):
    s = pl.program_id(0)
    n = pl.num_programs(0)
    tm = hs_ref.shape[0]
    slot = s % 2

    def start_gather(d_ref, to_slot):
        def body(r, c):
            for k in range(TOP_K):
                pltpu.make_async_copy(y_hbm.at[pl.ds(d_ref[k, r], 1)], ybuf.at[to_slot, k, pl.ds(r, 1)],
                                      sem.at[to_slot]).start()
            return c
        lax.fori_loop(0, tm, body, 0)

    @pl.when(s == 0)
    def _():
        start_gather(dest_ref, 0)

    @pl.when(s + 1 < n)
    def _():
        start_gather(dnext_ref, 1 - slot)

    for k in range(TOP_K):
        pltpu.make_async_copy(y_hbm.at[pl.ds(0, tm)], ybuf.at[slot, k], sem.at[slot]).wait()

    D = hs_ref.shape[1]

    def body(r, c):
        rows = pl.ds(pl.multiple_of(r * sub, sub), sub)
        g = gate_ref[rows, :]
        ssq = jnp.zeros((sub, 1), F32)
        for j in range(D // (2 * half)):
            lo_cols = slice(2 * j * half, (2 * j + 1) * half)
            hi_cols = slice((2 * j + 1) * half, (2 * j + 2) * half)
            lo = hs_ref[rows, lo_cols]
            hi = hs_ref[rows, hi_cols]
            for k in range(TOP_K):
                word = ybuf[slot, k, rows, j * half:(j + 1) * half]
                gk = g[:, k:k + 1]
                lo = lo + lax.bitcast_convert_type(word << jnp.uint32(16), F32) * gk
                hi = hi + lax.bitcast_convert_type(word & jnp.uint32(0xFFFF0000), F32) * gk
            o_ref[rows, lo_cols] = lo
            o_ref[rows, hi_cols] = hi
            ssq = ssq + jnp.sum(lo * lo, axis=-1, keepdims=True) + jnp.sum(hi * hi, axis=-1, keepdims=True)
        o_ref[rows, :] = o_ref[rows, :] * lax.rsqrt(ssq / D + RMS_EPS) * nw_ref[...]
        return c
    lax.fori_loop(0, tm // sub, body, 0)


def _combine(hs1, gates_t, dest3, y, final_norm_w, *, B, LP, seq_len, tile, sub, half):
    T, D = hs1.shape
    W = y.shape[1]
    per_seq = LP // tile
    n_seq_tiles = seq_len // tile
    lead = (LP - seq_len) // tile
    n_steps = B * n_seq_tiles

    def tok(s):
        return (s // n_seq_tiles) * per_seq + lead + s % n_seq_tiles

    return pl.pallas_call(
        functools.partial(_combine_kernel, sub=sub, half=half),
        out_shape=jax.ShapeDtypeStruct((B, seq_len, D), F32),
        grid=(n_steps,),
        in_specs=[
            pl.BlockSpec((None, TOP_K, tile), lambda s: (tok(s), 0, 0), memory_space=pltpu.SMEM),
            pl.BlockSpec((None, TOP_K, tile), lambda s: (tok(jnp.minimum(s + 1, n_steps - 1)), 0, 0),
                         memory_space=pltpu.SMEM),
            pl.BlockSpec((tile, D), lambda s: (tok(s), 0)),
            pl.BlockSpec((tile, TOP_K), lambda s: (tok(s), 0)),
            pl.BlockSpec((1, D), lambda s: (0, 0)),
            pl.BlockSpec(memory_space=pl.ANY),
        ],
        out_specs=pl.BlockSpec((None, tile, D), lambda s: (s // n_seq_tiles, s % n_seq_tiles, 0)),
        scratch_shapes=[pltpu.VMEM((2, TOP_K, tile, W), y.dtype), pltpu.SemaphoreType.DMA((2,))],
        compiler_params=_params(("arbitrary",)),
        name="combine",
    )(dest3, dest3, hs1, gates_t, final_norm_w.reshape(1, D), y)


def _forward(x, meta_tokens, norm_mix_w, w_in, ret_log_rate_fwd, ret_log_rate_bwd, pool_w, pool_scale,
             w_out, norm_ffn_w, router_w, router_b, w_gate_up, b_gate_up, w_down, b_down, final_norm_w, tiles):
    B, S, D = x.shape
    depth = norm_mix_w.shape[0]
    assert depth == 1, "single-layer block"
    H = ret_log_rate_fwd.shape[-1]
    n_groups, cg = pool_w.shape[1], pool_w.shape[2]
    PW = n_groups * cg
    RW = w_out.shape[1] - PW
    d = RW // H
    E = router_w.shape[-1]
    assert w_in.shape[-1] == 4 * RW + PW and S % CHUNK == 0
    LP = PAD + N_META + S
    T = B * LP
    t = tiles

    hs0 = jnp.concatenate(
        [jnp.zeros((B, PAD, D), x.dtype), jnp.broadcast_to(meta_tokens.astype(x.dtype)[None], (B, N_META, D)), x],
        axis=1).reshape(T, D)

    proj = _in_proj(hs0, norm_mix_w[0], w_in[0].astype(BF16), tm=t["in_tm"], tn=t["in_tn"], sub=t["norm_rows"])
    pos = (jnp.arange(LP, dtype=jnp.int32) - PAD).astype(F32)
    inv_freq = ROPE_BASE ** (-jnp.arange(d // 2, dtype=F32) / (d // 2))
    ang = pos[:, None] * inv_freq[None, :]
    log_gamma = jnp.stack([-jnp.exp(ret_log_rate_fwd[0].astype(F32)), -jnp.exp(ret_log_rate_bwd[0].astype(F32))])
    ret = _retention(proj, log_gamma, jnp.cos(ang), jnp.sin(ang), B=B, LP=LP, H=H, d=d, GC=t["ret_group_chunks"],
                     HB=t["ret_heads_per_step"])
    pooled = _pool(proj, pool_w[0].astype(BF16), pool_scale[0].astype(F32), LP=LP, seq_len=N_META + S,
                   u_col0=4 * RW, tm=t["pool_tm"])
    hs1 = _out_proj(ret, pooled, w_out[0].astype(BF16), hs0, tm=t["out_tm"], tn=t["out_tn"])

    row_id = jnp.arange(T, dtype=jnp.int32)
    is_token = (row_id % LP) >= PAD
    fp, eid, gates, rank, cnt = _router(hs1, norm_ffn_w[0], router_w[0], router_b[0], is_token,
                                        tm=t["router_tm"], sub=t["norm_rows"])
    align = GROUP_SIZES[-1]
    tile = t["token_tile"]
    counts = cnt[:, 0].astype(jnp.int32)
    padded = (counts + align - 1) // align * align
    pend = jnp.cumsum(padded).astype(jnp.int32)
    pstart = pend - padded
    n_assign = B * (N_META + S) * TOP_K
    rows = -(-(n_assign + E * (align - 1)) // align) * align
    n_used = pend[-1:] // align
    spare = rows + jnp.arange(TOP_K, dtype=jnp.int32)[:, None] * tile + (row_id % tile)[None, :]
    expert_ids = jnp.arange(E, dtype=jnp.int32)[:, None, None]
    first_row = jnp.sum(jnp.where(eid[None] == expert_ids, pstart[:, None, None], 0), axis=0)
    dest = jnp.where(is_token[None, :], first_row + rank, spare)
    dest3 = dest.reshape(TOP_K, T // tile, tile).transpose(1, 0, 2)

    spare_rows = -(-(TOP_K * tile) // align) * align
    xs = _dispatch(fp, dest3, pend, padded, n_used, rows_total=rows + spare_rows, moe_tm=align, tile=tile)
    hmid = _gate_up(xs, pstart, padded, w_gate_up[0], b_gate_up[0], rows=rows, tn=t["gu_tn"])
    y = _down(hmid, pstart, padded, w_down[0], b_down[0], tn=t["down_tn"])
    return _combine(hs1, gates.T, dest3, y, final_norm_w, B=B, LP=LP, seq_len=S, tile=tile, sub=t["combine_rows"],
                    half=t["down_tn"] // 2)


def kernel(x, meta_tokens, norm_mix_w, w_in, ret_log_rate_fwd, ret_log_rate_bwd, pool_w, pool_scale, w_out,
           norm_ffn_w, router_w, router_b, w_gate_up, b_gate_up, w_down, b_down, final_norm_w):
    return _forward(x, meta_tokens, norm_mix_w, w_in, ret_log_rate_fwd, ret_log_rate_bwd, pool_w, pool_scale,
                    w_out, norm_ffn_w, router_w, router_b, w_gate_up, b_gate_up, w_down, b_down, final_norm_w,
                    DEFAULT_TILES)
```

```python
import functools

import jax
import jax.numpy as jnp
from jax import lax
from jax.experimental import pallas as pl
from jax.experimental.pallas import tpu as pltpu

N_META = 16
CHUNK = 128
PAD = CHUNK - N_META
POOL_WINDOWS = (2, 4, 8, 16)
POOL_HALO = 64
TOP_K = 4
GROUP_SIZES = (1024, 512, 256, 128)
SWIGLU_LIMIT = 7.0
SWIGLU_ALPHA = 1.702
RMS_EPS = 1e-5
ROPE_BASE = 10000.0

V7X_VMEM_BYTES = 64 * 1024 * 1024
VMEM_LIMIT = V7X_VMEM_BYTES - 2 * 1024 * 1024

F32 = jnp.float32
BF16 = jnp.bfloat16

DEFAULT_TILES = dict(
    in_tm=640, in_tn=1024, norm_rows=32,
    ret_group_chunks=13, ret_heads_per_step=2,
    pool_tm=640,
    out_tm=1280, out_tn=512,
    router_tm=640, router_rows=32,
    gu_tn=512, down_tn=2048,
    dispatch_tile=256, token_tile=128, combine_rows=32,
)


def _params(semantics):
    return pltpu.CompilerParams(dimension_semantics=semantics, vmem_limit_bytes=VMEM_LIMIT)


def _in_proj_kernel(x_ref, nw_ref, w_ref, o_ref, a_scr, *, sub):
    @pl.when(pl.program_id(1) == 0)
    def _():
        def body(r, c):
            rows = pl.ds(pl.multiple_of(r * sub, sub), sub)
            x = x_ref[rows, :]
            ms = jnp.mean(x * x, axis=-1, keepdims=True)
            a_scr[rows, :] = (x * lax.rsqrt(ms + RMS_EPS) * nw_ref[...]).astype(a_scr.dtype)
            return c
        lax.fori_loop(0, x_ref.shape[0] // sub, body, 0)

    o_ref[...] = jnp.dot(a_scr[...], w_ref[...], preferred_element_type=F32).astype(o_ref.dtype)


def _in_proj(hs, norm_w, w_in_bf16, *, tm, tn, sub):
    T, D = hs.shape
    N = w_in_bf16.shape[1]
    return pl.pallas_call(
        functools.partial(_in_proj_kernel, sub=sub),
        out_shape=jax.ShapeDtypeStruct((T, N), BF16),
        grid=(T // tm, N // tn),
        in_specs=[
            pl.BlockSpec((tm, D), lambda i, j: (i, 0)),
            pl.BlockSpec((1, D), lambda i, j: (0, 0)),
            pl.BlockSpec((D, tn), lambda i, j: (0, j)),
        ],
        out_specs=pl.BlockSpec((tm, tn), lambda i, j: (i, j)),
        scratch_shapes=[pltpu.VMEM((tm, D), BF16)],
        compiler_params=_params(("arbitrary", "arbitrary")),
        name="in_proj",
    )(hs, norm_w.reshape(1, D), w_in_bf16)


def _retention_kernel(lg_ref, q_ref, k_ref, v_ref, g_ref, cos_ref, sin_ref, o_ref,
                      acc_scr, sf_scr, sb_scr, dmat_scr, dvec_scr, *, G, GC):
    hb = pl.program_id(1)
    s = pl.program_id(2)
    phase = s // G
    w = s % G
    gi = w + phase * (G - 1 - 2 * w)
    C = CHUNK
    d = cos_ref.shape[1] * 2
    half = d // 2
    HB = q_ref.shape[1] // d
    GR = GC * C
    k_scale = d ** -0.5
    nt = (((1,), (1,)), ((), ()))
    tn = (((0,), (0,)), ((), ()))

    @pl.when(s == 0)
    def _():
        t = lax.broadcasted_iota(jnp.int32, (C, C), 0).astype(F32)
        u = lax.broadcasted_iota(jnp.int32, (C, C), 1).astype(F32)
        diff = t - u
        idx = lax.broadcasted_iota(jnp.int32, (C, d), 0).astype(F32)
        for hh in range(HB):
            lgf = lg_ref[0, hb * HB + hh]
            lgb = lg_ref[1, hb * HB + hh]
            dmat_scr[hh] = jnp.where(diff >= 0.0, jnp.exp(lgf * jnp.maximum(diff, 0.0)),
                                     jnp.exp(lgb * jnp.maximum(-diff, 0.0)))
            dvec_scr[hh, 0] = jnp.exp(lgf * (idx + 1.0))
            dvec_scr[hh, 1] = jnp.exp(lgf * (C - 1.0 - idx))
            dvec_scr[hh, 2] = jnp.exp(lgb * (C - idx))
            dvec_scr[hh, 3] = jnp.exp(lgb * idx)
            dvec_scr[hh, 4] = jnp.exp(jnp.full((C, d), lgf * C, F32))
            dvec_scr[hh, 5] = jnp.exp(jnp.full((C, d), lgb * C, F32))
        sf_scr[...] = jnp.zeros_like(sf_scr)

    @pl.when(s == G)
    def _():
        sb_scr[...] = jnp.zeros_like(sb_scr)

    def rotary(ref, rows, hh):
        x = ref[rows, hh * d:(hh + 1) * d].astype(F32)
        x1 = x[:, :half]
        x2 = x[:, half:]
        cs = cos_ref[rows, :]
        sn = sin_ref[rows, :]
        return jnp.concatenate([x1 * cs - x2 * sn, x2 * cs + x1 * sn], axis=-1)

    @pl.when(phase == 0)
    def _():
        def body(c, carry):
            rows = pl.ds(pl.multiple_of(c * C, C), C)
            arows = pl.ds(pl.multiple_of(gi * GR + c * C, C), C)
            for hh in range(HB):
                cols = slice(hh * d, (hh + 1) * d)
                qc = rotary(q_ref, rows, hh).astype(BF16)
                kf = rotary(k_ref, rows, hh) * k_scale
                vc = v_ref[rows, cols]
                sc = lax.dot_general(qc, kf.astype(BF16), nt, preferred_element_type=F32)
                p = (sc * dmat_scr[hh]).astype(BF16)
                inner = jnp.dot(p, vc, preferred_element_type=F32)
                sf = sf_scr[hh]
                cross = jnp.dot(qc, sf.astype(BF16), preferred_element_type=F32) * dvec_scr[hh, 0]
                acc_scr[arows, cols] = inner + cross
                kd = (kf * dvec_scr[hh, 1]).astype(BF16)
                sf_scr[hh] = sf * dvec_scr[hh, 4, 0:1, :] + lax.dot_general(kd, vc, tn, preferred_element_type=F32)
            return carry
        lax.fori_loop(0, GC, body, 0)

    @pl.when(phase == 1)
    def _():
        def body(cc, carry):
            c = GC - 1 - cc
            rows = pl.ds(pl.multiple_of(c * C, C), C)
            arows = pl.ds(pl.multiple_of(gi * GR + c * C, C), C)
            for hh in range(HB):
                cols = slice(hh * d, (hh + 1) * d)
                qc = rotary(q_ref, rows, hh).astype(BF16)
                kf = rotary(k_ref, rows, hh) * k_scale
                vc = v_ref[rows, cols]
                sb = sb_scr[hh]
                r = acc_scr[arows, cols] + jnp.dot(qc, sb.astype(BF16), preferred_element_type=F32) * dvec_scr[hh, 2]
                kd = (kf * dvec_scr[hh, 3]).astype(BF16)
                sb_scr[hh] = sb * dvec_scr[hh, 5, 0:1, :] + lax.dot_general(kd, vc, tn, preferred_element_type=F32)
                rn = r * lax.rsqrt(jnp.mean(r * r, axis=-1, keepdims=True) + RMS_EPS)
                gg = g_ref[rows, cols].astype(F32)
                o_ref[rows, cols] = (gg * jax.nn.sigmoid(gg) * rn).astype(o_ref.dtype)
            return carry
        lax.fori_loop(0, GC, body, 0)


def _retention(proj, log_gamma, cos_tab, sin_tab, *, B, LP, H, d, GC, HB):
    T = proj.shape[0]
    GR = GC * CHUNK
    G = LP // GR
    assert H % HB == 0
    H = H // HB
    bw = HB * d

    def grp(s):
        phase = s // G
        w = s % G
        return phase, w + phase * (G - 1 - 2 * w)

    def qkv_map(col0):
        def m(b, h, s):
            _, gi = grp(s)
            return (b * G + gi, col0 + h)
        return m

    def late_map(col0):
        def m(b, h, s):
            phase, gi = grp(s)
            return (b * G + jnp.where(phase == 0, G - 1, gi), col0 + h)
        return m

    def tab_map(b, h, s):
        _, gi = grp(s)
        return (gi, 0)

    return pl.pallas_call(
        functools.partial(_retention_kernel, G=G, GC=GC),
        out_shape=jax.ShapeDtypeStruct((T, H * bw), BF16),
        grid=(B, H, 2 * G),
        in_specs=[
            pl.BlockSpec(memory_space=pltpu.SMEM),
            pl.BlockSpec((GR, bw), qkv_map(0)),
            pl.BlockSpec((GR, bw), qkv_map(H)),
            pl.BlockSpec((GR, bw), qkv_map(2 * H)),
            pl.BlockSpec((GR, bw), late_map(3 * H)),
            pl.BlockSpec((GR, d // 2), tab_map),
            pl.BlockSpec((GR, d // 2), tab_map),
        ],
        out_specs=pl.BlockSpec((GR, bw), late_map(0)),
        scratch_shapes=[
            pltpu.VMEM((LP, bw), F32),
            pltpu.VMEM((HB, d, d), F32),
            pltpu.VMEM((HB, d, d), F32),
            pltpu.VMEM((HB, CHUNK, CHUNK), F32),
            pltpu.VMEM((HB, 6, CHUNK, d), F32),
        ],
        compiler_params=_params(("arbitrary", "arbitrary", "arbitrary")),
        name="retention",
    )(log_gamma, proj, proj, proj, proj, cos_tab, sin_tab)


def _pool_kernel(prev_ref, cur_ref, next_ref, pw_ref, ps_ref, o_ref, *, tiles_per_seq, seq_len):
    i = pl.program_id(0)
    tm = cur_ref.shape[0]
    ext = tm + 2 * POOL_HALO
    n_groups = pw_ref.shape[0]
    cg = pw_ref.shape[1]
    row0 = (i % tiles_per_seq) * tm - PAD
    pos_t = row0 + lax.broadcasted_iota(jnp.int32, (tm, ext), 0)
    pos_s = row0 - POOL_HALO + lax.broadcasted_iota(jnp.int32, (tm, ext), 1)
    pos_c = row0 + lax.broadcasted_iota(jnp.int32, (tm, 1), 0)
    for gi in range(n_groups):
        win = POOL_WINDOWS[gi]
        cols = slice(gi * cg, (gi + 1) * cg)
        lo = jnp.clip(pos_t - win // 2, 0, seq_len)
        hi = jnp.clip(pos_t + (win - win // 2), 0, seq_len)
        band = jnp.where(pos_s >= lo, jnp.where(pos_s < hi, 1.0, 0.0), 0.0).astype(BF16)
        cnt = (jnp.clip(pos_c + (win - win // 2), 0, seq_len) - jnp.clip(pos_c - win // 2, 0, seq_len)).astype(F32)
        inv = jnp.where(pos_c >= 0, 1.0 / jnp.maximum(cnt, 1.0), 0.0)
        u_cur = cur_ref[:, cols]
        u_ext = jnp.concatenate([prev_ref[:, cols], u_cur, next_ref[:, cols]], axis=0)
        wsum = jnp.dot(band, u_ext, preferred_element_type=F32)
        pooled = wsum * inv - u_cur.astype(F32)
        y = jnp.dot(pooled.astype(BF16), pw_ref[gi], preferred_element_type=F32)
        o_ref[:, cols] = (y * ps_ref[:, cols]).astype(o_ref.dtype)


def _pool(proj, pool_w_bf16, pool_scale, *, LP, seq_len, u_col0, tm):
    T = proj.shape[0]
    n_groups, cg, _ = pool_w_bf16.shape
    PW = n_groups * cg
    col_blk = u_col0 // PW
    hb = tm // POOL_HALO
    n_halo = T // POOL_HALO
    return pl.pallas_call(
        functools.partial(_pool_kernel, tiles_per_seq=LP // tm, seq_len=seq_len),
        out_shape=jax.ShapeDtypeStruct((T, PW), BF16),
        grid=(T // tm,),
        in_specs=[
            pl.BlockSpec((POOL_HALO, PW), lambda i: (jnp.maximum(i * hb - 1, 0), col_blk)),
            pl.BlockSpec((tm, PW), lambda i: (i, col_blk)),
            pl.BlockSpec((POOL_HALO, PW), lambda i: (jnp.minimum((i + 1) * hb, n_halo - 1), col_blk)),
            pl.BlockSpec((n_groups, cg, cg), lambda i: (0, 0, 0)),
            pl.BlockSpec((1, PW), lambda i: (0, 0)),
        ],
        out_specs=pl.BlockSpec((tm, PW), lambda i: (i, 0)),
        compiler_params=_params(("arbitrary",)),
        name="pool",
    )(proj, proj, proj, pool_w_bf16, pool_scale.reshape(1, PW))


def _out_proj_kernel(ret_ref, pool_ref, wr_ref, wp_ref, hs_ref, o_ref):
    y = jnp.dot(ret_ref[...], wr_ref[...], preferred_element_type=F32)
    y = y + jnp.dot(pool_ref[...], wp_ref[...], preferred_element_type=F32)
    o_ref[...] = hs_ref[...] + y


def _out_proj(ret, pool, w_out_bf16, hs, *, tm, tn):
    T, RW = ret.shape
    PW = pool.shape[1]
    D = hs.shape[1]
    assert RW == PW, "the two mixer halves share one row-block split of w_out"
    return pl.pallas_call(
        _out_proj_kernel,
        out_shape=jax.ShapeDtypeStruct((T, D), F32),
        grid=(T // tm, D // tn),
        in_specs=[
            pl.BlockSpec((tm, RW), lambda i, j: (i, 0)),
            pl.BlockSpec((tm, PW), lambda i, j: (i, 0)),
            pl.BlockSpec((RW, tn), lambda i, j: (0, j)),
            pl.BlockSpec((PW, tn), lambda i, j: (1, j)),
            pl.BlockSpec((tm, tn), lambda i, j: (i, j)),
        ],
        out_specs=pl.BlockSpec((tm, tn), lambda i, j: (i, j)),
        compiler_params=_params(("arbitrary", "arbitrary")),
        name="out_proj",
    )(ret, pool, w_out_bf16, w_out_bf16, hs)


def _pack_rounded_pair(lo, hi):
    lo_bits = lax.bitcast_convert_type(lo, jnp.uint32)
    hi_bits = lax.bitcast_convert_type(hi, jnp.uint32)
    return (hi_bits & jnp.uint32(0xFFFF0000)) | (lo_bits >> jnp.uint32(16))


def _pack_bf16_pair(lo, hi):
    return _pack_rounded_pair(lo.astype(BF16).astype(F32), hi.astype(BF16).astype(F32))


def _unpack_bf16_pair(word):
    lo = lax.bitcast_convert_type(word << jnp.uint32(16), F32).astype(BF16)
    hi = lax.bitcast_convert_type(word & jnp.uint32(0xFFFF0000), F32).astype(BF16)
    return lo, hi


def _router_kernel(hs_ref, nw_ref, rw_ref, rwhi_ref, rb_ref, valid_ref,
                   fp_ref, eid_ref, gate_ref, rank_ref, cnt_ref,
                   fhi_scr, flo_scr, carry_scr, *, sub):
    i = pl.program_id(0)
    tm, D = hs_ref.shape
    E = rb_ref.shape[0]
    half = D // 2

    @pl.when(i == 0)
    def _():
        carry_scr[...] = jnp.zeros_like(carry_scr)

    def body(r, c):
        rows = pl.ds(pl.multiple_of(r * sub, sub), sub)
        x = hs_ref[rows, :]
        ms = jnp.mean(x * x, axis=-1, keepdims=True)
        f = x * lax.rsqrt(ms + RMS_EPS) * nw_ref[...]
        f_hi = f.astype(BF16)
        f_rounded = f_hi.astype(F32)
        fhi_scr[rows, :] = f_hi
        flo_scr[rows, :] = (f - f_rounded).astype(BF16)
        fp_ref[rows, :] = _pack_rounded_pair(f_rounded[:, :half], f_rounded[:, half:])
        return c
    lax.fori_loop(0, tm // sub, body, 0)

    nt = (((1,), (1,)), ((), ()))
    part = lax.dot_general(rw_ref[...], fhi_scr[...], nt, preferred_element_type=F32)
    logits = part[:E] + part[E:] + lax.dot_general(rwhi_ref[...], flo_scr[...], nt, preferred_element_type=F32)
    logits = logits + rb_ref[...]

    eio = lax.broadcasted_iota(jnp.int32, (E, tm), 0).astype(F32)
    valid = valid_ref[...]

    vals, onehots = [], []
    l = logits
    for k in range(TOP_K):
        m = jnp.max(l, axis=0, keepdims=True)
        idx = jnp.min(jnp.where(l == m, eio, float(E)), axis=0, keepdims=True)
        sel = eio == idx
        vals.append(m)
        onehots.append(jnp.where(sel, valid, 0.0))
        eid_ref[pl.ds(k, 1), :] = idx.astype(jnp.int32)
        l = jnp.where(sel, -jnp.inf, l)

    exps = [jnp.exp(v - vals[0]) for v in vals]
    denom = exps[0] + exps[1] + exps[2] + exps[3]
    for k in range(TOP_K):
        gate_ref[pl.ds(k, 1), :] = exps[k] / denom

    oh = onehots[0] + onehots[1] + onehots[2] + onehots[3]
    tt = lax.broadcasted_iota(jnp.int32, (tm, tm), 0)
    uu = lax.broadcasted_iota(jnp.int32, (tm, tm), 1)
    upper = jnp.where(tt < uu, 1.0, 0.0).astype(BF16)
    carry = carry_scr[...]
    before = jnp.dot(oh.astype(BF16), upper, preferred_element_type=F32) + carry[:, 0:1]
    for k in range(TOP_K):
        rank_ref[pl.ds(k, 1), :] = jnp.sum(onehots[k] * before, axis=0, keepdims=True).astype(jnp.int32)
    carry = carry + jnp.sum(oh, axis=1, keepdims=True)
    carry_scr[...] = carry
    cnt_ref[...] = carry


def _router(hs1, norm_w, router_w, router_b, is_token, *, tm, sub):
    T, D = hs1.shape
    E = router_w.shape[1]
    rw_t = router_w.T.astype(F32)
    rw_hi = rw_t.astype(BF16)
    rw_lo = (rw_t - rw_hi.astype(F32)).astype(BF16)
    rw_stack = jnp.concatenate([rw_hi, rw_lo], axis=0)
    n = T // tm
    return pl.pallas_call(
        functools.partial(_router_kernel, sub=sub),
        out_shape=(
            jax.ShapeDtypeStruct((T, D // 2), jnp.uint32),
            jax.ShapeDtypeStruct((TOP_K, T), jnp.int32),
            jax.ShapeDtypeStruct((TOP_K, T), F32),
            jax.ShapeDtypeStruct((TOP_K, T), jnp.int32),
            jax.ShapeDtypeStruct((E, CHUNK), F32),
        ),
        grid=(n,),
        in_specs=[
            pl.BlockSpec((tm, D), lambda i: (i, 0)),
            pl.BlockSpec((1, D), lambda i: (0, 0)),
            pl.BlockSpec((2 * E, D), lambda i: (0, 0)),
            pl.BlockSpec((E, D), lambda i: (0, 0)),
            pl.BlockSpec((E, 1), lambda i: (0, 0)),
            pl.BlockSpec((1, tm), lambda i: (0, i)),
        ],
        out_specs=(
            pl.BlockSpec((tm, D // 2), lambda i: (i, 0)),
            pl.BlockSpec((TOP_K, tm), lambda i: (0, i)),
            pl.BlockSpec((TOP_K, tm), lambda i: (0, i)),
            pl.BlockSpec((TOP_K, tm), lambda i: (0, i)),
            pl.BlockSpec((E, CHUNK), lambda i: (0, 0)),
        ),
        scratch_shapes=[pltpu.VMEM((tm, D), BF16), pltpu.VMEM((tm, D), BF16), pltpu.VMEM((E, CHUNK), F32)],
        compiler_params=_params(("arbitrary",)),
        name="router",
    )(hs1, norm_w.reshape(1, D), rw_stack, rw_hi, router_b.reshape(E, 1).astype(F32),
      is_token.astype(F32).reshape(1, T))


def _dispatch_kernel(pend_ref, padded_ref, nu_ref, dest_ref, fp_ref, zeros_ref, xs_hbm, zsem, sem, *, moe_tm):
    i = pl.program_id(0)
    tm = fp_ref.shape[0]
    E = pend_ref.shape[0]
    n_blocks = xs_hbm.shape[0] // moe_tm

    def zero_rows(start):
        return pltpu.make_async_copy(zeros_ref, xs_hbm.at[pl.ds(pl.multiple_of(start, moe_tm), moe_tm)], zsem)

    @pl.when(i == 0)
    def _():
        for e in range(E):
            @pl.when(padded_ref[e] > 0)
            def _():
                zero_rows(pend_ref[e] - moe_tm).start()

        def start_tail(b, c):
            zero_rows(b * moe_tm).start()
            return c
        lax.fori_loop(nu_ref[0], n_blocks, start_tail, 0)
        for e in range(E):
            @pl.when(padded_ref[e] > 0)
            def _():
                zero_rows(pend_ref[e] - moe_tm).wait()

        def wait_tail(b, c):
            zero_rows(b * moe_tm).wait()
            return c
        lax.fori_loop(nu_ref[0], n_blocks, wait_tail, 0)

    def body(r, c):
        for k in range(TOP_K):
            pltpu.make_async_copy(fp_ref.at[pl.ds(r, 1)], xs_hbm.at[pl.ds(dest_ref[k, r], 1)], sem).start()
        return c
    lax.fori_loop(0, tm, body, 0)
    for k in range(TOP_K):
        pltpu.make_async_copy(fp_ref, xs_hbm.at[pl.ds(0, tm)], sem).wait()


def _dispatch(fp, dest3, pend, padded, n_used, *, rows_total, moe_tm, tile):
    T, W = fp.shape
    assert rows_total % moe_tm == 0
    zeros = jnp.zeros((moe_tm, W), fp.dtype)
    grid_spec = pltpu.PrefetchScalarGridSpec(
        num_scalar_prefetch=3,
        grid=(T // tile,),
        in_specs=[
            pl.BlockSpec((None, TOP_K, tile), lambda i, *_: (i, 0, 0), memory_space=pltpu.SMEM),
            pl.BlockSpec((tile, W), lambda i, *_: (i, 0)),
            pl.BlockSpec((moe_tm, W), lambda i, *_: (0, 0)),
        ],
        out_specs=pl.BlockSpec(memory_space=pl.ANY),
        scratch_shapes=[pltpu.SemaphoreType.DMA, pltpu.SemaphoreType.DMA],
    )
    return pl.pallas_call(
        functools.partial(_dispatch_kernel, moe_tm=moe_tm),
        out_shape=jax.ShapeDtypeStruct((rows_total, W), fp.dtype),
        grid_spec=grid_spec,
        compiler_params=_params(("arbitrary",)),
        name="dispatch",
    )(pend, padded, n_used, dest3, fp, zeros)


def _expert_rows_pipeline(rs_ref, pr_ref, nx_ref, x_hbm, o_hbm, xbuf, obuf, zbuf, isem, osem, zsem, st,
                          *, col0, compute):
    big, small = GROUP_SIZES[0], GROUP_SIZES[-1]
    j = pl.program_id(0)
    e = pl.program_id(1)
    nj = pl.num_programs(0)
    E = pl.num_programs(1)
    tn = obuf.shape[2]
    first_step = jnp.logical_and(j == 0, e == 0)
    last_step = jnp.logical_and(j == nj - 1, e == E - 1)

    def items_of(expert):
        pr = pr_ref[expert]
        n_full = pr // big
        rem = pr - n_full * big
        n_items = n_full
        for size in GROUP_SIZES[1:]:
            n_items = n_items + (rem // size) % 2
        return n_full, rem, n_items

    def item(t, n_full, rem):
        cls = jnp.int32(0)
        off = jnp.minimum(t, n_full) * big
        seen = jnp.int32(0)
        before = jnp.int32(0)
        for c, size in enumerate(GROUP_SIZES[1:], start=1):
            has = (rem // size) % 2
            here = jnp.logical_and(has == 1, t == n_full + seen)
            cls = jnp.where(here, c, cls)
            off = jnp.where(here, n_full * big + before, off)
            seen = seen + has
            before = before + has * size
        return cls, off

    def in_copy(row0, size, slot):
        rows = pl.ds(pl.multiple_of(row0, small), size)
        return pltpu.make_async_copy(x_hbm.at[rows], xbuf.at[slot, pl.ds(0, size)], isem.at[slot])

    def out_copy(row0, size, slot):
        rows = pl.ds(pl.multiple_of(row0, small), size)
        return pltpu.make_async_copy(obuf.at[slot, pl.ds(0, size)], o_hbm.at[rows, pl.ds(col0, tn)], osem.at[slot])

    def by_size(cls, fn):
        for c, size in enumerate(GROUP_SIZES):
            @pl.when(cls == c)
            def _():
                fn(size)

    def start_first_item(expert, slot):
        n_full, rem, n_items = items_of(expert)
        cls, _ = item(0, n_full, rem)

        @pl.when(n_items > 0)
        def _():
            by_size(cls, lambda size: in_copy(rs_ref[expert], size, slot).start())

    def wait_pending_out(slot):
        by_size(st[1 + slot], lambda size: out_copy(rs_ref[0], size, slot).wait())
        st[1 + slot] = -1

    @pl.when(first_step)
    def _():
        st[0] = 0
        st[1] = -1
        st[2] = -1
        start_first_item(jnp.where(pr_ref[0] > 0, 0, nx_ref[0]), 0)

    rs = rs_ref[e]
    n_full, rem, n_items = items_of(e)
    nxt = nx_ref[e]
    stream_continues = jnp.logical_not(jnp.logical_and(nxt <= e, j == nj - 1))

    def body(t, carry):
        g = st[0]
        slot = g % 2
        cls, off = item(t, n_full, rem)
        by_size(cls, lambda size: in_copy(rs + off, size, slot).wait())

        @pl.when(t + 1 < n_items)
        def _():
            cls1, off1 = item(t + 1, n_full, rem)
            by_size(cls1, lambda size: in_copy(rs + off1, size, 1 - slot).start())

        @pl.when(jnp.logical_and(t + 1 == n_items, stream_continues))
        def _():
            start_first_item(nxt, 1 - slot)

        wait_pending_out(slot)

        def run(size):
            compute(size, slot)
            out_copy(rs + off, size, slot).start()
        by_size(cls, run)
        st[1 + slot] = cls
        st[0] = g + 1
        return carry
    lax.fori_loop(0, n_items, body, 0)

    @pl.when(last_step)
    def _():
        wait_pending_out(0)
        wait_pending_out(1)

    @pl.when(e == E - 1)
    def _():
        zbuf[...] = jnp.zeros_like(zbuf)
        first = (rs + pr_ref[e]) // small
        last = o_hbm.shape[0] // small

        def fill_copy(b):
            rows = pl.ds(pl.multiple_of(b * small, small), small)
            return pltpu.make_async_copy(zbuf, o_hbm.at[rows, pl.ds(col0, tn)], zsem)

        def start_fill(b, c):
            fill_copy(b).start()
            return c

        def wait_fill(b, c):
            fill_copy(b).wait()
            return c
        lax.fori_loop(first, last, start_fill, 0)
        lax.fori_loop(first, last, wait_fill, 0)


def _next_nonempty(padded):
    E = padded.shape[0]
    ids = jnp.arange(E, dtype=jnp.int32)
    dist = jnp.where(padded[None, :] > 0, (ids[None, :] - ids[:, None] - 1) % E, E)
    return ((ids + 1 + jnp.min(dist, axis=1)) % E).astype(jnp.int32)


def _expert_scratch(big, width_in, dtype_in, tn, dtype_out):
    small = GROUP_SIZES[-1]
    return [pltpu.VMEM((2, big, width_in), dtype_in), pltpu.VMEM((2, big, tn), dtype_out),
            pltpu.VMEM((small, tn), dtype_out),
            pltpu.SemaphoreType.DMA((2,)), pltpu.SemaphoreType.DMA((2,)), pltpu.SemaphoreType.DMA,
            pltpu.SMEM((3,), jnp.int32)]


def _gate_up_kernel(rs_ref, pr_ref, nx_ref, xs_hbm, wg_ref, wu_ref, bg_ref, bu_ref, h_hbm,
                    xbuf, obuf, zbuf, isem, osem, zsem, st):
    tn = obuf.shape[2]

    def compute(size, slot):
        lo, hi = _unpack_bf16_pair(xbuf[slot, pl.ds(0, size), :])
        x = jnp.concatenate([lo, hi], axis=-1)
        gate = jnp.dot(x, wg_ref[...].astype(BF16), preferred_element_type=F32) + bg_ref[...]
        up = jnp.dot(x, wu_ref[...].astype(BF16), preferred_element_type=F32) + bu_ref[...]
        gate = jnp.minimum(gate, SWIGLU_LIMIT)
        up = jnp.clip(up, -SWIGLU_LIMIT, SWIGLU_LIMIT)
        obuf[slot, pl.ds(0, size), :] = ((up + 1.0) * gate * jax.nn.sigmoid(SWIGLU_ALPHA * gate)).astype(obuf.dtype)

    _expert_rows_pipeline(rs_ref, pr_ref, nx_ref, xs_hbm, h_hbm, xbuf, obuf, zbuf, isem, osem, zsem, st,
                          col0=pl.multiple_of(pl.program_id(0) * tn, tn), compute=compute)


def _gate_up(xs, row_start, row_count, w_gate_up, b_gate_up, *, rows, tn):
    E, D, F2 = w_gate_up.shape
    FF = F2 // 2
    nj = FF // tn
    W = xs.shape[1]
    grid_spec = pltpu.PrefetchScalarGridSpec(
        num_scalar_prefetch=3,
        grid=(nj, E),
        in_specs=[
            pl.BlockSpec(memory_space=pl.ANY),
            pl.BlockSpec((None, D, tn), lambda j, e, *_: (e, 0, j)),
            pl.BlockSpec((None, D, tn), lambda j, e, *_: (e, 0, nj + j)),
            pl.BlockSpec((None, 1, tn), lambda j, e, *_: (e, 0, j)),
            pl.BlockSpec((None, 1, tn), lambda j, e, *_: (e, 0, nj + j)),
        ],
        out_specs=pl.BlockSpec(memory_space=pl.ANY),
        scratch_shapes=_expert_scratch(GROUP_SIZES[0], W, xs.dtype, tn, BF16),
    )
    b3 = b_gate_up.reshape(E, 1, F2)
    return pl.pallas_call(
        _gate_up_kernel,
        out_shape=jax.ShapeDtypeStruct((rows, FF), BF16),
        grid_spec=grid_spec,
        compiler_params=_params(("arbitrary", "arbitrary")),
        name="gate_up",
    )(row_start, row_count, _next_nonempty(row_count), xs, w_gate_up, w_gate_up, b3, b3)


def _down_kernel(rs_ref, pr_ref, nx_ref, h_hbm, w_ref, b_ref, y_hbm, xbuf, obuf, zbuf, isem, osem, zsem, st):
    tn = obuf.shape[2]

    def compute(size, slot):
        hrows = xbuf[slot, pl.ds(0, size), :]
        y = jnp.dot(hrows, w_ref[...].astype(BF16), preferred_element_type=F32) + b_ref[...]
        obuf[slot, pl.ds(0, size), :] = _pack_bf16_pair(y[:, :tn], y[:, tn:])

    _expert_rows_pipeline(rs_ref, pr_ref, nx_ref, h_hbm, y_hbm, xbuf, obuf, zbuf, isem, osem, zsem, st,
                          col0=pl.multiple_of(pl.program_id(0) * tn, tn), compute=compute)


def _down(h, row_start, row_count, w_down, b_down, *, tn):
    E, FF, D = w_down.shape
    rows = h.shape[0]
    grid_spec = pltpu.PrefetchScalarGridSpec(
        num_scalar_prefetch=3,
        grid=(D // tn, E),
        in_specs=[
            pl.BlockSpec(memory_space=pl.ANY),
            pl.BlockSpec((None, FF, tn), lambda j, e, *_: (e, 0, j)),
            pl.BlockSpec((None, 1, tn), lambda j, e, *_: (e, 0, j)),
        ],
        out_specs=pl.BlockSpec(memory_space=pl.ANY),
        scratch_shapes=_expert_scratch(GROUP_SIZES[0], FF, h.dtype, tn // 2, jnp.uint32),
    )
    return pl.pallas_call(
        _down_kernel,
        out_shape=jax.ShapeDtypeStruct((rows, D // 2), jnp.uint32),
        grid_spec=grid_spec,
        compiler_params=_params(("arbitrary", "arbitrary")),
        name="down",
    )(row_start, row_count, _next_nonempty(row_count), h, w_down, b_down.reshape(E, 1, D))


def _combine_kernel(dest_ref, dnext_ref, hs_ref, gate_ref, nw_ref, y_hbm, o_ref, ybuf, sem, *, sub, half):
    s = pl.program_id(0)
    n = pl.num_programs(0)
    tm, D = hs_ref.shape
    W = ybuf.shape[3]
    slot = s % 2
    lanes = CHUNK
    n_chunks = W // lanes
    per_chunk = sub * TOP_K // n_chunks
    assert per_chunk * n_chunks == sub * TOP_K and half % lanes == 0

    def row_copy(d_ref, r, k, to_slot):
        return pltpu.make_async_copy(y_hbm.at[pl.ds(d_ref[k, r], 1)], ybuf.at[to_slot, k, pl.ds(r, 1)],
                                     sem.at[to_slot])

    def wait_slot(which):
        for k in range(TOP_K):
            pltpu.make_async_copy(y_hbm.at[pl.ds(0, tm)], ybuf.at[which, k], sem.at[which]).wait()

    @pl.when(s == 0)
    def _():
        def body(r, c):
            for k in range(TOP_K):
                row_copy(dest_ref, r, k, 0).start()
            return c
        lax.fori_loop(0, tm, body, 0)

    wait_slot(slot)

    def body(rb, c):
        row0 = pl.multiple_of(rb * sub, sub)
        rows = pl.ds(row0, sub)
        g = gate_ref[rows, :]
        gk = [jnp.broadcast_to(g[:, k:k + 1], (sub, lanes)) for k in range(TOP_K)]
        ssq = jnp.zeros((sub, lanes), F32)
        for ch in range(n_chunks):
            for i in range(ch * per_chunk, (ch + 1) * per_chunk):
                row_copy(dnext_ref, row0 + i // TOP_K, i % TOP_K, 1 - slot).start()
            j, within = divmod(ch * lanes, half)
            lo_cols = slice(2 * j * half + within, 2 * j * half + within + lanes)
            hi_cols = slice((2 * j + 1) * half + within, (2 * j + 1) * half + within + lanes)
            lo = hs_ref[rows, lo_cols]
            hi = hs_ref[rows, hi_cols]
            for k in range(TOP_K):
                word = ybuf[slot, k, rows, ch * lanes:(ch + 1) * lanes]
                lo = lo + lax.bitcast_convert_type(word << jnp.uint32(16), F32) * gk[k]
                hi = hi + lax.bitcast_convert_type(word & jnp.uint32(0xFFFF0000), F32) * gk[k]
            o_ref[rows, lo_cols] = lo
            o_ref[rows, hi_cols] = hi
            ssq = ssq + lo * lo + hi * hi
        ms = jnp.sum(ssq, axis=-1, keepdims=True) / D
        o_ref[rows, :] = o_ref[rows, :] * lax.rsqrt(ms + RMS_EPS) * nw_ref[...]
        return c
    lax.fori_loop(0, tm // sub, body, 0)

    @pl.when(s == n - 1)
    def _():
        wait_slot(1 - slot)


def _combine(hs1, gates_t, dest3, y, final_norm_w, *, B, LP, seq_len, tile, sub, half):
    T, D = hs1.shape
    W = y.shape[1]
    per_seq = LP // tile
    n_seq_tiles = seq_len // tile
    lead = (LP - seq_len) // tile
    n_steps = B * n_seq_tiles

    def tok(s):
        return (s // n_seq_tiles) * per_seq + lead + s % n_seq_tiles

    return pl.pallas_call(
        functools.partial(_combine_kernel, sub=sub, half=half),
        out_shape=jax.ShapeDtypeStruct((B, seq_len, D), F32),
        grid=(n_steps,),
        in_specs=[
            pl.BlockSpec((None, TOP_K, tile), lambda s: (tok(s), 0, 0), memory_space=pltpu.SMEM),
            pl.BlockSpec((None, TOP_K, tile), lambda s: (tok(jnp.minimum(s + 1, n_steps - 1)), 0, 0),
                         memory_space=pltpu.SMEM),
            pl.BlockSpec((tile, D), lambda s: (tok(s), 0)),
            pl.BlockSpec((tile, TOP_K), lambda s: (tok(s), 0)),
            pl.BlockSpec((1, D), lambda s: (0, 0)),
            pl.BlockSpec(memory_space=pl.ANY),
        ],
        out_specs=pl.BlockSpec((None, tile, D), lambda s: (s // n_seq_tiles, s % n_seq_tiles, 0)),
        scratch_shapes=[pltpu.VMEM((2, TOP_K, tile, W), y.dtype), pltpu.SemaphoreType.DMA((2,))],
        compiler_params=_params(("arbitrary",)),
        name="combine",
    )(dest3, dest3, hs1, gates_t, final_norm_w.reshape(1, D), y)


def _forward(x, meta_tokens, norm_mix_w, w_in, ret_log_rate_fwd, ret_log_rate_bwd, pool_w, pool_scale,
             w_out, norm_ffn_w, router_w, router_b, w_gate_up, b_gate_up, w_down, b_down, final_norm_w, tiles):
    B, S, D = x.shape
    depth = norm_mix_w.shape[0]
    assert depth == 1, "single-layer block"
    H = ret_log_rate_fwd.shape[-1]
    n_groups, cg = pool_w.shape[1], pool_w.shape[2]
    PW = n_groups * cg
    RW = w_out.shape[1] - PW
    d = RW // H
    E = router_w.shape[-1]
    assert w_in.shape[-1] == 4 * RW + PW and S % CHUNK == 0
    LP = PAD + N_META + S
    T = B * LP
    t = tiles

    hs0 = jnp.concatenate(
        [jnp.zeros((B, PAD, D), x.dtype), jnp.broadcast_to(meta_tokens.astype(x.dtype)[None], (B, N_META, D)), x],
        axis=1).reshape(T, D)

    proj = _in_proj(hs0, norm_mix_w[0], w_in[0].astype(BF16), tm=t["in_tm"], tn=t["in_tn"], sub=t["norm_rows"])
    pos = (jnp.arange(LP, dtype=jnp.int32) - PAD).astype(F32)
    inv_freq = ROPE_BASE ** (-jnp.arange(d // 2, dtype=F32) / (d // 2))
    ang = pos[:, None] * inv_freq[None, :]
    log_gamma = jnp.stack([-jnp.exp(ret_log_rate_fwd[0].astype(F32)), -jnp.exp(ret_log_rate_bwd[0].astype(F32))])
    ret = _retention(proj, log_gamma, jnp.cos(ang), jnp.sin(ang), B=B, LP=LP, H=H, d=d, GC=t["ret_group_chunks"],
                     HB=t["ret_heads_per_step"])
    pooled = _pool(proj, pool_w[0].astype(BF16), pool_scale[0].astype(F32), LP=LP, seq_len=N_META + S,
                   u_col0=4 * RW, tm=t["pool_tm"])
    hs1 = _out_proj(ret, pooled, w_out[0].astype(BF16), hs0, tm=t["out_tm"], tn=t["out_tn"])

    row_id = jnp.arange(T, dtype=jnp.int32)
    is_token = (row_id % LP) >= PAD
    fp, eid, gates, rank, cnt = _router(hs1, norm_ffn_w[0], router_w[0], router_b[0], is_token,
                                        tm=t["router_tm"], sub=t["router_rows"])
    align = GROUP_SIZES[-1]
    tile = t["token_tile"]
    dtile = t["dispatch_tile"]
    counts = cnt[:, 0].astype(jnp.int32)
    padded = (counts + align - 1) // align * align
    pend = jnp.cumsum(padded).astype(jnp.int32)
    pstart = pend - padded
    n_assign = B * (N_META + S) * TOP_K
    rows = -(-(n_assign + E * (align - 1)) // align) * align
    n_used = pend[-1:] // align
    spare = rows + jnp.arange(TOP_K, dtype=jnp.int32)[:, None] * dtile + (row_id % dtile)[None, :]
    expert_ids = jnp.arange(E, dtype=jnp.int32)[:, None, None]
    first_row = jnp.sum(jnp.where(eid[None] == expert_ids, pstart[:, None, None], 0), axis=0)
    dest = jnp.where(is_token[None, :], first_row + rank, spare)

    def by_tile(width):
        return dest.reshape(TOP_K, T // width, width).transpose(1, 0, 2)
    dest3 = by_tile(tile)

    spare_rows = -(-(TOP_K * dtile) // align) * align
    xs = _dispatch(fp, by_tile(dtile), pend, padded, n_used, rows_total=rows + spare_rows, moe_tm=align, tile=dtile)
    hmid = _gate_up(xs, pstart, padded, w_gate_up[0], b_gate_up[0], rows=rows, tn=t["gu_tn"])
    y = _down(hmid, pstart, padded, w_down[0], b_down[0], tn=t["down_tn"])
    return _combine(hs1, gates.T, dest3, y, final_norm_w, B=B, LP=LP, seq_len=S, tile=tile, sub=t["combine_rows"],
                    half=t["down_tn"] // 2)


def kernel(x, meta_tokens, norm_mix_w, w_in, ret_log_rate_fwd, ret_log_rate_bwd, pool_w, pool_scale, w_out,
           norm_ffn_w, router_w, router_b, w_gate_up, b_gate_up, w_down, b_down, final_norm_w):
    return _forward(x, meta_tokens, norm_mix_w, w_in, ret_log_rate_fwd, ret_log_rate_bwd, pool_w, pool_scale,
                    w_out, norm_ffn_w, router_w, router_b, w_gate_up, b_gate_up, w_down, b_down, final_norm_w,
                    DEFAULT_TILES)
```

```python
import functools

import jax
import jax.numpy as jnp
from jax import lax
from jax.experimental import pallas as pl
from jax.experimental.pallas import tpu as pltpu

N_META = 16
CHUNK = 128
PAD = CHUNK - N_META
POOL_WINDOWS = (2, 4, 8, 16)
POOL_HALO = 64
TOP_K = 4
GROUP_SIZES = (1024, 512, 256, 128)
SWIGLU_LIMIT = 7.0
SWIGLU_ALPHA = 1.702
RMS_EPS = 1e-5
ROPE_BASE = 10000.0

V7X_VMEM_BYTES = 64 * 1024 * 1024
VMEM_LIMIT = V7X_VMEM_BYTES - 2 * 1024 * 1024

F32 = jnp.float32
BF16 = jnp.bfloat16

DEFAULT_TILES = dict(
    prep_chunks=5, in_tm=1280, in_tn=1024, norm_rows=32,
    ret_group_chunks=5, ret_heads_per_step=2,
    pool_tm=640,
    out_tm=1280, out_tn=512,
    router_tm=640, router_rows=32,
    gu_tn=512, down_tn=2048,
    dispatch_tile=256, token_tile=128, combine_rows=32,
)


def _params(semantics):
    return pltpu.CompilerParams(dimension_semantics=semantics, vmem_limit_bytes=VMEM_LIMIT)


def _prep_kernel(*refs, sub, n_chunks, chunks_per_seq):
    x_refs = refs[:n_chunks]
    lead_ref, nw_ref, hs_ref, a_ref = refs[n_chunks:]
    i = pl.program_id(0)
    for c in range(n_chunks):
        is_lead = (i * n_chunks + c) % chunks_per_seq == 0

        def body(r, carry, c=c, is_lead=is_lead):
            rows = pl.ds(pl.multiple_of(r * sub, sub), sub)
            out_rows = pl.ds(pl.multiple_of(c * CHUNK + r * sub, sub), sub)
            x = jnp.where(is_lead, lead_ref[rows, :], x_refs[c][rows, :])
            hs_ref[out_rows, :] = x
            ms = jnp.mean(x * x, axis=-1, keepdims=True)
            a_ref[out_rows, :] = (x * lax.rsqrt(ms + RMS_EPS) * nw_ref[...]).astype(a_ref.dtype)
            return carry
        lax.fori_loop(0, CHUNK // sub, body, 0)


def _prep(x, lead, norm_w, *, n_chunks, sub):
    B, S, D = x.shape
    x_chunks_per_seq = S // CHUNK
    chunks_per_seq = x_chunks_per_seq + 1
    T = B * chunks_per_seq * CHUNK
    assert (B * chunks_per_seq) % n_chunks == 0

    def x_map(c):
        def m(i):
            q = i * n_chunks + c
            return ((q // chunks_per_seq) * x_chunks_per_seq + jnp.maximum(q % chunks_per_seq - 1, 0), 0)
        return m

    return pl.pallas_call(
        functools.partial(_prep_kernel, sub=sub, n_chunks=n_chunks, chunks_per_seq=chunks_per_seq),
        out_shape=(jax.ShapeDtypeStruct((T, D), x.dtype), jax.ShapeDtypeStruct((T, D), BF16)),
        grid=(B * chunks_per_seq // n_chunks,),
        in_specs=[pl.BlockSpec((CHUNK, D), x_map(c)) for c in range(n_chunks)] + [
            pl.BlockSpec((CHUNK, D), lambda i: (0, 0)),
            pl.BlockSpec((1, D), lambda i: (0, 0)),
        ],
        out_specs=(pl.BlockSpec((n_chunks * CHUNK, D), lambda i: (i, 0)),
                   pl.BlockSpec((n_chunks * CHUNK, D), lambda i: (i, 0))),
        compiler_params=_params(("arbitrary",)),
        name="prep",
    )(*([x.reshape(B * S, D)] * n_chunks), lead, norm_w.reshape(1, D))


def _in_proj_kernel(a_ref, w_ref, o_ref):
    o_ref[...] = jnp.dot(a_ref[...], w_ref[...], preferred_element_type=F32).astype(o_ref.dtype)


def _in_proj(a, w_in_bf16, *, tm, tn):
    T, D = a.shape
    N = w_in_bf16.shape[1]
    return pl.pallas_call(
        _in_proj_kernel,
        out_shape=jax.ShapeDtypeStruct((T, N), BF16),
        grid=(T // tm, N // tn),
        in_specs=[
            pl.BlockSpec((tm, D), lambda i, j: (i, 0)),
            pl.BlockSpec((D, tn), lambda i, j: (0, j)),
        ],
        out_specs=pl.BlockSpec((tm, tn), lambda i, j: (i, j)),
        compiler_params=_params(("arbitrary", "arbitrary")),
        name="in_proj",
    )(a, w_in_bf16)


def _retention_kernel(lg_ref, q_ref, k_ref, v_ref, g_ref, cos_ref, sin_ref, o_ref,
                      acc_scr, sf_scr, sb_scr, dmat_scr, dvec_scr, *, G, GC):
    hb = pl.program_id(0)
    s = pl.program_id(1)
    phase = s // G
    w = s % G
    gi = w + phase * (G - 1 - 2 * w)
    C = CHUNK
    d = cos_ref.shape[1] * 2
    half = d // 2
    NB = q_ref.shape[0]
    HB = q_ref.shape[2] // d
    GR = GC * C
    k_scale = d ** -0.5
    nt = (((1,), (1,)), ((), ()))
    tn = (((0,), (0,)), ((), ()))

    @pl.when(s == 0)
    def _():
        t = lax.broadcasted_iota(jnp.int32, (C, C), 0).astype(F32)
        u = lax.broadcasted_iota(jnp.int32, (C, C), 1).astype(F32)
        diff = t - u
        idx = lax.broadcasted_iota(jnp.int32, (C, d), 0).astype(F32)
        for hh in range(HB):
            lgf = lg_ref[0, hb * HB + hh]
            lgb = lg_ref[1, hb * HB + hh]
            dmat_scr[hh] = jnp.where(diff >= 0.0, jnp.exp(lgf * jnp.maximum(diff, 0.0)),
                                     jnp.exp(lgb * jnp.maximum(-diff, 0.0)))
            dvec_scr[hh, 0] = jnp.exp(lgf * (idx + 1.0))
            dvec_scr[hh, 1] = jnp.exp(lgf * (C - 1.0 - idx))
            dvec_scr[hh, 2] = jnp.exp(lgb * (C - idx))
            dvec_scr[hh, 3] = jnp.exp(lgb * idx)
            dvec_scr[hh, 4] = jnp.exp(jnp.full((C, d), lgf * C, F32))
            dvec_scr[hh, 5] = jnp.exp(jnp.full((C, d), lgb * C, F32))
        sf_scr[...] = jnp.zeros_like(sf_scr)

    @pl.when(s == G)
    def _():
        sb_scr[...] = jnp.zeros_like(sb_scr)

    def rotary(ref, bb, rows, hh):
        x = ref[bb, rows, hh * d:(hh + 1) * d].astype(F32)
        x1 = x[:, :half]
        x2 = x[:, half:]
        cs = cos_ref[rows, :]
        sn = sin_ref[rows, :]
        return jnp.concatenate([x1 * cs - x2 * sn, x2 * cs + x1 * sn], axis=-1)

    chains = [(bb, hh) for bb in range(NB) for hh in range(HB)]

    @pl.when(phase == 0)
    def _():
        def body(c, carry):
            rows = pl.ds(pl.multiple_of(c * C, C), C)
            arows = pl.ds(pl.multiple_of(gi * GR + c * C, C), C)
            for ci, (bb, hh) in enumerate(chains):
                cols = slice(hh * d, (hh + 1) * d)
                qc = rotary(q_ref, bb, rows, hh).astype(BF16)
                kf = rotary(k_ref, bb, rows, hh) * k_scale
                vc = v_ref[bb, rows, cols]
                sc = lax.dot_general(qc, kf.astype(BF16), nt, preferred_element_type=F32)
                p = (sc * dmat_scr[hh]).astype(BF16)
                inner = jnp.dot(p, vc, preferred_element_type=F32)
                sf = sf_scr[ci]
                cross = jnp.dot(qc, sf.astype(BF16), preferred_element_type=F32) * dvec_scr[hh, 0]
                acc_scr[bb, arows, cols] = inner + cross
                kd = (kf * dvec_scr[hh, 1]).astype(BF16)
                sf_scr[ci] = sf * dvec_scr[hh, 4, 0:1, :] + lax.dot_general(kd, vc, tn, preferred_element_type=F32)
            return carry
        lax.fori_loop(0, GC, body, 0)

    @pl.when(phase == 1)
    def _():
        def body(cc, carry):
            c = GC - 1 - cc
            rows = pl.ds(pl.multiple_of(c * C, C), C)
            arows = pl.ds(pl.multiple_of(gi * GR + c * C, C), C)
            for ci, (bb, hh) in enumerate(chains):
                cols = slice(hh * d, (hh + 1) * d)
                qc = rotary(q_ref, bb, rows, hh).astype(BF16)
                kf = rotary(k_ref, bb, rows, hh) * k_scale
                vc = v_ref[bb, rows, cols]
                sb = sb_scr[ci]
                r = acc_scr[bb, arows, cols] + jnp.dot(qc, sb.astype(BF16), preferred_element_type=F32) * dvec_scr[hh, 2]
                kd = (kf * dvec_scr[hh, 3]).astype(BF16)
                sb_scr[ci] = sb * dvec_scr[hh, 5, 0:1, :] + lax.dot_general(kd, vc, tn, preferred_element_type=F32)
                rn = r * lax.rsqrt(jnp.mean(r * r, axis=-1, keepdims=True) + RMS_EPS)
                gg = g_ref[bb, rows, cols].astype(F32)
                o_ref[bb, rows, cols] = (gg * jax.nn.sigmoid(gg) * rn).astype(o_ref.dtype)
            return carry
        lax.fori_loop(0, GC, body, 0)


def _retention(proj, log_gamma, cos_tab, sin_tab, *, B, LP, H, d, GC, HB):
    T, N = proj.shape
    GR = GC * CHUNK
    G = LP // GR
    assert H % HB == 0
    H = H // HB
    bw = HB * d
    proj3 = proj.reshape(B, LP, N)

    def grp(s):
        phase = s // G
        w = s % G
        return phase, w + phase * (G - 1 - 2 * w)

    def qkv_map(col0):
        def m(h, s):
            _, gi = grp(s)
            return (0, gi, col0 + h)
        return m

    def late_map(col0):
        def m(h, s):
            phase, gi = grp(s)
            return (0, jnp.where(phase == 0, G - 1, gi), col0 + h)
        return m

    def tab_map(h, s):
        _, gi = grp(s)
        return (gi, 0)

    ret = pl.pallas_call(
        functools.partial(_retention_kernel, G=G, GC=GC),
        out_shape=jax.ShapeDtypeStruct((B, LP, H * bw), BF16),
        grid=(H, 2 * G),
        in_specs=[
            pl.BlockSpec(memory_space=pltpu.SMEM),
            pl.BlockSpec((B, GR, bw), qkv_map(0)),
            pl.BlockSpec((B, GR, bw), qkv_map(H)),
            pl.BlockSpec((B, GR, bw), qkv_map(2 * H)),
            pl.BlockSpec((B, GR, bw), late_map(3 * H)),
            pl.BlockSpec((GR, d // 2), tab_map),
            pl.BlockSpec((GR, d // 2), tab_map),
        ],
        out_specs=pl.BlockSpec((B, GR, bw), late_map(0)),
        scratch_shapes=[
            pltpu.VMEM((B, LP, bw), F32),
            pltpu.VMEM((B * HB, d, d), F32),
            pltpu.VMEM((B * HB, d, d), F32),
            pltpu.VMEM((HB, CHUNK, CHUNK), F32),
            pltpu.VMEM((HB, 6, CHUNK, d), F32),
        ],
        compiler_params=_params(("arbitrary", "arbitrary")),
        name="retention",
    )(log_gamma, proj3, proj3, proj3, proj3, cos_tab, sin_tab)
    return ret.reshape(T, H * bw)


def _pool_kernel(prev_ref, cur_ref, next_ref, pw_ref, ps_ref, o_ref, *, tiles_per_seq, seq_len):
    i = pl.program_id(0)
    tm = cur_ref.shape[0]
    ext = tm + 2 * POOL_HALO
    n_groups = pw_ref.shape[0]
    cg = pw_ref.shape[1]
    row0 = (i % tiles_per_seq) * tm - PAD
    pos_t = row0 + lax.broadcasted_iota(jnp.int32, (tm, ext), 0)
    pos_s = row0 - POOL_HALO + lax.broadcasted_iota(jnp.int32, (tm, ext), 1)
    pos_c = row0 + lax.broadcasted_iota(jnp.int32, (tm, 1), 0)
    for gi in range(n_groups):
        win = POOL_WINDOWS[gi]
        cols = slice(gi * cg, (gi + 1) * cg)
        lo = jnp.clip(pos_t - win // 2, 0, seq_len)
        hi = jnp.clip(pos_t + (win - win // 2), 0, seq_len)
        band = jnp.where(pos_s >= lo, jnp.where(pos_s < hi, 1.0, 0.0), 0.0).astype(BF16)
        cnt = (jnp.clip(pos_c + (win - win // 2), 0, seq_len) - jnp.clip(pos_c - win // 2, 0, seq_len)).astype(F32)
        inv = jnp.where(pos_c >= 0, 1.0 / jnp.maximum(cnt, 1.0), 0.0)
        u_cur = cur_ref[:, cols]
        u_ext = jnp.concatenate([prev_ref[:, cols], u_cur, next_ref[:, cols]], axis=0)
        wsum = jnp.dot(band, u_ext, preferred_element_type=F32)
        pooled = wsum * inv - u_cur.astype(F32)
        y = jnp.dot(pooled.astype(BF16), pw_ref[gi], preferred_element_type=F32)
        o_ref[:, cols] = (y * ps_ref[:, cols]).astype(o_ref.dtype)


def _pool(proj, pool_w_bf16, pool_scale, *, LP, seq_len, u_col0, tm):
    T = proj.shape[0]
    n_groups, cg, _ = pool_w_bf16.shape
    PW = n_groups * cg
    col_blk = u_col0 // PW
    hb = tm // POOL_HALO
    n_halo = T // POOL_HALO
    return pl.pallas_call(
        functools.partial(_pool_kernel, tiles_per_seq=LP // tm, seq_len=seq_len),
        out_shape=jax.ShapeDtypeStruct((T, PW), BF16),
        grid=(T // tm,),
        in_specs=[
            pl.BlockSpec((POOL_HALO, PW), lambda i: (jnp.maximum(i * hb - 1, 0), col_blk)),
            pl.BlockSpec((tm, PW), lambda i: (i, col_blk)),
            pl.BlockSpec((POOL_HALO, PW), lambda i: (jnp.minimum((i + 1) * hb, n_halo - 1), col_blk)),
            pl.BlockSpec((n_groups, cg, cg), lambda i: (0, 0, 0)),
            pl.BlockSpec((1, PW), lambda i: (0, 0)),
        ],
        out_specs=pl.BlockSpec((tm, PW), lambda i: (i, 0)),
        compiler_params=_params(("arbitrary",)),
        name="pool",
    )(proj, proj, proj, pool_w_bf16, pool_scale.reshape(1, PW))


def _out_proj_kernel(ret_ref, pool_ref, wr_ref, wp_ref, hs_ref, o_ref):
    y = jnp.dot(ret_ref[...], wr_ref[...], preferred_element_type=F32)
    y = y + jnp.dot(pool_ref[...], wp_ref[...], preferred_element_type=F32)
    o_ref[...] = hs_ref[...] + y


def _out_proj(ret, pool, w_out_bf16, hs, *, tm, tn):
    T, RW = ret.shape
    PW = pool.shape[1]
    D = hs.shape[1]
    assert RW == PW, "the two mixer halves share one row-block split of w_out"
    return pl.pallas_call(
        _out_proj_kernel,
        out_shape=jax.ShapeDtypeStruct((T, D), F32),
        grid=(T // tm, D // tn),
        in_specs=[
            pl.BlockSpec((tm, RW), lambda i, j: (i, 0)),
            pl.BlockSpec((tm, PW), lambda i, j: (i, 0)),
            pl.BlockSpec((RW, tn), lambda i, j: (0, j)),
            pl.BlockSpec((PW, tn), lambda i, j: (1, j)),
            pl.BlockSpec((tm, tn), lambda i, j: (i, j)),
        ],
        out_specs=pl.BlockSpec((tm, tn), lambda i, j: (i, j)),
        compiler_params=_params(("arbitrary", "arbitrary")),
        name="out_proj",
    )(ret, pool, w_out_bf16, w_out_bf16, hs)


def _pack_rounded_pair(lo, hi):
    lo_bits = lax.bitcast_convert_type(lo, jnp.uint32)
    hi_bits = lax.bitcast_convert_type(hi, jnp.uint32)
    return (hi_bits & jnp.uint32(0xFFFF0000)) | (lo_bits >> jnp.uint32(16))


def _pack_bf16_pair(lo, hi):
    return _pack_rounded_pair(lo.astype(BF16).astype(F32), hi.astype(BF16).astype(F32))


def _unpack_bf16_pair(word):
    lo = lax.bitcast_convert_type(word << jnp.uint32(16), F32).astype(BF16)
    hi = lax.bitcast_convert_type(word & jnp.uint32(0xFFFF0000), F32).astype(BF16)
    return lo, hi


def _router_kernel(hs_ref, nw_ref, rw_ref, rwhi_ref, rb_ref, valid_ref,
                   fp_ref, eid_ref, gate_ref, rank_ref, cnt_ref,
                   fhi_scr, flo_scr, carry_scr, *, sub):
    i = pl.program_id(0)
    tm, D = hs_ref.shape
    E = rb_ref.shape[0]
    half = D // 2

    @pl.when(i == 0)
    def _():
        carry_scr[...] = jnp.zeros_like(carry_scr)

    def body(r, c):
        rows = pl.ds(pl.multiple_of(r * sub, sub), sub)
        x = hs_ref[rows, :]
        ms = jnp.mean(x * x, axis=-1, keepdims=True)
        f = x * lax.rsqrt(ms + RMS_EPS) * nw_ref[...]
        f_hi = f.astype(BF16)
        f_rounded = f_hi.astype(F32)
        fhi_scr[rows, :] = f_hi
        flo_scr[rows, :] = (f - f_rounded).astype(BF16)
        fp_ref[rows, :] = _pack_rounded_pair(f_rounded[:, :half], f_rounded[:, half:])
        return c
    lax.fori_loop(0, tm // sub, body, 0)

    nt = (((1,), (1,)), ((), ()))
    part = lax.dot_general(rw_ref[...], fhi_scr[...], nt, preferred_element_type=F32)
    logits = part[:E] + part[E:] + lax.dot_general(rwhi_ref[...], flo_scr[...], nt, preferred_element_type=F32)
    logits = logits + rb_ref[...]

    eio = lax.broadcasted_iota(jnp.int32, (E, tm), 0).astype(F32)
    valid = valid_ref[...]

    vals, onehots = [], []
    l = logits
    for k in range(TOP_K):
        m = jnp.max(l, axis=0, keepdims=True)
        idx = jnp.min(jnp.where(l == m, eio, float(E)), axis=0, keepdims=True)
        sel = eio == idx
        vals.append(m)
        onehots.append(jnp.where(sel, valid, 0.0))
        eid_ref[pl.ds(k, 1), :] = idx.astype(jnp.int32)
        l = jnp.where(sel, -jnp.inf, l)

    exps = [jnp.exp(v - vals[0]) for v in vals]
    denom = exps[0] + exps[1] + exps[2] + exps[3]
    for k in range(TOP_K):
        gate_ref[pl.ds(k, 1), :] = exps[k] / denom

    oh = onehots[0] + onehots[1] + onehots[2] + onehots[3]
    tt = lax.broadcasted_iota(jnp.int32, (tm, tm), 0)
    uu = lax.broadcasted_iota(jnp.int32, (tm, tm), 1)
    upper = jnp.where(tt < uu, 1.0, 0.0).astype(BF16)
    carry = carry_scr[...]
    before = jnp.dot(oh.astype(BF16), upper, preferred_element_type=F32) + carry[:, 0:1]
    for k in range(TOP_K):
        rank_ref[pl.ds(k, 1), :] = jnp.sum(onehots[k] * before, axis=0, keepdims=True).astype(jnp.int32)
    carry = carry + jnp.sum(oh, axis=1, keepdims=True)
    carry_scr[...] = carry
    cnt_ref[...] = carry


def _router(hs1, norm_w, router_w, router_b, is_token, *, tm, sub):
    T, D = hs1.shape
    E = router_w.shape[1]
    rw_t = router_w.T.astype(F32)
    rw_hi = rw_t.astype(BF16)
    rw_lo = (rw_t - rw_hi.astype(F32)).astype(BF16)
    rw_stack = jnp.concatenate([rw_hi, rw_lo], axis=0)
    n = T // tm
    return pl.pallas_call(
        functools.partial(_router_kernel, sub=sub),
        out_shape=(
            jax.ShapeDtypeStruct((T, D // 2), jnp.uint32),
            jax.ShapeDtypeStruct((TOP_K, T), jnp.int32),
            jax.ShapeDtypeStruct((TOP_K, T), F32),
            jax.ShapeDtypeStruct((TOP_K, T), jnp.int32),
            jax.ShapeDtypeStruct((E, CHUNK), F32),
        ),
        grid=(n,),
        in_specs=[
            pl.BlockSpec((tm, D), lambda i: (i, 0)),
            pl.BlockSpec((1, D), lambda i: (0, 0)),
            pl.BlockSpec((2 * E, D), lambda i: (0, 0)),
            pl.BlockSpec((E, D), lambda i: (0, 0)),
            pl.BlockSpec((E, 1), lambda i: (0, 0)),
            pl.BlockSpec((1, tm), lambda i: (0, i)),
        ],
        out_specs=(
            pl.BlockSpec((tm, D // 2), lambda i: (i, 0)),
            pl.BlockSpec((TOP_K, tm), lambda i: (0, i)),
            pl.BlockSpec((TOP_K, tm), lambda i: (0, i)),
            pl.BlockSpec((TOP_K, tm), lambda i: (0, i)),
            pl.BlockSpec((E, CHUNK), lambda i: (0, 0)),
        ),
        scratch_shapes=[pltpu.VMEM((tm, D), BF16), pltpu.VMEM((tm, D), BF16), pltpu.VMEM((E, CHUNK), F32)],
        compiler_params=_params(("arbitrary",)),
        name="router",
    )(hs1, norm_w.reshape(1, D), rw_stack, rw_hi, router_b.reshape(E, 1).astype(F32),
      is_token.astype(F32).reshape(1, T))


def _dispatch_kernel(pend_ref, padded_ref, nu_ref, dest_ref, fp_ref, zeros_ref, xs_hbm, zsem, sem, *, moe_tm):
    i = pl.program_id(0)
    tm = fp_ref.shape[0]
    E = pend_ref.shape[0]
    n_blocks = xs_hbm.shape[0] // moe_tm

    def zero_rows(start):
        return pltpu.make_async_copy(zeros_ref, xs_hbm.at[pl.ds(pl.multiple_of(start, moe_tm), moe_tm)], zsem)

    @pl.when(i == 0)
    def _():
        for e in range(E):
            @pl.when(padded_ref[e] > 0)
            def _():
                zero_rows(pend_ref[e] - moe_tm).start()

        def start_tail(b, c):
            zero_rows(b * moe_tm).start()
            return c
        lax.fori_loop(nu_ref[0], n_blocks, start_tail, 0)
        for e in range(E):
            @pl.when(padded_ref[e] > 0)
            def _():
                zero_rows(pend_ref[e] - moe_tm).wait()

        def wait_tail(b, c):
            zero_rows(b * moe_tm).wait()
            return c
        lax.fori_loop(nu_ref[0], n_blocks, wait_tail, 0)

    def body(r, c):
        for k in range(TOP_K):
            pltpu.make_async_copy(fp_ref.at[pl.ds(r, 1)], xs_hbm.at[pl.ds(dest_ref[k, r], 1)], sem).start()
        return c
    lax.fori_loop(0, tm, body, 0)
    for k in range(TOP_K):
        pltpu.make_async_copy(fp_ref, xs_hbm.at[pl.ds(0, tm)], sem).wait()


def _dispatch(fp, dest3, pend, padded, n_used, *, rows_total, moe_tm, tile):
    T, W = fp.shape
    assert rows_total % moe_tm == 0
    zeros = jnp.zeros((moe_tm, W), fp.dtype)
    grid_spec = pltpu.PrefetchScalarGridSpec(
        num_scalar_prefetch=3,
        grid=(T // tile,),
        in_specs=[
            pl.BlockSpec((None, TOP_K, tile), lambda i, *_: (i, 0, 0), memory_space=pltpu.SMEM),
            pl.BlockSpec((tile, W), lambda i, *_: (i, 0)),
            pl.BlockSpec((moe_tm, W), lambda i, *_: (0, 0)),
        ],
        out_specs=pl.BlockSpec(memory_space=pl.ANY),
        scratch_shapes=[pltpu.SemaphoreType.DMA, pltpu.SemaphoreType.DMA],
    )
    return pl.pallas_call(
        functools.partial(_dispatch_kernel, moe_tm=moe_tm),
        out_shape=jax.ShapeDtypeStruct((rows_total, W), fp.dtype),
        grid_spec=grid_spec,
        compiler_params=_params(("arbitrary",)),
        name="dispatch",
    )(pend, padded, n_used, dest3, fp, zeros)


def _expert_rows_pipeline(rs_ref, pr_ref, nx_ref, x_hbm, o_hbm, xbuf, obuf, zbuf, isem, osem, zsem, st,
                          *, col0, compute):
    big, small = GROUP_SIZES[0], GROUP_SIZES[-1]
    j = pl.program_id(0)
    e = pl.program_id(1)
    nj = pl.num_programs(0)
    E = pl.num_programs(1)
    tn = obuf.shape[2]
    first_step = jnp.logical_and(j == 0, e == 0)
    last_step = jnp.logical_and(j == nj - 1, e == E - 1)

    def items_of(expert):
        pr = pr_ref[expert]
        n_full = pr // big
        rem = pr - n_full * big
        n_items = n_full
        for size in GROUP_SIZES[1:]:
            n_items = n_items + (rem // size) % 2
        return n_full, rem, n_items

    def item(t, n_full, rem):
        cls = jnp.int32(0)
        off = jnp.minimum(t, n_full) * big
        seen = jnp.int32(0)
        before = jnp.int32(0)
        for c, size in enumerate(GROUP_SIZES[1:], start=1):
            has = (rem // size) % 2
            here = jnp.logical_and(has == 1, t == n_full + seen)
            cls = jnp.where(here, c, cls)
            off = jnp.where(here, n_full * big + before, off)
            seen = seen + has
            before = before + has * size
        return cls, off

    def in_copy(row0, size, slot):
        rows = pl.ds(pl.multiple_of(row0, small), size)
        return pltpu.make_async_copy(x_hbm.at[rows], xbuf.at[slot, pl.ds(0, size)], isem.at[slot])

    def out_copy(row0, size, slot):
        rows = pl.ds(pl.multiple_of(row0, small), size)
        return pltpu.make_async_copy(obuf.at[slot, pl.ds(0, size)], o_hbm.at[rows, pl.ds(col0, tn)], osem.at[slot])

    def by_size(cls, fn):
        for c, size in enumerate(GROUP_SIZES):
            @pl.when(cls == c)
            def _():
                fn(size)

    def start_first_item(expert, slot):
        n_full, rem, n_items = items_of(expert)
        cls, _ = item(0, n_full, rem)

        @pl.when(n_items > 0)
        def _():
            by_size(cls, lambda size: in_copy(rs_ref[expert], size, slot).start())

    def wait_pending_out(slot):
        by_size(st[1 + slot], lambda size: out_copy(rs_ref[0], size, slot).wait())
        st[1 + slot] = -1

    @pl.when(first_step)
    def _():
        st[0] = 0
        st[1] = -1
        st[2] = -1
        start_first_item(jnp.where(pr_ref[0] > 0, 0, nx_ref[0]), 0)

    rs = rs_ref[e]
    n_full, rem, n_items = items_of(e)
    nxt = nx_ref[e]
    stream_continues = jnp.logical_not(jnp.logical_and(nxt <= e, j == nj - 1))

    def body(t, carry):
        g = st[0]
        slot = g % 2
        cls, off = item(t, n_full, rem)
        by_size(cls, lambda size: in_copy(rs + off, size, slot).wait())

        @pl.when(t + 1 < n_items)
        def _():
            cls1, off1 = item(t + 1, n_full, rem)
            by_size(cls1, lambda size: in_copy(rs + off1, size, 1 - slot).start())

        @pl.when(jnp.logical_and(t + 1 == n_items, stream_continues))
        def _():
            start_first_item(nxt, 1 - slot)

        wait_pending_out(slot)

        def run(size):
            compute(size, slot)
            out_copy(rs + off, size, slot).start()
        by_size(cls, run)
        st[1 + slot] = cls
        st[0] = g + 1
        return carry
    lax.fori_loop(0, n_items, body, 0)

    @pl.when(last_step)
    def _():
        wait_pending_out(0)
        wait_pending_out(1)

    @pl.when(e == E - 1)
    def _():
        zbuf[...] = jnp.zeros_like(zbuf)
        first = (rs + pr_ref[e]) // small
        last = o_hbm.shape[0] // small

        def fill_copy(b):
            rows = pl.ds(pl.multiple_of(b * small, small), small)
            return pltpu.make_async_copy(zbuf, o_hbm.at[rows, pl.ds(col0, tn)], zsem)

        def start_fill(b, c):
            fill_copy(b).start()
            return c

        def wait_fill(b, c):
            fill_copy(b).wait()
            return c
        lax.fori_loop(first, last, start_fill, 0)
        lax.fori_loop(first, last, wait_fill, 0)


def _next_nonempty(padded):
    E = padded.shape[0]
    ids = jnp.arange(E, dtype=jnp.int32)
    dist = jnp.where(padded[None, :] > 0, (ids[None, :] - ids[:, None] - 1) % E, E)
    return ((ids + 1 + jnp.min(dist, axis=1)) % E).astype(jnp.int32)


def _expert_scratch(big, width_in, dtype_in, tn, dtype_out):
    small = GROUP_SIZES[-1]
    return [pltpu.VMEM((2, big, width_in), dtype_in), pltpu.VMEM((2, big, tn), dtype_out),
            pltpu.VMEM((small, tn), dtype_out),
            pltpu.SemaphoreType.DMA((2,)), pltpu.SemaphoreType.DMA((2,)), pltpu.SemaphoreType.DMA,
            pltpu.SMEM((3,), jnp.int32)]


def _gate_up_kernel(rs_ref, pr_ref, nx_ref, xs_hbm, wg_ref, wu_ref, bg_ref, bu_ref, h_hbm,
                    xbuf, obuf, zbuf, isem, osem, zsem, st):
    tn = obuf.shape[2]

    def compute(size, slot):
        lo, hi = _unpack_bf16_pair(xbuf[slot, pl.ds(0, size), :])
        x = jnp.concatenate([lo, hi], axis=-1)
        gate = jnp.dot(x, wg_ref[...].astype(BF16), preferred_element_type=F32) + bg_ref[...]
        up = jnp.dot(x, wu_ref[...].astype(BF16), preferred_element_type=F32) + bu_ref[...]
        gate = jnp.minimum(gate, SWIGLU_LIMIT)
        up = jnp.clip(up, -SWIGLU_LIMIT, SWIGLU_LIMIT)
        obuf[slot, pl.ds(0, size), :] = ((up + 1.0) * gate * jax.nn.sigmoid(SWIGLU_ALPHA * gate)).astype(obuf.dtype)

    _expert_rows_pipeline(rs_ref, pr_ref, nx_ref, xs_hbm, h_hbm, xbuf, obuf, zbuf, isem, osem, zsem, st,
                          col0=pl.multiple_of(pl.program_id(0) * tn, tn), compute=compute)


def _gate_up(xs, row_start, row_count, w_gate_up, b_gate_up, *, rows, tn):
    E, D, F2 = w_gate_up.shape
    FF = F2 // 2
    nj = FF // tn
    W = xs.shape[1]
    grid_spec = pltpu.PrefetchScalarGridSpec(
        num_scalar_prefetch=3,
        grid=(nj, E),
        in_specs=[
            pl.BlockSpec(memory_space=pl.ANY),
            pl.BlockSpec((None, D, tn), lambda j, e, *_: (e, 0, j)),
            pl.BlockSpec((None, D, tn), lambda j, e, *_: (e, 0, nj + j)),
            pl.BlockSpec((None, 1, tn), lambda j, e, *_: (e, 0, j)),
            pl.BlockSpec((None, 1, tn), lambda j, e, *_: (e, 0, nj + j)),
        ],
        out_specs=pl.BlockSpec(memory_space=pl.ANY),
        scratch_shapes=_expert_scratch(GROUP_SIZES[0], W, xs.dtype, tn, BF16),
    )
    b3 = b_gate_up.reshape(E, 1, F2)
    return pl.pallas_call(
        _gate_up_kernel,
        out_shape=jax.ShapeDtypeStruct((rows, FF), BF16),
        grid_spec=grid_spec,
        compiler_params=_params(("arbitrary", "arbitrary")),
        name="gate_up",
    )(row_start, row_count, _next_nonempty(row_count), xs, w_gate_up, w_gate_up, b3, b3)


def _down_kernel(rs_ref, pr_ref, nx_ref, h_hbm, w_ref, b_ref, y_hbm, xbuf, obuf, zbuf, isem, osem, zsem, st):
    tn = obuf.shape[2]

    def compute(size, slot):
        hrows = xbuf[slot, pl.ds(0, size), :]
        y = jnp.dot(hrows, w_ref[...].astype(BF16), preferred_element_type=F32) + b_ref[...]
        obuf[slot, pl.ds(0, size), :] = _pack_bf16_pair(y[:, :tn], y[:, tn:])

    _expert_rows_pipeline(rs_ref, pr_ref, nx_ref, h_hbm, y_hbm, xbuf, obuf, zbuf, isem, osem, zsem, st,
                          col0=pl.multiple_of(pl.program_id(0) * tn, tn), compute=compute)


def _down(h, row_start, row_count, w_down, b_down, *, tn):
    E, FF, D = w_down.shape
    rows = h.shape[0]
    grid_spec = pltpu.PrefetchScalarGridSpec(
        num_scalar_prefetch=3,
        grid=(D // tn, E),
        in_specs=[
            pl.BlockSpec(memory_space=pl.ANY),
            pl.BlockSpec((None, FF, tn), lambda j, e, *_: (e, 0, j)),
            pl.BlockSpec((None, 1, tn), lambda j, e, *_: (e, 0, j)),
        ],
        out_specs=pl.BlockSpec(memory_space=pl.ANY),
        scratch_shapes=_expert_scratch(GROUP_SIZES[0], FF, h.dtype, tn // 2, jnp.uint32),
    )
    return pl.pallas_call(
        _down_kernel,
        out_shape=jax.ShapeDtypeStruct((rows, D // 2), jnp.uint32),
        grid_spec=grid_spec,
        compiler_params=_params(("arbitrary", "arbitrary")),
        name="down",
    )(row_start, row_count, _next_nonempty(row_count), h, w_down, b_down.reshape(E, 1, D))


def _combine_kernel(dest_ref, dnext_ref, hs_ref, gate_ref, nw_ref, y_hbm, o_ref, ybuf, sem, *, sub, half):
    s = pl.program_id(0)
    n = pl.num_programs(0)
    tm, D = hs_ref.shape
    W = ybuf.shape[3]
    slot = s % 2
    lanes = CHUNK
    n_chunks = W // lanes
    per_chunk = sub * TOP_K // n_chunks
    assert per_chunk * n_chunks == sub * TOP_K and half % lanes == 0

    def row_copy(d_ref, r, k, to_slot):
        return pltpu.make_async_copy(y_hbm.at[pl.ds(d_ref[k, r], 1)], ybuf.at[to_slot, k, pl.ds(r, 1)],
                                     sem.at[to_slot])

    def wait_slot(which):
        for k in range(TOP_K):
            pltpu.make_async_copy(y_hbm.at[pl.ds(0, tm)], ybuf.at[which, k], sem.at[which]).wait()

    @pl.when(s == 0)
    def _():
        def body(r, c):
            for k in range(TOP_K):
                row_copy(dest_ref, r, k, 0).start()
            return c
        lax.fori_loop(0, tm, body, 0)

    wait_slot(slot)

    def body(rb, c):
        row0 = pl.multiple_of(rb * sub, sub)
        rows = pl.ds(row0, sub)
        g = gate_ref[rows, :]
        gk = [jnp.broadcast_to(g[:, k:k + 1], (sub, lanes)) for k in range(TOP_K)]
        ssq = jnp.zeros((sub, lanes), F32)
        for ch in range(n_chunks):
            for i in range(ch * per_chunk, (ch + 1) * per_chunk):
                row_copy(dnext_ref, row0 + i // TOP_K, i % TOP_K, 1 - slot).start()
            j, within = divmod(ch * lanes, half)
            lo_cols = slice(2 * j * half + within, 2 * j * half + within + lanes)
            hi_cols = slice((2 * j + 1) * half + within, (2 * j + 1) * half + within + lanes)
            lo = hs_ref[rows, lo_cols]
            hi = hs_ref[rows, hi_cols]
            for k in range(TOP_K):
                word = ybuf[slot, k, rows, ch * lanes:(ch + 1) * lanes]
                lo = lo + lax.bitcast_convert_type(word << jnp.uint32(16), F32) * gk[k]
                hi = hi + lax.bitcast_convert_type(word & jnp.uint32(0xFFFF0000), F32) * gk[k]
            o_ref[rows, lo_cols] = lo
            o_ref[rows, hi_cols] = hi
            ssq = ssq + lo * lo + hi * hi
        ms = jnp.sum(ssq, axis=-1, keepdims=True) / D
        o_ref[rows, :] = o_ref[rows, :] * lax.rsqrt(ms + RMS_EPS) * nw_ref[...]
        return c
    lax.fori_loop(0, tm // sub, body, 0)

    @pl.when(s == n - 1)
    def _():
        wait_slot(1 - slot)


def _combine(hs1, gates_t, dest3, y, final_norm_w, *, B, LP, seq_len, tile, sub, half):
    T, D = hs1.shape
    W = y.shape[1]
    per_seq = LP // tile
    n_seq_tiles = seq_len // tile
    lead = (LP - seq_len) // tile
    n_steps = B * n_seq_tiles

    def tok(s):
        return (s // n_seq_tiles) * per_seq + lead + s % n_seq_tiles

    return pl.pallas_call(
        functools.partial(_combine_kernel, sub=sub, half=half),
        out_shape=jax.ShapeDtypeStruct((B, seq_len, D), F32),
        grid=(n_steps,),
        in_specs=[
            pl.BlockSpec((None, TOP_K, tile), lambda s: (tok(s), 0, 0), memory_space=pltpu.SMEM),
            pl.BlockSpec((None, TOP_K, tile), lambda s: (tok(jnp.minimum(s + 1, n_steps - 1)), 0, 0),
                         memory_space=pltpu.SMEM),
            pl.BlockSpec((tile, D), lambda s: (tok(s), 0)),
            pl.BlockSpec((tile, TOP_K), lambda s: (tok(s), 0)),
            pl.BlockSpec((1, D), lambda s: (0, 0)),
            pl.BlockSpec(memory_space=pl.ANY),
        ],
        out_specs=pl.BlockSpec((None, tile, D), lambda s: (s // n_seq_tiles, s % n_seq_tiles, 0)),
        scratch_shapes=[pltpu.VMEM((2, TOP_K, tile, W), y.dtype), pltpu.SemaphoreType.DMA((2,))],
        compiler_params=_params(("arbitrary",)),
        name="combine",
    )(dest3, dest3, hs1, gates_t, final_norm_w.reshape(1, D), y)


def _forward(x, meta_tokens, norm_mix_w, w_in, ret_log_rate_fwd, ret_log_rate_bwd, pool_w, pool_scale,
             w_out, norm_ffn_w, router_w, router_b, w_gate_up, b_gate_up, w_down, b_down, final_norm_w, tiles):
    B, S, D = x.shape
    depth = norm_mix_w.shape[0]
    assert depth == 1, "single-layer block"
    H = ret_log_rate_fwd.shape[-1]
    n_groups, cg = pool_w.shape[1], pool_w.shape[2]
    PW = n_groups * cg
    RW = w_out.shape[1] - PW
    d = RW // H
    E = router_w.shape[-1]
    assert w_in.shape[-1] == 4 * RW + PW and S % CHUNK == 0
    LP = PAD + N_META + S
    T = B * LP
    t = tiles

    lead = jnp.concatenate([jnp.zeros((PAD, D), x.dtype), meta_tokens.astype(x.dtype)], axis=0)
    hs0, a0 = _prep(x, lead, norm_mix_w[0], n_chunks=t["prep_chunks"], sub=t["norm_rows"])

    proj = _in_proj(a0, w_in[0].astype(BF16), tm=t["in_tm"], tn=t["in_tn"])
    pos = (jnp.arange(LP, dtype=jnp.int32) - PAD).astype(F32)
    inv_freq = ROPE_BASE ** (-jnp.arange(d // 2, dtype=F32) / (d // 2))
    ang = pos[:, None] * inv_freq[None, :]
    log_gamma = jnp.stack([-jnp.exp(ret_log_rate_fwd[0].astype(F32)), -jnp.exp(ret_log_rate_bwd[0].astype(F32))])
    ret = _retention(proj, log_gamma, jnp.cos(ang), jnp.sin(ang), B=B, LP=LP, H=H, d=d, GC=t["ret_group_chunks"],
                     HB=t["ret_heads_per_step"])
    pooled = _pool(proj, pool_w[0].astype(BF16), pool_scale[0].astype(F32), LP=LP, seq_len=N_META + S,
                   u_col0=4 * RW, tm=t["pool_tm"])
    hs1 = _out_proj(ret, pooled, w_out[0].astype(BF16), hs0, tm=t["out_tm"], tn=t["out_tn"])

    row_id = jnp.arange(T, dtype=jnp.int32)
    is_token = (row_id % LP) >= PAD
    fp, eid, gates, rank, cnt = _router(hs1, norm_ffn_w[0], router_w[0], router_b[0], is_token,
                                        tm=t["router_tm"], sub=t["router_rows"])
    align = GROUP_SIZES[-1]
    tile = t["token_tile"]
    dtile = t["dispatch_tile"]
    counts = cnt[:, 0].astype(jnp.int32)
    padded = (counts + align - 1) // align * align
    pend = jnp.cumsum(padded).astype(jnp.int32)
    pstart = pend - padded
    n_assign = B * (N_META + S) * TOP_K
    rows = -(-(n_assign + E * (align - 1)) // align) * align
    n_used = pend[-1:] // align
    spare = rows + jnp.arange(TOP_K, dtype=jnp.int32)[:, None] * dtile + (row_id % dtile)[None, :]
    expert_ids = jnp.arange(E, dtype=jnp.int32)[:, None, None]
    first_row = jnp.sum(jnp.where(eid[None] == expert_ids, pstart[:, None, None], 0), axis=0)
    dest = jnp.where(is_token[None, :], first_row + rank, spare)

    def by_tile(width):
        return dest.reshape(TOP_K, T // width, width).transpose(1, 0, 2)
    dest3 = by_tile(tile)

    spare_rows = -(-(TOP_K * dtile) // align) * align
    xs = _dispatch(fp, by_tile(dtile), pend, padded, n_used, rows_total=rows + spare_rows, moe_tm=align, tile=dtile)
    hmid = _gate_up(xs, pstart, padded, w_gate_up[0], b_gate_up[0], rows=rows, tn=t["gu_tn"])
    y = _down(hmid, pstart, padded, w_down[0], b_down[0], tn=t["down_tn"])
    return _combine(hs1, gates.T, dest3, y, final_norm_w, B=B, LP=LP, seq_len=S, tile=tile, sub=t["combine_rows"],
                    half=t["down_tn"] // 2)


def kernel(x, meta_tokens, norm_mix_w, w_in, ret_log_rate_fwd, ret_log_rate_bwd, pool_w, pool_scale, w_out,
           norm_ffn_w, router_w, router_b, w_gate_up, b_gate_up, w_down, b_down, final_norm_w):
    return _forward(x, meta_tokens, norm_mix_w, w_in, ret_log_rate_fwd, ret_log_rate_bwd, pool_w, pool_scale,
                    w_out, norm_ffn_w, router_w, router_b, w_gate_up, b_gate_up, w_down, b_down, final_norm_w,
                    DEFAULT_TILES)
```

```python
import functools

import jax
import jax.numpy as jnp
from jax import lax
from jax.experimental import pallas as pl
from jax.experimental.pallas import tpu as pltpu

N_META = 16
CHUNK = 128
PAD = CHUNK - N_META
POOL_WINDOWS = (2, 4, 8, 16)
NORM_LANES = 512
POOL_HALO = 64
TOP_K = 4
GROUP_SIZES = (1024, 512, 256, 128)
SWIGLU_LIMIT = 7.0
SWIGLU_ALPHA = 1.702
RMS_EPS = 1e-5
ROPE_BASE = 10000.0

V7X_VMEM_BYTES = 64 * 1024 * 1024
VMEM_LIMIT = V7X_VMEM_BYTES - 2 * 1024 * 1024

F32 = jnp.float32
BF16 = jnp.bfloat16

DEFAULT_TILES = dict(
    prep_chunks=5, in_tm=1280, in_tn=1024, norm_rows=32,
    ret_group_chunks=5, ret_heads_per_step=2,
    pool_tm=640,
    out_tm=1280, out_tn=512,
    router_tm=640, router_rows=32,
    gu_tn=512, down_tn=2048,
    dispatch_tile=256, token_tile=128, combine_rows=32,
)


def _params(semantics):
    return pltpu.CompilerParams(dimension_semantics=semantics, vmem_limit_bytes=VMEM_LIMIT)


def _prep_kernel(*refs, sub, n_chunks, chunks_per_seq):
    x_refs = refs[:n_chunks]
    lead_ref, nw_ref, hs_ref, a_ref = refs[n_chunks:]
    i = pl.program_id(0)
    for c in range(n_chunks):
        is_lead = (i * n_chunks + c) % chunks_per_seq == 0

        def body(r, carry, c=c, is_lead=is_lead):
            rows = pl.ds(pl.multiple_of(r * sub, sub), sub)
            out_rows = pl.ds(pl.multiple_of(c * CHUNK + r * sub, sub), sub)
            x = jnp.where(is_lead, lead_ref[rows, :], x_refs[c][rows, :])
            hs_ref[out_rows, :] = x
            ms = jnp.mean(x * x, axis=-1, keepdims=True)
            a_ref[out_rows, :] = (x * lax.rsqrt(ms + RMS_EPS) * nw_ref[...]).astype(a_ref.dtype)
            return carry
        lax.fori_loop(0, CHUNK // sub, body, 0)


def _prep(x, lead, norm_w, *, n_chunks, sub):
    B, S, D = x.shape
    x_chunks_per_seq = S // CHUNK
    chunks_per_seq = x_chunks_per_seq + 1
    T = B * chunks_per_seq * CHUNK
    assert (B * chunks_per_seq) % n_chunks == 0

    def x_map(c):
        def m(i):
            q = i * n_chunks + c
            return ((q // chunks_per_seq) * x_chunks_per_seq + jnp.maximum(q % chunks_per_seq - 1, 0), 0)
        return m

    return pl.pallas_call(
        functools.partial(_prep_kernel, sub=sub, n_chunks=n_chunks, chunks_per_seq=chunks_per_seq),
        out_shape=(jax.ShapeDtypeStruct((T, D), x.dtype), jax.ShapeDtypeStruct((T, D), BF16)),
        grid=(B * chunks_per_seq // n_chunks,),
        in_specs=[pl.BlockSpec((CHUNK, D), x_map(c)) for c in range(n_chunks)] + [
            pl.BlockSpec((CHUNK, D), lambda i: (0, 0)),
            pl.BlockSpec((1, D), lambda i: (0, 0)),
        ],
        out_specs=(pl.BlockSpec((n_chunks * CHUNK, D), lambda i: (i, 0)),
                   pl.BlockSpec((n_chunks * CHUNK, D), lambda i: (i, 0))),
        compiler_params=_params(("arbitrary",)),
        name="prep",
    )(*([x.reshape(B * S, D)] * n_chunks), lead, norm_w.reshape(1, D))


def _in_proj_kernel(a_ref, w_ref, o_ref):
    o_ref[...] = jnp.dot(a_ref[...], w_ref[...], preferred_element_type=F32).astype(o_ref.dtype)


def _in_proj(a, w_in_bf16, *, tm, tn):
    T, D = a.shape
    N = w_in_bf16.shape[1]
    return pl.pallas_call(
        _in_proj_kernel,
        out_shape=jax.ShapeDtypeStruct((T, N), BF16),
        grid=(T // tm, N // tn),
        in_specs=[
            pl.BlockSpec((tm, D), lambda i, j: (i, 0)),
            pl.BlockSpec((D, tn), lambda i, j: (0, j)),
        ],
        out_specs=pl.BlockSpec((tm, tn), lambda i, j: (i, j)),
        compiler_params=_params(("arbitrary", "arbitrary")),
        name="in_proj",
    )(a, w_in_bf16)


def _retention_kernel(lg_ref, q_ref, k_ref, v_ref, g_ref, cos_ref, sin_ref, o_ref,
                      acc_scr, sf_scr, sb_scr, dmat_scr, dvec_scr, *, G, GC):
    hb = pl.program_id(0)
    s = pl.program_id(1)
    phase = s // G
    w = s % G
    gi = w + phase * (G - 1 - 2 * w)
    C = CHUNK
    d = cos_ref.shape[1] * 2
    half = d // 2
    NB = q_ref.shape[0]
    HB = q_ref.shape[2] // d
    GR = GC * C
    k_scale = d ** -0.5
    nt = (((1,), (1,)), ((), ()))
    tn = (((0,), (0,)), ((), ()))

    @pl.when(s == 0)
    def _():
        t = lax.broadcasted_iota(jnp.int32, (C, C), 0).astype(F32)
        u = lax.broadcasted_iota(jnp.int32, (C, C), 1).astype(F32)
        diff = t - u
        idx = lax.broadcasted_iota(jnp.int32, (C, d), 0).astype(F32)
        for hh in range(HB):
            lgf = lg_ref[0, hb * HB + hh]
            lgb = lg_ref[1, hb * HB + hh]
            dmat_scr[hh] = jnp.where(diff >= 0.0, jnp.exp(lgf * jnp.maximum(diff, 0.0)),
                                     jnp.exp(lgb * jnp.maximum(-diff, 0.0)))
            dvec_scr[hh, 0] = jnp.exp(lgf * (idx + 1.0))
            dvec_scr[hh, 1] = jnp.exp(lgf * (C - 1.0 - idx))
            dvec_scr[hh, 2] = jnp.exp(lgb * (C - idx))
            dvec_scr[hh, 3] = jnp.exp(lgb * idx)
            dvec_scr[hh, 4] = jnp.exp(jnp.full((C, d), lgf * C, F32))
            dvec_scr[hh, 5] = jnp.exp(jnp.full((C, d), lgb * C, F32))
        sf_scr[...] = jnp.zeros_like(sf_scr)

    @pl.when(s == G)
    def _():
        sb_scr[...] = jnp.zeros_like(sb_scr)

    def rotary(ref, bb, rows, hh):
        x = ref[bb, rows, hh * d:(hh + 1) * d].astype(F32)
        x1 = x[:, :half]
        x2 = x[:, half:]
        cs = cos_ref[rows, :]
        sn = sin_ref[rows, :]
        return jnp.concatenate([x1 * cs - x2 * sn, x2 * cs + x1 * sn], axis=-1)

    chains = [(bb, hh) for bb in range(NB) for hh in range(HB)]

    @pl.when(phase == 0)
    def _():
        def body(c, carry):
            rows = pl.ds(pl.multiple_of(c * C, C), C)
            arows = pl.ds(pl.multiple_of(gi * GR + c * C, C), C)
            for ci, (bb, hh) in enumerate(chains):
                cols = slice(hh * d, (hh + 1) * d)
                qc = rotary(q_ref, bb, rows, hh).astype(BF16)
                kf = rotary(k_ref, bb, rows, hh) * k_scale
                vc = v_ref[bb, rows, cols]
                sc = lax.dot_general(qc, kf.astype(BF16), nt, preferred_element_type=F32)
                p = (sc * dmat_scr[hh]).astype(BF16)
                inner = jnp.dot(p, vc, preferred_element_type=F32)
                sf = sf_scr[ci]
                cross = jnp.dot(qc, sf.astype(BF16), preferred_element_type=F32) * dvec_scr[hh, 0]
                acc_scr[bb, arows, cols] = inner + cross
                kd = (kf * dvec_scr[hh, 1]).astype(BF16)
                sf_scr[ci] = sf * dvec_scr[hh, 4, 0:1, :] + lax.dot_general(kd, vc, tn, preferred_element_type=F32)
            return carry
        lax.fori_loop(0, GC, body, 0)

    @pl.when(phase == 1)
    def _():
        def body(cc, carry):
            c = GC - 1 - cc
            rows = pl.ds(pl.multiple_of(c * C, C), C)
            arows = pl.ds(pl.multiple_of(gi * GR + c * C, C), C)
            for ci, (bb, hh) in enumerate(chains):
                cols = slice(hh * d, (hh + 1) * d)
                qc = rotary(q_ref, bb, rows, hh).astype(BF16)
                kf = rotary(k_ref, bb, rows, hh) * k_scale
                vc = v_ref[bb, rows, cols]
                sb = sb_scr[ci]
                r = acc_scr[bb, arows, cols] + jnp.dot(qc, sb.astype(BF16), preferred_element_type=F32) * dvec_scr[hh, 2]
                kd = (kf * dvec_scr[hh, 3]).astype(BF16)
                sb_scr[ci] = sb * dvec_scr[hh, 5, 0:1, :] + lax.dot_general(kd, vc, tn, preferred_element_type=F32)
                rn = r * lax.rsqrt(jnp.mean(r * r, axis=-1, keepdims=True) + RMS_EPS)
                gg = g_ref[bb, rows, cols].astype(F32)
                o_ref[bb, rows, cols] = (gg * jax.nn.sigmoid(gg) * rn).astype(o_ref.dtype)
            return carry
        lax.fori_loop(0, GC, body, 0)


def _retention(proj, log_gamma, cos_tab, sin_tab, *, B, LP, H, d, GC, HB):
    T, N = proj.shape
    GR = GC * CHUNK
    G = LP // GR
    assert H % HB == 0
    H = H // HB
    bw = HB * d
    proj3 = proj.reshape(B, LP, N)

    def grp(s):
        phase = s // G
        w = s % G
        return phase, w + phase * (G - 1 - 2 * w)

    def qkv_map(col0):
        def m(h, s):
            _, gi = grp(s)
            return (0, gi, col0 + h)
        return m

    def late_map(col0):
        def m(h, s):
            phase, gi = grp(s)
            return (0, jnp.where(phase == 0, G - 1, gi), col0 + h)
        return m

    def tab_map(h, s):
        _, gi = grp(s)
        return (gi, 0)

    ret = pl.pallas_call(
        functools.partial(_retention_kernel, G=G, GC=GC),
        out_shape=jax.ShapeDtypeStruct((B, LP, H * bw), BF16),
        grid=(H, 2 * G),
        in_specs=[
            pl.BlockSpec(memory_space=pltpu.SMEM),
            pl.BlockSpec((B, GR, bw), qkv_map(0)),
            pl.BlockSpec((B, GR, bw), qkv_map(H)),
            pl.BlockSpec((B, GR, bw), qkv_map(2 * H)),
            pl.BlockSpec((B, GR, bw), late_map(3 * H)),
            pl.BlockSpec((GR, d // 2), tab_map),
            pl.BlockSpec((GR, d // 2), tab_map),
        ],
        out_specs=pl.BlockSpec((B, GR, bw), late_map(0)),
        scratch_shapes=[
            pltpu.VMEM((B, LP, bw), F32),
            pltpu.VMEM((B * HB, d, d), F32),
            pltpu.VMEM((B * HB, d, d), F32),
            pltpu.VMEM((HB, CHUNK, CHUNK), F32),
            pltpu.VMEM((HB, 6, CHUNK, d), F32),
        ],
        compiler_params=_params(("arbitrary", "arbitrary")),
        name="retention",
    )(log_gamma, proj3, proj3, proj3, proj3, cos_tab, sin_tab)
    return ret.reshape(T, H * bw)


def _pool_kernel(prev_ref, cur_ref, next_ref, pw_ref, ps_ref, o_ref, *, tiles_per_seq, seq_len):
    i = pl.program_id(0)
    tm = cur_ref.shape[0]
    ext = tm + 2 * POOL_HALO
    n_groups = pw_ref.shape[0]
    cg = pw_ref.shape[1]
    row0 = (i % tiles_per_seq) * tm - PAD
    pos_t = row0 + lax.broadcasted_iota(jnp.int32, (tm, ext), 0)
    pos_s = row0 - POOL_HALO + lax.broadcasted_iota(jnp.int32, (tm, ext), 1)
    pos_c = row0 + lax.broadcasted_iota(jnp.int32, (tm, 1), 0)
    for gi in range(n_groups):
        win = POOL_WINDOWS[gi]
        cols = slice(gi * cg, (gi + 1) * cg)
        lo = jnp.clip(pos_t - win // 2, 0, seq_len)
        hi = jnp.clip(pos_t + (win - win // 2), 0, seq_len)
        band = jnp.where(pos_s >= lo, jnp.where(pos_s < hi, 1.0, 0.0), 0.0).astype(BF16)
        cnt = (jnp.clip(pos_c + (win - win // 2), 0, seq_len) - jnp.clip(pos_c - win // 2, 0, seq_len)).astype(F32)
        inv = jnp.where(pos_c >= 0, 1.0 / jnp.maximum(cnt, 1.0), 0.0)
        u_cur = cur_ref[:, cols]
        u_ext = jnp.concatenate([prev_ref[:, cols], u_cur, next_ref[:, cols]], axis=0)
        wsum = jnp.dot(band, u_ext, preferred_element_type=F32)
        pooled = wsum * inv - u_cur.astype(F32)
        y = jnp.dot(pooled.astype(BF16), pw_ref[gi], preferred_element_type=F32)
        o_ref[:, cols] = (y * ps_ref[:, cols]).astype(o_ref.dtype)


def _pool(proj, pool_w_bf16, pool_scale, *, LP, seq_len, u_col0, tm):
    T = proj.shape[0]
    n_groups, cg, _ = pool_w_bf16.shape
    PW = n_groups * cg
    col_blk = u_col0 // PW
    hb = tm // POOL_HALO
    n_halo = T // POOL_HALO
    return pl.pallas_call(
        functools.partial(_pool_kernel, tiles_per_seq=LP // tm, seq_len=seq_len),
        out_shape=jax.ShapeDtypeStruct((T, PW), BF16),
        grid=(T // tm,),
        in_specs=[
            pl.BlockSpec((POOL_HALO, PW), lambda i: (jnp.maximum(i * hb - 1, 0), col_blk)),
            pl.BlockSpec((tm, PW), lambda i: (i, col_blk)),
            pl.BlockSpec((POOL_HALO, PW), lambda i: (jnp.minimum((i + 1) * hb, n_halo - 1), col_blk)),
            pl.BlockSpec((n_groups, cg, cg), lambda i: (0, 0, 0)),
            pl.BlockSpec((1, PW), lambda i: (0, 0)),
        ],
        out_specs=pl.BlockSpec((tm, PW), lambda i: (i, 0)),
        compiler_params=_params(("arbitrary",)),
        name="pool",
    )(proj, proj, proj, pool_w_bf16, pool_scale.reshape(1, PW))


def _out_proj_kernel(ret_ref, pool_ref, wr_ref, wp_ref, hs_ref, o_ref):
    mixed = jnp.concatenate([ret_ref[...], pool_ref[...]], axis=-1)
    w = jnp.concatenate([wr_ref[...], wp_ref[...]], axis=0)
    o_ref[...] = hs_ref[...] + jnp.dot(mixed, w, preferred_element_type=F32)


def _out_proj(ret, pool, w_out_bf16, hs, *, tm, tn):
    T, RW = ret.shape
    PW = pool.shape[1]
    D = hs.shape[1]
    assert RW == PW, "the two mixer halves share one row-block split of w_out"
    return pl.pallas_call(
        _out_proj_kernel,
        out_shape=jax.ShapeDtypeStruct((T, D), F32),
        grid=(T // tm, D // tn),
        in_specs=[
            pl.BlockSpec((tm, RW), lambda i, j: (i, 0)),
            pl.BlockSpec((tm, PW), lambda i, j: (i, 0)),
            pl.BlockSpec((RW, tn), lambda i, j: (0, j)),
            pl.BlockSpec((PW, tn), lambda i, j: (1, j)),
            pl.BlockSpec((tm, tn), lambda i, j: (i, j)),
        ],
        out_specs=pl.BlockSpec((tm, tn), lambda i, j: (i, j)),
        compiler_params=_params(("arbitrary", "arbitrary")),
        name="out_proj",
    )(ret, pool, w_out_bf16, w_out_bf16, hs)


def _pack_rounded_pair(lo, hi):
    lo_bits = lax.bitcast_convert_type(lo, jnp.uint32)
    hi_bits = lax.bitcast_convert_type(hi, jnp.uint32)
    return (hi_bits & jnp.uint32(0xFFFF0000)) | (lo_bits >> jnp.uint32(16))


def _pack_bf16_pair(lo, hi):
    return _pack_rounded_pair(lo.astype(BF16).astype(F32), hi.astype(BF16).astype(F32))


def _unpack_bf16_pair(word):
    lo = lax.bitcast_convert_type(word << jnp.uint32(16), F32).astype(BF16)
    hi = lax.bitcast_convert_type(word & jnp.uint32(0xFFFF0000), F32).astype(BF16)
    return lo, hi


def _router_kernel(hs_ref, nw_ref, rw_ref, rwhi_ref, rb_ref, valid_ref,
                   fp_ref, eid_ref, gate_ref, rank_ref, cnt_ref,
                   fhi_scr, flo_scr, carry_scr, *, sub):
    i = pl.program_id(0)
    tm, D = hs_ref.shape
    E = rb_ref.shape[0]
    half = D // 2

    @pl.when(i == 0)
    def _():
        carry_scr[...] = jnp.zeros_like(carry_scr)

    cw = min(NORM_LANES, half)

    def body(r, c):
        rows = pl.ds(pl.multiple_of(r * sub, sub), sub)
        sq = jnp.zeros((sub, cw), F32)
        for c0 in range(0, D, cw):
            x = hs_ref[rows, c0:c0 + cw]
            sq = sq + x * x
        scale = lax.rsqrt(jnp.sum(sq, axis=-1, keepdims=True) / D + RMS_EPS)
        for c0 in range(0, half, cw):
            rounded = []
            for cols in (slice(c0, c0 + cw), slice(half + c0, half + c0 + cw)):
                f = hs_ref[rows, cols] * scale * nw_ref[:, cols]
                f_hi = f.astype(BF16)
                f_rounded = f_hi.astype(F32)
                fhi_scr[rows, cols] = f_hi
                flo_scr[rows, cols] = (f - f_rounded).astype(BF16)
                rounded.append(f_rounded)
            fp_ref[rows, c0:c0 + cw] = _pack_rounded_pair(rounded[0], rounded[1])
        return c
    lax.fori_loop(0, tm // sub, body, 0)

    nt = (((1,), (1,)), ((), ()))
    part = lax.dot_general(rw_ref[...], fhi_scr[...], nt, preferred_element_type=F32)
    logits = part[:E] + part[E:] + lax.dot_general(rwhi_ref[...], flo_scr[...], nt, preferred_element_type=F32)
    logits = logits + rb_ref[...]

    eio = lax.broadcasted_iota(jnp.int32, (E, tm), 0).astype(F32)
    valid = valid_ref[...]

    vals, onehots = [], []
    l = logits
    for k in range(TOP_K):
        m = jnp.max(l, axis=0, keepdims=True)
        idx = jnp.min(jnp.where(l == m, eio, float(E)), axis=0, keepdims=True)
        sel = eio == idx
        vals.append(m)
        onehots.append(jnp.where(sel, valid, 0.0))
        eid_ref[pl.ds(k, 1), :] = idx.astype(jnp.int32)
        l = jnp.where(sel, -jnp.inf, l)

    exps = [jnp.exp(v - vals[0]) for v in vals]
    denom = exps[0] + exps[1] + exps[2] + exps[3]
    for k in range(TOP_K):
        gate_ref[pl.ds(k, 1), :] = exps[k] / denom

    oh = onehots[0] + onehots[1] + onehots[2] + onehots[3]
    tt = lax.broadcasted_iota(jnp.int32, (tm, tm), 0)
    uu = lax.broadcasted_iota(jnp.int32, (tm, tm), 1)
    upper = jnp.where(tt < uu, 1.0, 0.0).astype(BF16)
    carry = carry_scr[...]
    before = jnp.dot(oh.astype(BF16), upper, preferred_element_type=F32) + carry[:, 0:1]
    for k in range(TOP_K):
        rank_ref[pl.ds(k, 1), :] = jnp.sum(onehots[k] * before, axis=0, keepdims=True).astype(jnp.int32)
    carry = carry + jnp.sum(oh, axis=1, keepdims=True)
    carry_scr[...] = carry
    cnt_ref[...] = carry


def _router(hs1, norm_w, router_w, router_b, is_token, *, tm, sub):
    T, D = hs1.shape
    E = router_w.shape[1]
    rw_t = router_w.T.astype(F32)
    rw_hi = rw_t.astype(BF16)
    rw_lo = (rw_t - rw_hi.astype(F32)).astype(BF16)
    rw_stack = jnp.concatenate([rw_hi, rw_lo], axis=0)
    n = T // tm
    return pl.pallas_call(
        functools.partial(_router_kernel, sub=sub),
        out_shape=(
            jax.ShapeDtypeStruct((T, D // 2), jnp.uint32),
            jax.ShapeDtypeStruct((TOP_K, T), jnp.int32),
            jax.ShapeDtypeStruct((TOP_K, T), F32),
            jax.ShapeDtypeStruct((TOP_K, T), jnp.int32),
            jax.ShapeDtypeStruct((E, CHUNK), F32),
        ),
        grid=(n,),
        in_specs=[
            pl.BlockSpec((tm, D), lambda i: (i, 0)),
            pl.BlockSpec((1, D), lambda i: (0, 0)),
            pl.BlockSpec((2 * E, D), lambda i: (0, 0)),
            pl.BlockSpec((E, D), lambda i: (0, 0)),
            pl.BlockSpec((E, 1), lambda i: (0, 0)),
            pl.BlockSpec((1, tm), lambda i: (0, i)),
        ],
        out_specs=(
            pl.BlockSpec((tm, D // 2), lambda i: (i, 0)),
            pl.BlockSpec((TOP_K, tm), lambda i: (0, i)),
            pl.BlockSpec((TOP_K, tm), lambda i: (0, i)),
            pl.BlockSpec((TOP_K, tm), lambda i: (0, i)),
            pl.BlockSpec((E, CHUNK), lambda i: (0, 0)),
        ),
        scratch_shapes=[pltpu.VMEM((tm, D), BF16), pltpu.VMEM((tm, D), BF16), pltpu.VMEM((E, CHUNK), F32)],
        compiler_params=_params(("arbitrary",)),
        name="router",
    )(hs1, norm_w.reshape(1, D), rw_stack, rw_hi, router_b.reshape(E, 1).astype(F32),
      is_token.astype(F32).reshape(1, T))


def _dispatch_kernel(pend_ref, padded_ref, nu_ref, dest_ref, fp_ref, zeros_ref, xs_hbm, zsem, sem, *, moe_tm):
    i = pl.program_id(0)
    tm = fp_ref.shape[0]
    E = pend_ref.shape[0]
    n_blocks = xs_hbm.shape[0] // moe_tm

    def zero_rows(start):
        return pltpu.make_async_copy(zeros_ref, xs_hbm.at[pl.ds(pl.multiple_of(start, moe_tm), moe_tm)], zsem)

    @pl.when(i == 0)
    def _():
        for e in range(E):
            @pl.when(padded_ref[e] > 0)
            def _():
                zero_rows(pend_ref[e] - moe_tm).start()

        def start_tail(b, c):
            zero_rows(b * moe_tm).start()
            return c
        lax.fori_loop(nu_ref[0], n_blocks, start_tail, 0)
        for e in range(E):
            @pl.when(padded_ref[e] > 0)
            def _():
                zero_rows(pend_ref[e] - moe_tm).wait()

        def wait_tail(b, c):
            zero_rows(b * moe_tm).wait()
            return c
        lax.fori_loop(nu_ref[0], n_blocks, wait_tail, 0)

    def body(r, c):
        for k in range(TOP_K):
            pltpu.make_async_copy(fp_ref.at[pl.ds(r, 1)], xs_hbm.at[pl.ds(dest_ref[k, r], 1)], sem).start()
        return c
    lax.fori_loop(0, tm, body, 0, unroll=2)
    for k in range(TOP_K):
        pltpu.make_async_copy(fp_ref, xs_hbm.at[pl.ds(0, tm)], sem).wait()


def _dispatch(fp, dest3, pend, padded, n_used, *, rows_total, moe_tm, tile):
    T, W = fp.shape
    assert rows_total % moe_tm == 0
    zeros = jnp.zeros((moe_tm, W), fp.dtype)
    grid_spec = pltpu.PrefetchScalarGridSpec(
        num_scalar_prefetch=3,
        grid=(T // tile,),
        in_specs=[
            pl.BlockSpec((None, TOP_K, tile), lambda i, *_: (i, 0, 0), memory_space=pltpu.SMEM),
            pl.BlockSpec((tile, W), lambda i, *_: (i, 0)),
            pl.BlockSpec((moe_tm, W), lambda i, *_: (0, 0)),
        ],
        out_specs=pl.BlockSpec(memory_space=pl.ANY),
        scratch_shapes=[pltpu.SemaphoreType.DMA, pltpu.SemaphoreType.DMA],
    )
    return pl.pallas_call(
        functools.partial(_dispatch_kernel, moe_tm=moe_tm),
        out_shape=jax.ShapeDtypeStruct((rows_total, W), fp.dtype),
        grid_spec=grid_spec,
        compiler_params=_params(("arbitrary",)),
        name="dispatch",
    )(pend, padded, n_used, dest3, fp, zeros)


def _expert_rows_pipeline(rs_ref, pr_ref, nx_ref, x_hbm, o_hbm, xbuf, obuf, zbuf, isem, osem, zsem, st,
                          *, col0, compute):
    big, small = GROUP_SIZES[0], GROUP_SIZES[-1]
    j = pl.program_id(0)
    e = pl.program_id(1)
    nj = pl.num_programs(0)
    E = pl.num_programs(1)
    tn = obuf.shape[2]
    first_step = jnp.logical_and(j == 0, e == 0)
    last_step = jnp.logical_and(j == nj - 1, e == E - 1)

    def items_of(expert):
        pr = pr_ref[expert]
        n_full = pr // big
        rem = pr - n_full * big
        n_items = n_full
        for size in GROUP_SIZES[1:]:
            n_items = n_items + (rem // size) % 2
        return n_full, rem, n_items

    def item(t, n_full, rem):
        cls = jnp.int32(0)
        off = jnp.minimum(t, n_full) * big
        seen = jnp.int32(0)
        before = jnp.int32(0)
        for c, size in enumerate(GROUP_SIZES[1:], start=1):
            has = (rem // size) % 2
            here = jnp.logical_and(has == 1, t == n_full + seen)
            cls = jnp.where(here, c, cls)
            off = jnp.where(here, n_full * big + before, off)
            seen = seen + has
            before = before + has * size
        return cls, off

    def in_copy(row0, size, slot):
        rows = pl.ds(pl.multiple_of(row0, small), size)
        return pltpu.make_async_copy(x_hbm.at[rows], xbuf.at[slot, pl.ds(0, size)], isem.at[slot])

    def out_copy(row0, size, slot):
        rows = pl.ds(pl.multiple_of(row0, small), size)
        return pltpu.make_async_copy(obuf.at[slot, pl.ds(0, size)], o_hbm.at[rows, pl.ds(col0, tn)], osem.at[slot])

    def by_size(cls, fn):
        for c, size in enumerate(GROUP_SIZES):
            @pl.when(cls == c)
            def _():
                fn(size)

    def start_first_item(expert, slot):
        n_full, rem, n_items = items_of(expert)
        cls, _ = item(0, n_full, rem)

        @pl.when(n_items > 0)
        def _():
            by_size(cls, lambda size: in_copy(rs_ref[expert], size, slot).start())

    def wait_pending_out(slot):
        by_size(st[1 + slot], lambda size: out_copy(rs_ref[0], size, slot).wait())
        st[1 + slot] = -1

    @pl.when(first_step)
    def _():
        st[0] = 0
        st[1] = -1
        st[2] = -1
        start_first_item(jnp.where(pr_ref[0] > 0, 0, nx_ref[0]), 0)

    rs = rs_ref[e]
    n_full, rem, n_items = items_of(e)
    nxt = nx_ref[e]
    stream_continues = jnp.logical_not(jnp.logical_and(nxt <= e, j == nj - 1))

    def body(t, carry):
        g = st[0]
        slot = g % 2
        cls, off = item(t, n_full, rem)
        by_size(cls, lambda size: in_copy(rs + off, size, slot).wait())

        @pl.when(t + 1 < n_items)
        def _():
            cls1, off1 = item(t + 1, n_full, rem)
            by_size(cls1, lambda size: in_copy(rs + off1, size, 1 - slot).start())

        @pl.when(jnp.logical_and(t + 1 == n_items, stream_continues))
        def _():
            start_first_item(nxt, 1 - slot)

        wait_pending_out(slot)

        def run(size):
            compute(size, slot)
            out_copy(rs + off, size, slot).start()
        by_size(cls, run)
        st[1 + slot] = cls
        st[0] = g + 1
        return carry
    lax.fori_loop(0, n_items, body, 0)

    @pl.when(last_step)
    def _():
        wait_pending_out(0)
        wait_pending_out(1)

    @pl.when(e == E - 1)
    def _():
        zbuf[...] = jnp.zeros_like(zbuf)
        first = (rs + pr_ref[e]) // small
        last = o_hbm.shape[0] // small

        def fill_copy(b):
            rows = pl.ds(pl.multiple_of(b * small, small), small)
            return pltpu.make_async_copy(zbuf, o_hbm.at[rows, pl.ds(col0, tn)], zsem)

        def start_fill(b, c):
            fill_copy(b).start()
            return c

        def wait_fill(b, c):
            fill_copy(b).wait()
            return c
        lax.fori_loop(first, last, start_fill, 0)
        lax.fori_loop(first, last, wait_fill, 0)


def _next_nonempty(padded):
    E = padded.shape[0]
    ids = jnp.arange(E, dtype=jnp.int32)
    dist = jnp.where(padded[None, :] > 0, (ids[None, :] - ids[:, None] - 1) % E, E)
    return ((ids + 1 + jnp.min(dist, axis=1)) % E).astype(jnp.int32)


def _expert_scratch(big, width_in, dtype_in, tn, dtype_out):
    small = GROUP_SIZES[-1]
    return [pltpu.VMEM((2, big, width_in), dtype_in), pltpu.VMEM((2, big, tn), dtype_out),
            pltpu.VMEM((small, tn), dtype_out),
            pltpu.SemaphoreType.DMA((2,)), pltpu.SemaphoreType.DMA((2,)), pltpu.SemaphoreType.DMA,
            pltpu.SMEM((3,), jnp.int32)]


def _gate_up_kernel(rs_ref, pr_ref, nx_ref, xs_hbm, wg_ref, wu_ref, bg_ref, bu_ref, h_hbm,
                    xbuf, obuf, zbuf, isem, osem, zsem, st):
    tn = obuf.shape[2]

    def compute(size, slot):
        lo, hi = _unpack_bf16_pair(xbuf[slot, pl.ds(0, size), :])
        x = jnp.concatenate([lo, hi], axis=-1)
        gate = jnp.dot(x, wg_ref[...].astype(BF16), preferred_element_type=F32) + bg_ref[...]
        up = jnp.dot(x, wu_ref[...].astype(BF16), preferred_element_type=F32) + bu_ref[...]
        gate = jnp.minimum(gate, SWIGLU_LIMIT)
        up = jnp.clip(up, -SWIGLU_LIMIT, SWIGLU_LIMIT)
        obuf[slot, pl.ds(0, size), :] = ((up + 1.0) * gate * jax.nn.sigmoid(SWIGLU_ALPHA * gate)).astype(obuf.dtype)

    _expert_rows_pipeline(rs_ref, pr_ref, nx_ref, xs_hbm, h_hbm, xbuf, obuf, zbuf, isem, osem, zsem, st,
                          col0=pl.multiple_of(pl.program_id(0) * tn, tn), compute=compute)


def _gate_up(xs, row_start, row_count, w_gate_up, b_gate_up, *, rows, tn):
    E, D, F2 = w_gate_up.shape
    FF = F2 // 2
    nj = FF // tn
    W = xs.shape[1]
    grid_spec = pltpu.PrefetchScalarGridSpec(
        num_scalar_prefetch=3,
        grid=(nj, E),
        in_specs=[
            pl.BlockSpec(memory_space=pl.ANY),
            pl.BlockSpec((None, D, tn), lambda j, e, *_: (e, 0, j)),
            pl.BlockSpec((None, D, tn), lambda j, e, *_: (e, 0, nj + j)),
            pl.BlockSpec((None, 1, tn), lambda j, e, *_: (e, 0, j)),
            pl.BlockSpec((None, 1, tn), lambda j, e, *_: (e, 0, nj + j)),
        ],
        out_specs=pl.BlockSpec(memory_space=pl.ANY),
        scratch_shapes=_expert_scratch(GROUP_SIZES[0], W, xs.dtype, tn, BF16),
    )
    b3 = b_gate_up.reshape(E, 1, F2)
    return pl.pallas_call(
        _gate_up_kernel,
        out_shape=jax.ShapeDtypeStruct((rows, FF), BF16),
        grid_spec=grid_spec,
        compiler_params=_params(("arbitrary", "arbitrary")),
        name="gate_up",
    )(row_start, row_count, _next_nonempty(row_count), xs, w_gate_up, w_gate_up, b3, b3)


def _down_kernel(rs_ref, pr_ref, nx_ref, h_hbm, w_ref, b_ref, y_hbm, xbuf, obuf, zbuf, isem, osem, zsem, st):
    tn = obuf.shape[2]

    def compute(size, slot):
        hrows = xbuf[slot, pl.ds(0, size), :]
        y = jnp.dot(hrows, w_ref[...].astype(BF16), preferred_element_type=F32) + b_ref[...]
        obuf[slot, pl.ds(0, size), :] = _pack_bf16_pair(y[:, :tn], y[:, tn:])

    _expert_rows_pipeline(rs_ref, pr_ref, nx_ref, h_hbm, y_hbm, xbuf, obuf, zbuf, isem, osem, zsem, st,
                          col0=pl.multiple_of(pl.program_id(0) * tn, tn), compute=compute)


def _down(h, row_start, row_count, w_down, b_down, *, tn):
    E, FF, D = w_down.shape
    rows = h.shape[0]
    grid_spec = pltpu.PrefetchScalarGridSpec(
        num_scalar_prefetch=3,
        grid=(D // tn, E),
        in_specs=[
            pl.BlockSpec(memory_space=pl.ANY),
            pl.BlockSpec((None, FF, tn), lambda j, e, *_: (e, 0, j)),
            pl.BlockSpec((None, 1, tn), lambda j, e, *_: (e, 0, j)),
        ],
        out_specs=pl.BlockSpec(memory_space=pl.ANY),
        scratch_shapes=_expert_scratch(GROUP_SIZES[0], FF, h.dtype, tn // 2, jnp.uint32),
    )
    return pl.pallas_call(
        _down_kernel,
        out_shape=jax.ShapeDtypeStruct((rows, D // 2), jnp.uint32),
        grid_spec=grid_spec,
        compiler_params=_params(("arbitrary", "arbitrary")),
        name="down",
    )(row_start, row_count, _next_nonempty(row_count), h, w_down, b_down.reshape(E, 1, D))


def _combine_kernel(dest_ref, dnext_ref, hs_ref, gate_ref, nw_ref, y_hbm, o_ref, ybuf, sem, *, sub, half):
    s = pl.program_id(0)
    n = pl.num_programs(0)
    tm, D = hs_ref.shape
    W = ybuf.shape[3]
    slot = s % 2
    lanes = CHUNK
    n_chunks = W // lanes
    per_chunk = sub * TOP_K // n_chunks
    assert per_chunk * n_chunks == sub * TOP_K and half % lanes == 0

    def row_copy(d_ref, r, k, to_slot):
        return pltpu.make_async_copy(y_hbm.at[pl.ds(d_ref[k, r], 1)], ybuf.at[to_slot, k, pl.ds(r, 1)],
                                     sem.at[to_slot])

    def wait_slot(which):
        for k in range(TOP_K):
            pltpu.make_async_copy(y_hbm.at[pl.ds(0, tm)], ybuf.at[which, k], sem.at[which]).wait()

    @pl.when(s == 0)
    def _():
        def body(r, c):
            for k in range(TOP_K):
                row_copy(dest_ref, r, k, 0).start()
            return c
        lax.fori_loop(0, tm, body, 0)

    wait_slot(slot)

    def body(rb, c):
        row0 = pl.multiple_of(rb * sub, sub)
        rows = pl.ds(row0, sub)
        g = gate_ref[rows, :]
        gk = [jnp.broadcast_to(g[:, k:k + 1], (sub, lanes)) for k in range(TOP_K)]
        ssq = jnp.zeros((sub, lanes), F32)
        for ch in range(n_chunks):
            for i in range(ch * per_chunk, (ch + 1) * per_chunk):
                row_copy(dnext_ref, row0 + i // TOP_K, i % TOP_K, 1 - slot).start()
            j, within = divmod(ch * lanes, half)
            lo_cols = slice(2 * j * half + within, 2 * j * half + within + lanes)
            hi_cols = slice((2 * j + 1) * half + within, (2 * j + 1) * half + within + lanes)
            lo = hs_ref[rows, lo_cols]
            hi = hs_ref[rows, hi_cols]
            for k in range(TOP_K):
                word = ybuf[slot, k, rows, ch * lanes:(ch + 1) * lanes]
                lo = lo + lax.bitcast_convert_type(word << jnp.uint32(16), F32) * gk[k]
                hi = hi + lax.bitcast_convert_type(word & jnp.uint32(0xFFFF0000), F32) * gk[k]
            o_ref[rows, lo_cols] = lo
            o_ref[rows, hi_cols] = hi
            ssq = ssq + lo * lo + hi * hi
        ms = jnp.sum(ssq, axis=-1, keepdims=True) / D
        o_ref[rows, :] = o_ref[rows, :] * lax.rsqrt(ms + RMS_EPS) * nw_ref[...]
        return c
    lax.fori_loop(0, tm // sub, body, 0)

    @pl.when(s == n - 1)
    def _():
        wait_slot(1 - slot)


def _combine(hs1, gates_t, dest3, y, final_norm_w, *, B, LP, seq_len, tile, sub, half):
    T, D = hs1.shape
    W = y.shape[1]
    per_seq = LP // tile
    n_seq_tiles = seq_len // tile
    lead = (LP - seq_len) // tile
    n_steps = B * n_seq_tiles

    def tok(s):
        return (s // n_seq_tiles) * per_seq + lead + s % n_seq_tiles

    return pl.pallas_call(
        functools.partial(_combine_kernel, sub=sub, half=half),
        out_shape=jax.ShapeDtypeStruct((B, seq_len, D), F32),
        grid=(n_steps,),
        in_specs=[
            pl.BlockSpec((None, TOP_K, tile), lambda s: (tok(s), 0, 0), memory_space=pltpu.SMEM),
            pl.BlockSpec((None, TOP_K, tile), lambda s: (tok(jnp.minimum(s + 1, n_steps - 1)), 0, 0),
                         memory_space=pltpu.SMEM),
            pl.BlockSpec((tile, D), lambda s: (tok(s), 0)),
            pl.BlockSpec((tile, TOP_K), lambda s: (tok(s), 0)),
            pl.BlockSpec((1, D), lambda s: (0, 0)),
            pl.BlockSpec(memory_space=pl.ANY),
        ],
        out_specs=pl.BlockSpec((None, tile, D), lambda s: (s // n_seq_tiles, s % n_seq_tiles, 0)),
        scratch_shapes=[pltpu.VMEM((2, TOP_K, tile, W), y.dtype), pltpu.SemaphoreType.DMA((2,))],
        compiler_params=_params(("arbitrary",)),
        name="combine",
    )(dest3, dest3, hs1, gates_t, final_norm_w.reshape(1, D), y)


def _forward(x, meta_tokens, norm_mix_w, w_in, ret_log_rate_fwd, ret_log_rate_bwd, pool_w, pool_scale,
             w_out, norm_ffn_w, router_w, router_b, w_gate_up, b_gate_up, w_down, b_down, final_norm_w, tiles):
    B, S, D = x.shape
    depth = norm_mix_w.shape[0]
    assert depth == 1, "single-layer block"
    H = ret_log_rate_fwd.shape[-1]
    n_groups, cg = pool_w.shape[1], pool_w.shape[2]
    PW = n_groups * cg
    RW = w_out.shape[1] - PW
    d = RW // H
    E = router_w.shape[-1]
    assert w_in.shape[-1] == 4 * RW + PW and S % CHUNK == 0
    LP = PAD + N_META + S
    T = B * LP
    t = tiles

    lead = jnp.concatenate([jnp.zeros((PAD, D), x.dtype), meta_tokens.astype(x.dtype)], axis=0)
    hs0, a0 = _prep(x, lead, norm_mix_w[0], n_chunks=t["prep_chunks"], sub=t["norm_rows"])

    proj = _in_proj(a0, w_in[0].astype(BF16), tm=t["in_tm"], tn=t["in_tn"])
    pos = (jnp.arange(LP, dtype=jnp.int32) - PAD).astype(F32)
    inv_freq = ROPE_BASE ** (-jnp.arange(d // 2, dtype=F32) / (d // 2))
    ang = pos[:, None] * inv_freq[None, :]
    log_gamma = jnp.stack([-jnp.exp(ret_log_rate_fwd[0].astype(F32)), -jnp.exp(ret_log_rate_bwd[0].astype(F32))])
    ret = _retention(proj, log_gamma, jnp.cos(ang), jnp.sin(ang), B=B, LP=LP, H=H, d=d, GC=t["ret_group_chunks"],
                     HB=t["ret_heads_per_step"])
    pooled = _pool(proj, pool_w[0].astype(BF16), pool_scale[0].astype(F32), LP=LP, seq_len=N_META + S,
                   u_col0=4 * RW, tm=t["pool_tm"])
    hs1 = _out_proj(ret, pooled, w_out[0].astype(BF16), hs0, tm=t["out_tm"], tn=t["out_tn"])

    row_id = jnp.arange(T, dtype=jnp.int32)
    is_token = (row_id % LP) >= PAD
    fp, eid, gates, rank, cnt = _router(hs1, norm_ffn_w[0], router_w[0], router_b[0], is_token,
                                        tm=t["router_tm"], sub=t["router_rows"])
    align = GROUP_SIZES[-1]
    tile = t["token_tile"]
    dtile = t["dispatch_tile"]
    counts = cnt[:, 0].astype(jnp.int32)
    padded = (counts + align - 1) // align * align
    pend = jnp.cumsum(padded).astype(jnp.int32)
    pstart = pend - padded
    n_assign = B * (N_META + S) * TOP_K
    rows = -(-(n_assign + E * (align - 1)) // align) * align
    n_used = pend[-1:] // align
    spare = rows + jnp.arange(TOP_K, dtype=jnp.int32)[:, None] * dtile + (row_id % dtile)[None, :]
    expert_ids = jnp.arange(E, dtype=jnp.int32)[:, None, None]
    first_row = jnp.sum(jnp.where(eid[None] == expert_ids, pstart[:, None, None], 0), axis=0)
    dest = jnp.where(is_token[None, :], first_row + rank, spare)

    def by_tile(width):
        return dest.reshape(TOP_K, T // width, width).transpose(1, 0, 2)
    dest3 = by_tile(tile)

    spare_rows = -(-(TOP_K * dtile) // align) * align
    xs = _dispatch(fp, by_tile(dtile), pend, padded, n_used, rows_total=rows + spare_rows, moe_tm=align, tile=dtile)
    hmid = _gate_up(xs, pstart, padded, w_gate_up[0], b_gate_up[0], rows=rows, tn=t["gu_tn"])
    y = _down(hmid, pstart, padded, w_down[0], b_down[0], tn=t["down_tn"])
    return _combine(hs1, gates.T, dest3, y, final_norm_w, B=B, LP=LP, seq_len=S, tile=tile, sub=t["combine_rows"],
                    half=t["down_tn"] // 2)


def kernel(x, meta_tokens, norm_mix_w, w_in, ret_log_rate_fwd, ret_log_rate_bwd, pool_w, pool_scale, w_out,
           norm_ffn_w, router_w, router_b, w_gate_up, b_gate_up, w_down, b_down, final_norm_w):
    return _forward(x, meta_tokens, norm_mix_w, w_in, ret_log_rate_fwd, ret_log_rate_bwd, pool_w, pool_scale,
                    w_out, norm_ffn_w, router_w, router_b, w_gate_up, b_gate_up, w_down, b_down, final_norm_w,
                    DEFAULT_TILES)
```

```python
import functools

import jax
import jax.numpy as jnp
from jax import lax
from jax.experimental import pallas as pl
from jax.experimental.pallas import tpu as pltpu

N_META = 16
CHUNK = 128
PAD = CHUNK - N_META
POOL_WINDOWS = (2, 4, 8, 16)
NORM_LANES = 512
POOL_HALO = 64
TOP_K = 4
GROUP_SIZES = (1024, 512, 256, 128)
SWIGLU_LIMIT = 7.0
SWIGLU_ALPHA = 1.702
RMS_EPS = 1e-5
ROPE_BASE = 10000.0

V7X_VMEM_BYTES = 64 * 1024 * 1024
VMEM_LIMIT = V7X_VMEM_BYTES - 2 * 1024 * 1024

F32 = jnp.float32
BF16 = jnp.bfloat16

DEFAULT_TILES = dict(
    prep_chunks=5, in_tm=1280, in_tn=1024, norm_rows=32,
    ret_group_chunks=5, ret_heads_per_step=2,
    pool_tm=640,
    out_tm=1280, out_tn=512,
    router_tm=640, router_rows=32,
    gu_tn=512, down_tn=2048,
    dispatch_tile=256, token_tile=128, combine_rows=32,
)


def _params(semantics):
    return pltpu.CompilerParams(dimension_semantics=semantics, vmem_limit_bytes=VMEM_LIMIT)


def _prep_kernel(*refs, sub, n_chunks, chunks_per_seq):
    x_refs = refs[:n_chunks]
    lead_ref, nw_ref, hs_ref, a_ref = refs[n_chunks:]
    i = pl.program_id(0)
    for c in range(n_chunks):
        is_lead = (i * n_chunks + c) % chunks_per_seq == 0

        def body(r, carry, c=c, is_lead=is_lead):
            rows = pl.ds(pl.multiple_of(r * sub, sub), sub)
            out_rows = pl.ds(pl.multiple_of(c * CHUNK + r * sub, sub), sub)
            x = jnp.where(is_lead, lead_ref[rows, :], x_refs[c][rows, :])
            hs_ref[out_rows, :] = x
            ms = jnp.mean(x * x, axis=-1, keepdims=True)
            a_ref[out_rows, :] = (x * lax.rsqrt(ms + RMS_EPS) * nw_ref[...]).astype(a_ref.dtype)
            return carry
        lax.fori_loop(0, CHUNK // sub, body, 0)


def _prep(x, lead, norm_w, *, n_chunks, sub):
    B, S, D = x.shape
    x_chunks_per_seq = S // CHUNK
    chunks_per_seq = x_chunks_per_seq + 1
    T = B * chunks_per_seq * CHUNK
    assert (B * chunks_per_seq) % n_chunks == 0

    def x_map(c):
        def m(i):
            q = i * n_chunks + c
            return ((q // chunks_per_seq) * x_chunks_per_seq + jnp.maximum(q % chunks_per_seq - 1, 0), 0)
        return m

    return pl.pallas_call(
        functools.partial(_prep_kernel, sub=sub, n_chunks=n_chunks, chunks_per_seq=chunks_per_seq),
        out_shape=(jax.ShapeDtypeStruct((T, D), x.dtype), jax.ShapeDtypeStruct((T, D), BF16)),
        grid=(B * chunks_per_seq // n_chunks,),
        in_specs=[pl.BlockSpec((CHUNK, D), x_map(c)) for c in range(n_chunks)] + [
            pl.BlockSpec((CHUNK, D), lambda i: (0, 0)),
            pl.BlockSpec((1, D), lambda i: (0, 0)),
        ],
        out_specs=(pl.BlockSpec((n_chunks * CHUNK, D), lambda i: (i, 0)),
                   pl.BlockSpec((n_chunks * CHUNK, D), lambda i: (i, 0))),
        compiler_params=_params(("arbitrary",)),
        name="prep",
    )(*([x.reshape(B * S, D)] * n_chunks), lead, norm_w.reshape(1, D))


def _in_proj_kernel(a_ref, w_ref, o_ref):
    o_ref[...] = jnp.dot(a_ref[...], w_ref[...], preferred_element_type=F32).astype(o_ref.dtype)


def _in_proj(a, w_in_bf16, *, tm, tn):
    T, D = a.shape
    N = w_in_bf16.shape[1]
    return pl.pallas_call(
        _in_proj_kernel,
        out_shape=jax.ShapeDtypeStruct((T, N), BF16),
        grid=(T // tm, N // tn),
        in_specs=[
            pl.BlockSpec((tm, D), lambda i, j: (i, 0)),
            pl.BlockSpec((D, tn), lambda i, j: (0, j)),
        ],
        out_specs=pl.BlockSpec((tm, tn), lambda i, j: (i, j)),
        compiler_params=_params(("arbitrary", "arbitrary")),
        name="in_proj",
    )(a, w_in_bf16)


def _retention_kernel(lg_ref, q_ref, k_ref, v_ref, g_ref, cos_ref, sin_ref, o_ref,
                      acc_scr, sf_scr, sb_scr, dmat_scr, dvec_scr, *, G, GC):
    hb = pl.program_id(0)
    s = pl.program_id(1)
    phase = s // G
    w = s % G
    gi = w + phase * (G - 1 - 2 * w)
    C = CHUNK
    d = cos_ref.shape[1] * 2
    half = d // 2
    NB = q_ref.shape[0]
    HB = q_ref.shape[2] // d
    GR = GC * C
    k_scale = d ** -0.5
    nt = (((1,), (1,)), ((), ()))
    tn = (((0,), (0,)), ((), ()))

    @pl.when(s == 0)
    def _():
        t = lax.broadcasted_iota(jnp.int32, (C, C), 0).astype(F32)
        u = lax.broadcasted_iota(jnp.int32, (C, C), 1).astype(F32)
        diff = t - u
        idx = lax.broadcasted_iota(jnp.int32, (C, d), 0).astype(F32)
        for hh in range(HB):
            lgf = lg_ref[0, hb * HB + hh]
            lgb = lg_ref[1, hb * HB + hh]
            dmat_scr[hh] = jnp.where(diff >= 0.0, jnp.exp(lgf * jnp.maximum(diff, 0.0)),
                                     jnp.exp(lgb * jnp.maximum(-diff, 0.0)))
            dvec_scr[hh, 0] = jnp.exp(lgf * (idx + 1.0))
            dvec_scr[hh, 1] = jnp.exp(lgf * (C - 1.0 - idx))
            dvec_scr[hh, 2] = jnp.exp(lgb * (C - idx))
            dvec_scr[hh, 3] = jnp.exp(lgb * idx)
            dvec_scr[hh, 4] = jnp.exp(jnp.full((C, d), lgf * C, F32))
            dvec_scr[hh, 5] = jnp.exp(jnp.full((C, d), lgb * C, F32))
        sf_scr[...] = jnp.zeros_like(sf_scr)

    @pl.when(s == G)
    def _():
        sb_scr[...] = jnp.zeros_like(sb_scr)

    def rotary(ref, bb, rows, hh):
        x = ref[bb, rows, hh * d:(hh + 1) * d].astype(F32)
        x1 = x[:, :half]
        x2 = x[:, half:]
        cs = cos_ref[rows, :]
        sn = sin_ref[rows, :]
        return jnp.concatenate([x1 * cs - x2 * sn, x2 * cs + x1 * sn], axis=-1)

    chains = [(bb, hh) for bb in range(NB) for hh in range(HB)]

    @pl.when(phase == 0)
    def _():
        def body(c, carry):
            rows = pl.ds(pl.multiple_of(c * C, C), C)
            arows = pl.ds(pl.multiple_of(gi * GR + c * C, C), C)
            for ci, (bb, hh) in enumerate(chains):
                cols = slice(hh * d, (hh + 1) * d)
                qc = rotary(q_ref, bb, rows, hh).astype(BF16)
                kf = rotary(k_ref, bb, rows, hh) * k_scale
                vc = v_ref[bb, rows, cols]
                sc = lax.dot_general(qc, kf.astype(BF16), nt, preferred_element_type=F32)
                p = (sc * dmat_scr[hh]).astype(BF16)
                inner = jnp.dot(p, vc, preferred_element_type=F32)
                sf = sf_scr[ci]
                cross = jnp.dot(qc, sf.astype(BF16), preferred_element_type=F32) * dvec_scr[hh, 0]
                acc_scr[bb, arows, cols] = inner + cross
                kd = (kf * dvec_scr[hh, 1]).astype(BF16)
                sf_scr[ci] = sf * dvec_scr[hh, 4, 0:1, :] + lax.dot_general(kd, vc, tn, preferred_element_type=F32)
            return carry
        lax.fori_loop(0, GC, body, 0)

    @pl.when(phase == 1)
    def _():
        def body(cc, carry):
            c = GC - 1 - cc
            rows = pl.ds(pl.multiple_of(c * C, C), C)
            arows = pl.ds(pl.multiple_of(gi * GR + c * C, C), C)
            for ci, (bb, hh) in enumerate(chains):
                cols = slice(hh * d, (hh + 1) * d)
                qc = rotary(q_ref, bb, rows, hh).astype(BF16)
                kf = rotary(k_ref, bb, rows, hh) * k_scale
                vc = v_ref[bb, rows, cols]
                sb = sb_scr[ci]
                r = acc_scr[bb, arows, cols] + jnp.dot(qc, sb.astype(BF16), preferred_element_type=F32) * dvec_scr[hh, 2]
                kd = (kf * dvec_scr[hh, 3]).astype(BF16)
                sb_scr[ci] = sb * dvec_scr[hh, 5, 0:1, :] + lax.dot_general(kd, vc, tn, preferred_element_type=F32)
                rn = r * lax.rsqrt(jnp.mean(r * r, axis=-1, keepdims=True) + RMS_EPS)
                gg = g_ref[bb, rows, cols].astype(F32)
                o_ref[bb, rows, cols] = (gg * jax.nn.sigmoid(gg) * rn).astype(o_ref.dtype)
            return carry
        lax.fori_loop(0, GC, body, 0)


def _retention(proj, log_gamma, cos_tab, sin_tab, *, B, LP, H, d, GC, HB):
    T, N = proj.shape
    GR = GC * CHUNK
    G = LP // GR
    assert H % HB == 0
    H = H // HB
    bw = HB * d
    proj3 = proj.reshape(B, LP, N)

    def grp(s):
        phase = s // G
        w = s % G
        return phase, w + phase * (G - 1 - 2 * w)

    def qkv_map(col0):
        def m(h, s):
            _, gi = grp(s)
            return (0, gi, col0 + h)
        return m

    def late_map(col0):
        def m(h, s):
            phase, gi = grp(s)
            return (0, jnp.where(phase == 0, G - 1, gi), col0 + h)
        return m

    def tab_map(h, s):
        _, gi = grp(s)
        return (gi, 0)

    ret = pl.pallas_call(
        functools.partial(_retention_kernel, G=G, GC=GC),
        out_shape=jax.ShapeDtypeStruct((B, LP, H * bw), BF16),
        grid=(H, 2 * G),
        in_specs=[
            pl.BlockSpec(memory_space=pltpu.SMEM),
            pl.BlockSpec((B, GR, bw), qkv_map(0)),
            pl.BlockSpec((B, GR, bw), qkv_map(H)),
            pl.BlockSpec((B, GR, bw), qkv_map(2 * H)),
            pl.BlockSpec((B, GR, bw), late_map(3 * H)),
            pl.BlockSpec((GR, d // 2), tab_map),
            pl.BlockSpec((GR, d // 2), tab_map),
        ],
        out_specs=pl.BlockSpec((B, GR, bw), late_map(0)),
        scratch_shapes=[
            pltpu.VMEM((B, LP, bw), F32),
            pltpu.VMEM((B * HB, d, d), F32),
            pltpu.VMEM((B * HB, d, d), F32),
            pltpu.VMEM((HB, CHUNK, CHUNK), F32),
            pltpu.VMEM((HB, 6, CHUNK, d), F32),
        ],
        compiler_params=_params(("arbitrary", "arbitrary")),
        name="retention",
    )(log_gamma, proj3, proj3, proj3, proj3, cos_tab, sin_tab)
    return ret.reshape(T, H * bw)


def _pool_kernel(prev_ref, cur_ref, next_ref, pw_ref, ps_ref, o_ref, *, tiles_per_seq, seq_len):
    i = pl.program_id(0)
    tm = cur_ref.shape[0]
    ext = tm + 2 * POOL_HALO
    n_groups = pw_ref.shape[0]
    cg = pw_ref.shape[1]
    row0 = (i % tiles_per_seq) * tm - PAD
    pos_t = row0 + lax.broadcasted_iota(jnp.int32, (tm, ext), 0)
    pos_s = row0 - POOL_HALO + lax.broadcasted_iota(jnp.int32, (tm, ext), 1)
    pos_c = row0 + lax.broadcasted_iota(jnp.int32, (tm, 1), 0)
    for gi in range(n_groups):
        win = POOL_WINDOWS[gi]
        cols = slice(gi * cg, (gi + 1) * cg)
        lo = jnp.clip(pos_t - win // 2, 0, seq_len)
        hi = jnp.clip(pos_t + (win - win // 2), 0, seq_len)
        band = jnp.where(pos_s >= lo, jnp.where(pos_s < hi, 1.0, 0.0), 0.0).astype(BF16)
        cnt = (jnp.clip(pos_c + (win - win // 2), 0, seq_len) - jnp.clip(pos_c - win // 2, 0, seq_len)).astype(F32)
        inv = jnp.where(pos_c >= 0, 1.0 / jnp.maximum(cnt, 1.0), 0.0)
        u_cur = cur_ref[:, cols]
        u_ext = jnp.concatenate([prev_ref[:, cols], u_cur, next_ref[:, cols]], axis=0)
        wsum = jnp.dot(band, u_ext, preferred_element_type=F32)
        pooled = wsum * inv - u_cur.astype(F32)
        y = jnp.dot(pooled.astype(BF16), pw_ref[gi], preferred_element_type=F32)
        o_ref[:, cols] = (y * ps_ref[:, cols]).astype(o_ref.dtype)


def _pool(proj, pool_w_bf16, pool_scale, *, LP, seq_len, u_col0, tm):
    T = proj.shape[0]
    n_groups, cg, _ = pool_w_bf16.shape
    PW = n_groups * cg
    col_blk = u_col0 // PW
    hb = tm // POOL_HALO
    n_halo = T // POOL_HALO
    return pl.pallas_call(
        functools.partial(_pool_kernel, tiles_per_seq=LP // tm, seq_len=seq_len),
        out_shape=jax.ShapeDtypeStruct((T, PW), BF16),
        grid=(T // tm,),
        in_specs=[
            pl.BlockSpec((POOL_HALO, PW), lambda i: (jnp.maximum(i * hb - 1, 0), col_blk)),
            pl.BlockSpec((tm, PW), lambda i: (i, col_blk)),
            pl.BlockSpec((POOL_HALO, PW), lambda i: (jnp.minimum((i + 1) * hb, n_halo - 1), col_blk)),
            pl.BlockSpec((n_groups, cg, cg), lambda i: (0, 0, 0)),
            pl.BlockSpec((1, PW), lambda i: (0, 0)),
        ],
        out_specs=pl.BlockSpec((tm, PW), lambda i: (i, 0)),
        compiler_params=_params(("arbitrary",)),
        name="pool",
    )(proj, proj, proj, pool_w_bf16, pool_scale.reshape(1, PW))


def _out_proj_kernel(ret_ref, pool_ref, wr_ref, wp_ref, hs_ref, o_ref):
    mixed = jnp.concatenate([ret_ref[...], pool_ref[...]], axis=-1)
    w = jnp.concatenate([wr_ref[...], wp_ref[...]], axis=0).astype(BF16)
    o_ref[...] = hs_ref[...] + jnp.dot(mixed, w, preferred_element_type=F32)


def _out_proj(ret, pool, w_out, hs, *, tm, tn):
    T, RW = ret.shape
    PW = pool.shape[1]
    D = hs.shape[1]
    assert RW == PW, "the two mixer halves share one row-block split of w_out"
    return pl.pallas_call(
        _out_proj_kernel,
        out_shape=jax.ShapeDtypeStruct((T, D), F32),
        grid=(T // tm, D // tn),
        in_specs=[
            pl.BlockSpec((tm, RW), lambda i, j: (i, 0)),
            pl.BlockSpec((tm, PW), lambda i, j: (i, 0)),
            pl.BlockSpec((RW, tn), lambda i, j: (0, j)),
            pl.BlockSpec((PW, tn), lambda i, j: (1, j)),
            pl.BlockSpec((tm, tn), lambda i, j: (i, j)),
        ],
        out_specs=pl.BlockSpec((tm, tn), lambda i, j: (i, j)),
        compiler_params=_params(("arbitrary", "arbitrary")),
        name="out_proj",
    )(ret, pool, w_out, w_out, hs)


def _pack_rounded_pair(lo, hi):
    lo_bits = lax.bitcast_convert_type(lo, jnp.uint32)
    hi_bits = lax.bitcast_convert_type(hi, jnp.uint32)
    return (hi_bits & jnp.uint32(0xFFFF0000)) | (lo_bits >> jnp.uint32(16))


def _pack_bf16_pair(lo, hi):
    return _pack_rounded_pair(lo.astype(BF16).astype(F32), hi.astype(BF16).astype(F32))


def _unpack_bf16_pair(word):
    lo = lax.bitcast_convert_type(word << jnp.uint32(16), F32).astype(BF16)
    hi = lax.bitcast_convert_type(word & jnp.uint32(0xFFFF0000), F32).astype(BF16)
    return lo, hi


def _router_kernel(hs_ref, nw_ref, rw_ref, rwhi_ref, rb_ref, valid_ref,
                   fp_ref, eid_ref, gate_ref, rank_ref, cnt_ref,
                   fhi_scr, flo_scr, carry_scr, *, sub):
    i = pl.program_id(0)
    tm, D = hs_ref.shape
    E = rb_ref.shape[0]
    half = D // 2

    @pl.when(i == 0)
    def _():
        carry_scr[...] = jnp.zeros_like(carry_scr)

    cw = min(NORM_LANES, half)

    def body(r, c):
        rows = pl.ds(pl.multiple_of(r * sub, sub), sub)
        sq = jnp.zeros((sub, cw), F32)
        for c0 in range(0, D, cw):
            x = hs_ref[rows, c0:c0 + cw]
            sq = sq + x * x
        scale = lax.rsqrt(jnp.sum(sq, axis=-1, keepdims=True) / D + RMS_EPS)
        for c0 in range(0, half, cw):
            rounded = []
            for cols in (slice(c0, c0 + cw), slice(half + c0, half + c0 + cw)):
                f = hs_ref[rows, cols] * scale * nw_ref[:, cols]
                f_hi = f.astype(BF16)
                f_rounded = f_hi.astype(F32)
                fhi_scr[rows, cols] = f_hi
                flo_scr[rows, cols] = (f - f_rounded).astype(BF16)
                rounded.append(f_rounded)
            fp_ref[rows, c0:c0 + cw] = _pack_rounded_pair(rounded[0], rounded[1])
        return c
    lax.fori_loop(0, tm // sub, body, 0)

    nt = (((1,), (1,)), ((), ()))
    part = lax.dot_general(rw_ref[...], fhi_scr[...], nt, preferred_element_type=F32)
    logits = part[:E] + part[E:] + lax.dot_general(rwhi_ref[...], flo_scr[...], nt, preferred_element_type=F32)
    logits = logits + rb_ref[...]

    eio = lax.broadcasted_iota(jnp.int32, (E, tm), 0).astype(F32)
    valid = valid_ref[...]

    vals, onehots = [], []
    l = logits
    for k in range(TOP_K):
        m = jnp.max(l, axis=0, keepdims=True)
        idx = jnp.min(jnp.where(l == m, eio, float(E)), axis=0, keepdims=True)
        sel = eio == idx
        vals.append(m)
        onehots.append(jnp.where(sel, valid, 0.0))
        eid_ref[pl.ds(k, 1), :] = idx.astype(jnp.int32)
        l = jnp.where(sel, -jnp.inf, l)

    exps = [jnp.exp(v - vals[0]) for v in vals]
    denom = exps[0] + exps[1] + exps[2] + exps[3]
    for k in range(TOP_K):
        gate_ref[pl.ds(k, 1), :] = exps[k] / denom

    oh = onehots[0] + onehots[1] + onehots[2] + onehots[3]
    tt = lax.broadcasted_iota(jnp.int32, (tm, tm), 0)
    uu = lax.broadcasted_iota(jnp.int32, (tm, tm), 1)
    upper = jnp.where(tt < uu, 1.0, 0.0).astype(BF16)
    carry = carry_scr[...]
    before = jnp.dot(oh.astype(BF16), upper, preferred_element_type=F32) + carry[:, 0:1]
    for k in range(TOP_K):
        rank_ref[pl.ds(k, 1), :] = jnp.sum(onehots[k] * before, axis=0, keepdims=True).astype(jnp.int32)
    carry = carry + jnp.sum(oh, axis=1, keepdims=True)
    carry_scr[...] = carry
    cnt_ref[...] = carry


def _router(hs1, norm_w, router_w, router_b, is_token, *, tm, sub):
    T, D = hs1.shape
    E = router_w.shape[1]
    rw_t = router_w.T.astype(F32)
    rw_hi = rw_t.astype(BF16)
    rw_lo = (rw_t - rw_hi.astype(F32)).astype(BF16)
    rw_stack = jnp.concatenate([rw_hi, rw_lo], axis=0)
    n = T // tm
    return pl.pallas_call(
        functools.partial(_router_kernel, sub=sub),
        out_shape=(
            jax.ShapeDtypeStruct((T, D // 2), jnp.uint32),
            jax.ShapeDtypeStruct((TOP_K, T), jnp.int32),
            jax.ShapeDtypeStruct((TOP_K, T), F32),
            jax.ShapeDtypeStruct((TOP_K, T), jnp.int32),
            jax.ShapeDtypeStruct((E, CHUNK), F32),
        ),
        grid=(n,),
        in_specs=[
            pl.BlockSpec((tm, D), lambda i: (i, 0)),
            pl.BlockSpec((1, D), lambda i: (0, 0)),
            pl.BlockSpec((2 * E, D), lambda i: (0, 0)),
            pl.BlockSpec((E, D), lambda i: (0, 0)),
            pl.BlockSpec((E, 1), lambda i: (0, 0)),
            pl.BlockSpec((1, tm), lambda i: (0, i)),
        ],
        out_specs=(
            pl.BlockSpec((tm, D // 2), lambda i: (i, 0)),
            pl.BlockSpec((TOP_K, tm), lambda i: (0, i)),
            pl.BlockSpec((TOP_K, tm), lambda i: (0, i)),
            pl.BlockSpec((TOP_K, tm), lambda i: (0, i)),
            pl.BlockSpec((E, CHUNK), lambda i: (0, 0)),
        ),
        scratch_shapes=[pltpu.VMEM((tm, D), BF16), pltpu.VMEM((tm, D), BF16), pltpu.VMEM((E, CHUNK), F32)],
        compiler_params=_params(("arbitrary",)),
        name="router",
    )(hs1, norm_w.reshape(1, D), rw_stack, rw_hi, router_b.reshape(E, 1).astype(F32),
      is_token.astype(F32).reshape(1, T))


def _dispatch_kernel(pend_ref, padded_ref, nu_ref, dest_ref, fp_ref, zeros_ref, xs_hbm, zsem, sem, *, moe_tm):
    i = pl.program_id(0)
    tm = fp_ref.shape[0]
    E = pend_ref.shape[0]
    n_blocks = xs_hbm.shape[0] // moe_tm

    def zero_rows(start):
        return pltpu.make_async_copy(zeros_ref, xs_hbm.at[pl.ds(pl.multiple_of(start, moe_tm), moe_tm)], zsem)

    @pl.when(i == 0)
    def _():
        for e in range(E):
            @pl.when(padded_ref[e] > 0)
            def _():
                zero_rows(pend_ref[e] - moe_tm).start()

        def start_tail(b, c):
            zero_rows(b * moe_tm).start()
            return c
        lax.fori_loop(nu_ref[0], n_blocks, start_tail, 0)
        for e in range(E):
            @pl.when(padded_ref[e] > 0)
            def _():
                zero_rows(pend_ref[e] - moe_tm).wait()

        def wait_tail(b, c):
            zero_rows(b * moe_tm).wait()
            return c
        lax.fori_loop(nu_ref[0], n_blocks, wait_tail, 0)

    def body(r, c):
        for k in range(TOP_K):
            pltpu.make_async_copy(fp_ref.at[pl.ds(r, 1)], xs_hbm.at[pl.ds(dest_ref[k, r], 1)], sem).start()
        return c
    lax.fori_loop(0, tm, body, 0, unroll=4)
    for k in range(TOP_K):
        pltpu.make_async_copy(fp_ref, xs_hbm.at[pl.ds(0, tm)], sem).wait()


def _dispatch(fp, dest3, pend, padded, n_used, *, rows_total, moe_tm, tile):
    T, W = fp.shape
    assert rows_total % moe_tm == 0
    zeros = jnp.zeros((moe_tm, W), fp.dtype)
    grid_spec = pltpu.PrefetchScalarGridSpec(
        num_scalar_prefetch=3,
        grid=(T // tile,),
        in_specs=[
            pl.BlockSpec((None, TOP_K, tile), lambda i, *_: (i, 0, 0), memory_space=pltpu.SMEM),
            pl.BlockSpec((tile, W), lambda i, *_: (i, 0)),
            pl.BlockSpec((moe_tm, W), lambda i, *_: (0, 0)),
        ],
        out_specs=pl.BlockSpec(memory_space=pl.ANY),
        scratch_shapes=[pltpu.SemaphoreType.DMA, pltpu.SemaphoreType.DMA],
    )
    return pl.pallas_call(
        functools.partial(_dispatch_kernel, moe_tm=moe_tm),
        out_shape=jax.ShapeDtypeStruct((rows_total, W), fp.dtype),
        grid_spec=grid_spec,
        compiler_params=_params(("arbitrary",)),
        name="dispatch",
    )(pend, padded, n_used, dest3, fp, zeros)


def _expert_rows_pipeline(rs_ref, pr_ref, nx_ref, x_hbm, o_hbm, xbuf, obuf, zbuf, isem, osem, zsem, st,
                          *, col0, compute):
    big, small = GROUP_SIZES[0], GROUP_SIZES[-1]
    j = pl.program_id(0)
    e = pl.program_id(1)
    nj = pl.num_programs(0)
    E = pl.num_programs(1)
    tn = obuf.shape[2]
    first_step = jnp.logical_and(j == 0, e == 0)
    last_step = jnp.logical_and(j == nj - 1, e == E - 1)

    def items_of(expert):
        pr = pr_ref[expert]
        n_full = pr // big
        rem = pr - n_full * big
        n_items = n_full
        for size in GROUP_SIZES[1:]:
            n_items = n_items + (rem // size) % 2
        return n_full, rem, n_items

    def item(t, n_full, rem):
        cls = jnp.int32(0)
        off = jnp.minimum(t, n_full) * big
        seen = jnp.int32(0)
        before = jnp.int32(0)
        for c, size in enumerate(GROUP_SIZES[1:], start=1):
            has = (rem // size) % 2
            here = jnp.logical_and(has == 1, t == n_full + seen)
            cls = jnp.where(here, c, cls)
            off = jnp.where(here, n_full * big + before, off)
            seen = seen + has
            before = before + has * size
        return cls, off

    def in_copy(row0, size, slot):
        rows = pl.ds(pl.multiple_of(row0, small), size)
        return pltpu.make_async_copy(x_hbm.at[rows], xbuf.at[slot, pl.ds(0, size)], isem.at[slot])

    def out_copy(row0, size, slot):
        rows = pl.ds(pl.multiple_of(row0, small), size)
        return pltpu.make_async_copy(obuf.at[slot, pl.ds(0, size)], o_hbm.at[rows, pl.ds(col0, tn)], osem.at[slot])

    def by_size(cls, fn):
        for c, size in enumerate(GROUP_SIZES):
            @pl.when(cls == c)
            def _():
                fn(size)

    def start_first_item(expert, slot):
        n_full, rem, n_items = items_of(expert)
        cls, _ = item(0, n_full, rem)

        @pl.when(n_items > 0)
        def _():
            by_size(cls, lambda size: in_copy(rs_ref[expert], size, slot).start())

    def wait_pending_out(slot):
        by_size(st[1 + slot], lambda size: out_copy(rs_ref[0], size, slot).wait())
        st[1 + slot] = -1

    @pl.when(first_step)
    def _():
        st[0] = 0
        st[1] = -1
        st[2] = -1
        start_first_item(jnp.where(pr_ref[0] > 0, 0, nx_ref[0]), 0)

    rs = rs_ref[e]
    n_full, rem, n_items = items_of(e)
    nxt = nx_ref[e]
    stream_continues = jnp.logical_not(jnp.logical_and(nxt <= e, j == nj - 1))

    def body(t, carry):
        g = st[0]
        slot = g % 2
        cls, off = item(t, n_full, rem)
        by_size(cls, lambda size: in_copy(rs + off, size, slot).wait())

        @pl.when(t + 1 < n_items)
        def _():
            cls1, off1 = item(t + 1, n_full, rem)
            by_size(cls1, lambda size: in_copy(rs + off1, size, 1 - slot).start())

        @pl.when(jnp.logical_and(t + 1 == n_items, stream_continues))
        def _():
            start_first_item(nxt, 1 - slot)

        wait_pending_out(slot)

        def run(size):
            compute(size, slot)
            out_copy(rs + off, size, slot).start()
        by_size(cls, run)
        st[1 + slot] = cls
        st[0] = g + 1
        return carry
    lax.fori_loop(0, n_items, body, 0)

    @pl.when(last_step)
    def _():
        wait_pending_out(0)
        wait_pending_out(1)

    @pl.when(e == E - 1)
    def _():
        zbuf[...] = jnp.zeros_like(zbuf)
        first = (rs + pr_ref[e]) // small
        last = o_hbm.shape[0] // small

        def fill_copy(b):
            rows = pl.ds(pl.multiple_of(b * small, small), small)
            return pltpu.make_async_copy(zbuf, o_hbm.at[rows, pl.ds(col0, tn)], zsem)

        def start_fill(b, c):
            fill_copy(b).start()
            return c

        def wait_fill(b, c):
            fill_copy(b).wait()
            return c
        lax.fori_loop(first, last, start_fill, 0)
        lax.fori_loop(first, last, wait_fill, 0)


def _next_nonempty(padded):
    E = padded.shape[0]
    ids = jnp.arange(E, dtype=jnp.int32)
    dist = jnp.where(padded[None, :] > 0, (ids[None, :] - ids[:, None] - 1) % E, E)
    return ((ids + 1 + jnp.min(dist, axis=1)) % E).astype(jnp.int32)


def _expert_scratch(big, width_in, dtype_in, tn, dtype_out):
    small = GROUP_SIZES[-1]
    return [pltpu.VMEM((2, big, width_in), dtype_in), pltpu.VMEM((2, big, tn), dtype_out),
            pltpu.VMEM((small, tn), dtype_out),
            pltpu.SemaphoreType.DMA((2,)), pltpu.SemaphoreType.DMA((2,)), pltpu.SemaphoreType.DMA,
            pltpu.SMEM((3,), jnp.int32)]


def _gate_up_kernel(rs_ref, pr_ref, nx_ref, xs_hbm, wg_ref, wu_ref, bg_ref, bu_ref, h_hbm,
                    xbuf, obuf, zbuf, isem, osem, zsem, st):
    tn = obuf.shape[2]

    def compute(size, slot):
        lo, hi = _unpack_bf16_pair(xbuf[slot, pl.ds(0, size), :])
        x = jnp.concatenate([lo, hi], axis=-1)
        gate = jnp.dot(x, wg_ref[...].astype(BF16), preferred_element_type=F32) + bg_ref[...]
        up = jnp.dot(x, wu_ref[...].astype(BF16), preferred_element_type=F32) + bu_ref[...]
        gate = jnp.minimum(gate, SWIGLU_LIMIT)
        up = jnp.clip(up, -SWIGLU_LIMIT, SWIGLU_LIMIT)
        obuf[slot, pl.ds(0, size), :] = ((up + 1.0) * gate * jax.nn.sigmoid(SWIGLU_ALPHA * gate)).astype(obuf.dtype)

    _expert_rows_pipeline(rs_ref, pr_ref, nx_ref, xs_hbm, h_hbm, xbuf, obuf, zbuf, isem, osem, zsem, st,
                          col0=pl.multiple_of(pl.program_id(0) * tn, tn), compute=compute)


def _gate_up(xs, row_start, row_count, w_gate_up, b_gate_up, *, rows, tn):
    E, D, F2 = w_gate_up.shape
    FF = F2 // 2
    nj = FF // tn
    W = xs.shape[1]
    grid_spec = pltpu.PrefetchScalarGridSpec(
        num_scalar_prefetch=3,
        grid=(nj, E),
        in_specs=[
            pl.BlockSpec(memory_space=pl.ANY),
            pl.BlockSpec((None, D, tn), lambda j, e, *_: (e, 0, j)),
            pl.BlockSpec((None, D, tn), lambda j, e, *_: (e, 0, nj + j)),
            pl.BlockSpec((None, 1, tn), lambda j, e, *_: (e, 0, j)),
            pl.BlockSpec((None, 1, tn), lambda j, e, *_: (e, 0, nj + j)),
        ],
        out_specs=pl.BlockSpec(memory_space=pl.ANY),
        scratch_shapes=_expert_scratch(GROUP_SIZES[0], W, xs.dtype, tn, BF16),
    )
    b3 = b_gate_up.reshape(E, 1, F2)
    return pl.pallas_call(
        _gate_up_kernel,
        out_shape=jax.ShapeDtypeStruct((rows, FF), BF16),
        grid_spec=grid_spec,
        compiler_params=_params(("arbitrary", "arbitrary")),
        name="gate_up",
    )(row_start, row_count, _next_nonempty(row_count), xs, w_gate_up, w_gate_up, b3, b3)


def _down_kernel(rs_ref, pr_ref, nx_ref, h_hbm, w_ref, b_ref, y_hbm, xbuf, obuf, zbuf, isem, osem, zsem, st):
    tn = obuf.shape[2]

    def compute(size, slot):
        hrows = xbuf[slot, pl.ds(0, size), :]
        y = jnp.dot(hrows, w_ref[...].astype(BF16), preferred_element_type=F32) + b_ref[...]
        obuf[slot, pl.ds(0, size), :] = _pack_bf16_pair(y[:, :tn], y[:, tn:])

    _expert_rows_pipeline(rs_ref, pr_ref, nx_ref, h_hbm, y_hbm, xbuf, obuf, zbuf, isem, osem, zsem, st,
                          col0=pl.multiple_of(pl.program_id(0) * tn, tn), compute=compute)


def _down(h, row_start, row_count, w_down, b_down, *, tn):
    E, FF, D = w_down.shape
    rows = h.shape[0]
    grid_spec = pltpu.PrefetchScalarGridSpec(
        num_scalar_prefetch=3,
        grid=(D // tn, E),
        in_specs=[
            pl.BlockSpec(memory_space=pl.ANY),
            pl.BlockSpec((None, FF, tn), lambda j, e, *_: (e, 0, j)),
            pl.BlockSpec((None, 1, tn), lambda j, e, *_: (e, 0, j)),
        ],
        out_specs=pl.BlockSpec(memory_space=pl.ANY),
        scratch_shapes=_expert_scratch(GROUP_SIZES[0], FF, h.dtype, tn // 2, jnp.uint32),
    )
    return pl.pallas_call(
        _down_kernel,
        out_shape=jax.ShapeDtypeStruct((rows, D // 2), jnp.uint32),
        grid_spec=grid_spec,
        compiler_params=_params(("arbitrary", "arbitrary")),
        name="down",
    )(row_start, row_count, _next_nonempty(row_count), h, w_down, b_down.reshape(E, 1, D))


def _combine_kernel(dest_ref, dnext_ref, hs_ref, gate_ref, nw_ref, y_hbm, o_ref, ybuf, sem, *, sub, half):
    s = pl.program_id(0)
    n = pl.num_programs(0)
    tm, D = hs_ref.shape
    W = ybuf.shape[3]
    slot = s % 2
    lanes = CHUNK
    n_chunks = W // lanes
    per_chunk = sub * TOP_K // n_chunks
    assert per_chunk * n_chunks == sub * TOP_K and half % lanes == 0

    def row_copy(d_ref, r, k, to_slot):
        return pltpu.make_async_copy(y_hbm.at[pl.ds(d_ref[k, r], 1)], ybuf.at[to_slot, k, pl.ds(r, 1)],
                                     sem.at[to_slot])

    def wait_slot(which):
        for k in range(TOP_K):
            pltpu.make_async_copy(y_hbm.at[pl.ds(0, tm)], ybuf.at[which, k], sem.at[which]).wait()

    @pl.when(s == 0)
    def _():
        def body(r, c):
            for k in range(TOP_K):
                row_copy(dest_ref, r, k, 0).start()
            return c
        lax.fori_loop(0, tm, body, 0)

    wait_slot(slot)

    def body(rb, c):
        row0 = pl.multiple_of(rb * sub, sub)
        rows = pl.ds(row0, sub)
        g = gate_ref[rows, :]
        gk = [jnp.broadcast_to(g[:, k:k + 1], (sub, lanes)) for k in range(TOP_K)]
        ssq = jnp.zeros((sub, lanes), F32)
        for ch in range(n_chunks):
            for i in range(ch * per_chunk, (ch + 1) * per_chunk):
                row_copy(dnext_ref, row0 + i // TOP_K, i % TOP_K, 1 - slot).start()
            j, within = divmod(ch * lanes, half)
            lo_cols = slice(2 * j * half + within, 2 * j * half + within + lanes)
            hi_cols = slice((2 * j + 1) * half + within, (2 * j + 1) * half + within + lanes)
            lo = hs_ref[rows, lo_cols]
            hi = hs_ref[rows, hi_cols]
            for k in range(TOP_K):
                word = ybuf[slot, k, rows, ch * lanes:(ch + 1) * lanes]
                lo = lo + lax.bitcast_convert_type(word << jnp.uint32(16), F32) * gk[k]
                hi = hi + lax.bitcast_convert_type(word & jnp.uint32(0xFFFF0000), F32) * gk[k]
            o_ref[rows, lo_cols] = lo
            o_ref[rows, hi_cols] = hi
            ssq = ssq + lo * lo + hi * hi
        ms = jnp.sum(ssq, axis=-1, keepdims=True) / D
        o_ref[rows, :] = o_ref[rows, :] * lax.rsqrt(ms + RMS_EPS) * nw_ref[...]
        return c
    lax.fori_loop(0, tm // sub, body, 0)

    @pl.when(s == n - 1)
    def _():
        wait_slot(1 - slot)


def _combine(hs1, gates_t, dest3, y, final_norm_w, *, B, LP, seq_len, tile, sub, half):
    T, D = hs1.shape
    W = y.shape[1]
    per_seq = LP // tile
    n_seq_tiles = seq_len // tile
    lead = (LP - seq_len) // tile
    n_steps = B * n_seq_tiles

    def tok(s):
        return (s // n_seq_tiles) * per_seq + lead + s % n_seq_tiles

    return pl.pallas_call(
        functools.partial(_combine_kernel, sub=sub, half=half),
        out_shape=jax.ShapeDtypeStruct((B, seq_len, D), F32),
        grid=(n_steps,),
        in_specs=[
            pl.BlockSpec((None, TOP_K, tile), lambda s: (tok(s), 0, 0), memory_space=pltpu.SMEM),
            pl.BlockSpec((None, TOP_K, tile), lambda s: (tok(jnp.minimum(s + 1, n_steps - 1)), 0, 0),
                         memory_space=pltpu.SMEM),
            pl.BlockSpec((tile, D), lambda s: (tok(s), 0)),
            pl.BlockSpec((tile, TOP_K), lambda s: (tok(s), 0)),
            pl.BlockSpec((1, D), lambda s: (0, 0)),
            pl.BlockSpec(memory_space=pl.ANY),
        ],
        out_specs=pl.BlockSpec((None, tile, D), lambda s: (s // n_seq_tiles, s % n_seq_tiles, 0)),
        scratch_shapes=[pltpu.VMEM((2, TOP_K, tile, W), y.dtype), pltpu.SemaphoreType.DMA((2,))],
        compiler_params=_params(("arbitrary",)),
        name="combine",
    )(dest3, dest3, hs1, gates_t, final_norm_w.reshape(1, D), y)


def _forward(x, meta_tokens, norm_mix_w, w_in, ret_log_rate_fwd, ret_log_rate_bwd, pool_w, pool_scale,
             w_out, norm_ffn_w, router_w, router_b, w_gate_up, b_gate_up, w_down, b_down, final_norm_w, tiles):
    B, S, D = x.shape
    depth = norm_mix_w.shape[0]
    assert depth == 1, "single-layer block"
    H = ret_log_rate_fwd.shape[-1]
    n_groups, cg = pool_w.shape[1], pool_w.shape[2]
    PW = n_groups * cg
    RW = w_out.shape[1] - PW
    d = RW // H
    E = router_w.shape[-1]
    assert w_in.shape[-1] == 4 * RW + PW and S % CHUNK == 0
    LP = PAD + N_META + S
    T = B * LP
    t = tiles

    lead = jnp.concatenate([jnp.zeros((PAD, D), x.dtype), meta_tokens.astype(x.dtype)], axis=0)
    hs0, a0 = _prep(x, lead, norm_mix_w[0], n_chunks=t["prep_chunks"], sub=t["norm_rows"])

    proj = _in_proj(a0, w_in[0].astype(BF16), tm=t["in_tm"], tn=t["in_tn"])
    pos = (jnp.arange(LP, dtype=jnp.int32) - PAD).astype(F32)
    inv_freq = ROPE_BASE ** (-jnp.arange(d // 2, dtype=F32) / (d // 2))
    ang = pos[:, None] * inv_freq[None, :]
    log_gamma = jnp.stack([-jnp.exp(ret_log_rate_fwd[0].astype(F32)), -jnp.exp(ret_log_rate_bwd[0].astype(F32))])
    ret = _retention(proj, log_gamma, jnp.cos(ang), jnp.sin(ang), B=B, LP=LP, H=H, d=d, GC=t["ret_group_chunks"],
                     HB=t["ret_heads_per_step"])
    pooled = _pool(proj, pool_w[0].astype(BF16), pool_scale[0].astype(F32), LP=LP, seq_len=N_META + S,
                   u_col0=4 * RW, tm=t["pool_tm"])
    hs1 = _out_proj(ret, pooled, w_out[0], hs0, tm=t["out_tm"], tn=t["out_tn"])

    row_id = jnp.arange(T, dtype=jnp.int32)
    is_token = (row_id % LP) >= PAD
    fp, eid, gates, rank, cnt = _router(hs1, norm_ffn_w[0], router_w[0], router_b[0], is_token,
                                        tm=t["router_tm"], sub=t["router_rows"])
    align = GROUP_SIZES[-1]
    tile = t["token_tile"]
    dtile = t["dispatch_tile"]
    counts = cnt[:, 0].astype(jnp.int32)
    padded = (counts + align - 1) // align * align
    pend = jnp.cumsum(padded).astype(jnp.int32)
    pstart = pend - padded
    n_assign = B * (N_META + S) * TOP_K
    rows = -(-(n_assign + E * (align - 1)) // align) * align
    n_used = pend[-1:] // align
    spare = rows + jnp.arange(TOP_K, dtype=jnp.int32)[:, None] * dtile + (row_id % dtile)[None, :]
    expert_ids = jnp.arange(E, dtype=jnp.int32)[:, None, None]
    first_row = jnp.sum(jnp.where(eid[None] == expert_ids, pstart[:, None, None], 0), axis=0)
    dest = jnp.where(is_token[None, :], first_row + rank, spare)

    def by_tile(width):
        return dest.reshape(TOP_K, T // width, width).transpose(1, 0, 2)
    dest3 = by_tile(tile)

    spare_rows = -(-(TOP_K * dtile) // align) * align
    xs = _dispatch(fp, by_tile(dtile), pend, padded, n_used, rows_total=rows + spare_rows, moe_tm=align, tile=dtile)
    hmid = _gate_up(xs, pstart, padded, w_gate_up[0], b_gate_up[0], rows=rows, tn=t["gu_tn"])
    y = _down(hmid, pstart, padded, w_down[0], b_down[0], tn=t["down_tn"])
    return _combine(hs1, gates.T, dest3, y, final_norm_w, B=B, LP=LP, seq_len=S, tile=tile, sub=t["combine_rows"],
                    half=t["down_tn"] // 2)


def kernel(x, meta_tokens, norm_mix_w, w_in, ret_log_rate_fwd, ret_log_rate_bwd, pool_w, pool_scale, w_out,
           norm_ffn_w, router_w, router_b, w_gate_up, b_gate_up, w_down, b_down, final_norm_w):
    return _forward(x, meta_tokens, norm_mix_w, w_in, ret_log_rate_fwd, ret_log_rate_bwd, pool_w, pool_scale,
                    w_out, norm_ffn_w, router_w, router_b, w_gate_up, b_gate_up, w_down, b_down, final_norm_w,
                    DEFAULT_TILES)
```
